```python
import jax, jax.numpy as jnp
from jax import lax
import numpy as np

D_MODEL = 2048
BATCH = 2
SEQ = 4096
DEPTH = 4
DEC_BATCH = 8
DEC_SEQ = 4
PAST_LEN = 16384
PAGE_SIZE = 128

N_ATT = (DEPTH + 1) // 2
N_RW = DEPTH // 2
ATT_HEADS = 8
ATT_HEAD_DIM = 128
ATT_WIDTH = ATT_HEADS * ATT_HEAD_DIM
IDX_HEADS = 16
IDX_DIM = 64
TOPK_MAX = 256
QBLOCK = 128
CONV_CH = D_MODEL // 2
CONV_W = 31
RW_HEAD = 64
RW_HEADS = D_MODEL // RW_HEAD
LORA_DECAY = 64
LORA_A = 64
LORA_GATE = 128
D_FF = 256 * ((8 * D_MODEL // 3 + 255) // 256)
ROPE_THETA = 10000.0
NORM_EPS = 1e-6
LN_EPS = 1e-5
LNX_EPS = 64e-5
N_MOD = 9
HALF_STEP = 0.5
IN_SPLITS = (ATT_WIDTH, 2 * ATT_WIDTH, 3 * ATT_WIDTH,
             3 * ATT_WIDTH + IDX_HEADS * IDX_DIM,
             3 * ATT_WIDTH + IDX_HEADS * IDX_DIM + IDX_DIM,
             3 * ATT_WIDTH + IDX_HEADS * IDX_DIM + IDX_DIM + IDX_HEADS)
IN_COLS = IN_SPLITS[-1] + 2 * CONV_CH

kernel_name = 'hybrid_dsa_conformer_rwkv7_step'


def rms_norm(x, g, eps=NORM_EPS):
    xf = x.astype(jnp.float32)
    y = xf * lax.rsqrt(jnp.mean(xf * xf, axis=-1, keepdims=True) + eps)
    return (y * g.astype(jnp.float32)).astype(x.dtype)


def layer_norm(x, g, b, eps=LN_EPS):
    xf = x.astype(jnp.float32)
    xc = xf - jnp.mean(xf, axis=-1, keepdims=True)
    var = jnp.mean(xc * xc, axis=-1, keepdims=True)
    return (xc * lax.rsqrt(var + eps) * g.astype(jnp.float32) + b.astype(jnp.float32)).astype(x.dtype)


def rope(x, pos):
    half = x.shape[-1] // 2
    inv = ROPE_THETA ** (-jnp.arange(half, dtype=jnp.float32) / half)
    ang = pos.astype(jnp.float32)[:, None] * inv[None, :]
    cos = jnp.cos(ang)[None, :, None, :]
    sin = jnp.sin(ang)[None, :, None, :]
    xf = x.astype(jnp.float32)
    x1, x2 = xf[..., :half], xf[..., half:]
    return jnp.concatenate([x1 * cos - x2 * sin, x2 * cos + x1 * sin], axis=-1).astype(x.dtype)


def take_rows(a, idx):
    return jax.vmap(lambda ab, ib: ab[ib])(a, idx)


def adaln(c, w, b):
    mod = jax.nn.silu(c) @ w + b
    return mod.reshape(c.shape[0], N_MOD, 1, D_MODEL)


def modnorm(x, g, shift, scale):
    return rms_norm(x, g) * (1 + scale) + shift


def swiglu(h, w1, w3, w2):
    return (jax.nn.silu(h @ w1) * (h @ w3)) @ w2


def macaron_pre(x, c, ng, aw, ab, w1, w3, w2):
    mod = adaln(c, aw, ab)
    x = x + HALF_STEP * mod[:, 2] * swiglu(modnorm(x, ng[0], mod[:, 0], mod[:, 1]), w1[0], w3[0], w2[0])
    h = modnorm(x, ng[1], mod[:, 3], mod[:, 4])
    return x, h, mod


def macaron_post(x, mixed, mod, ng, w1, w3, w2):
    x = x + mod[:, 5] * mixed
    return x + HALF_STEP * mod[:, 8] * swiglu(modnorm(x, ng[2], mod[:, 6], mod[:, 7]), w1[1], w3[1], w2[1])


def even_project(h, pos, w_in, q_g, k_g, ik_g):
    B, T, _ = h.shape
    q, k, v, iq, ik, iw, u = jnp.split(h @ w_in, list(IN_SPLITS), axis=-1)
    q = rope(rms_norm(q.reshape(B, T, ATT_HEADS, ATT_HEAD_DIM), q_g), pos)
    k = rope(rms_norm(k.reshape(B, T, ATT_HEADS, ATT_HEAD_DIM), k_g), pos)
    v = v.reshape(B, T, ATT_HEADS, ATT_HEAD_DIM)
    iq = rope(iq.reshape(B, T, IDX_HEADS, IDX_DIM), pos)
    ik = rope(rms_norm(ik, ik_g)[:, :, None, :], pos)[:, :, 0, :]
    iw = iw * IDX_HEADS ** -0.5
    u_val, u_gate = jnp.split(u, 2, axis=-1)
    return q, k, v, iq, ik, iw, u_val * jax.nn.sigmoid(u_gate)


def index_select(iq, iw, ik_all, qpos, top_k):
    L = ik_all.shape[1]
    dots = jnp.einsum('bqhd,bsd->bqhs', iq.astype(jnp.float32), ik_all.astype(jnp.float32)) * IDX_DIM ** -0.5
    score = jnp.einsum('bqh,bqhs->bqs', iw.astype(jnp.float32), jax.nn.relu(dots))
    admissible = jnp.arange(L)[None, :] <= qpos[:, None]
    score = jnp.where(admissible[None], score, -jnp.inf)
    _, sel = lax.top_k(score, top_k)
    valid = sel <= qpos[None, :, None]
    return sel, valid


def sparse_attend(q, kg, vg, valid):
    s = jnp.einsum('bqhd,bqkhd->bhqk', q, kg).astype(jnp.float32) * ATT_HEAD_DIM ** -0.5
    s = jnp.where(valid[:, None], s, -jnp.inf)
    p = jax.nn.softmax(s, axis=-1).astype(vg.dtype)
    return jnp.einsum('bhqk,bqkhd->bqhd', p, vg)


def dsa_prompt(q, k, v, iq, ik, iw):
    B, T = q.shape[:2]
    top_k = min(TOPK_MAX, T // 4)

    def block(start):
        qb = lax.dynamic_slice_in_dim(q, start, QBLOCK, axis=1)
        iqb = lax.dynamic_slice_in_dim(iq, start, QBLOCK, axis=1)
        iwb = lax.dynamic_slice_in_dim(iw, start, QBLOCK, axis=1)
        qpos = start + jnp.arange(QBLOCK)
        sel, valid = index_select(iqb, iwb, ik, qpos, top_k)
        return sparse_attend(qb, take_rows(k, sel), take_rows(v, sel), valid)

    out = lax.map(block, jnp.arange(T // QBLOCK) * QBLOCK)
    return jnp.moveaxis(out, 0, 1).reshape(B, T, ATT_HEADS, ATT_HEAD_DIM)


def dsa_sample(q, k_new, v_new, iq, ik_new, iw, cache_k, cache_v, cache_idx_k, layer, page_table):
    DB, DS = q.shape[:2]
    past = page_table.shape[1] * PAGE_SIZE
    top_k = min(TOPK_MAX, (past + DS) // 4)
    ik_past = cache_idx_k[layer, page_table].reshape(DB, past, IDX_DIM)
    ik_all = jnp.concatenate([ik_past.astype(ik_new.dtype), ik_new], axis=1)
    qpos = past + jnp.arange(DS)
    sel, valid = index_select(iq, iw, ik_all, qpos, top_k)
    in_past = (sel < past)[..., None, None]
    phys = take_rows(page_table, jnp.minimum(sel, past - 1) // PAGE_SIZE)
    off = sel % PAGE_SIZE
    new_idx = jnp.clip(sel - past, 0, DS - 1)
    kg = jnp.where(in_past, cache_k[layer, phys, off].astype(k_new.dtype), take_rows(k_new, new_idx))
    vg = jnp.where(in_past, cache_v[layer, phys, off].astype(v_new.dtype), take_rows(v_new, new_idx))
    return sparse_attend(q, kg, vg, valid)


def even_output(att, conv_in, w_out, cw, cb, ln_g, ln_b):
    B, T = att.shape[:2]
    y = lax.conv_general_dilated(conv_in, cw[:, None, :].astype(conv_in.dtype), (1,), 'VALID',
                                 dimension_numbers=('NWC', 'WIO', 'NWC'), feature_group_count=CONV_CH)
    y = jax.nn.silu(layer_norm(y + cb, ln_g, ln_b))
    return jnp.concatenate([att.reshape(B, T, ATT_WIDTH), y.astype(att.dtype)], axis=-1) @ w_out


def head_group_norm(y, g, b):
    mu = jnp.mean(y, axis=-1, keepdims=True)
    yc = y - mu
    var = jnp.mean(yc * yc, axis=-1, keepdims=True)
    yn = yc * lax.rsqrt(var + LNX_EPS)
    return yn * g.astype(jnp.float32).reshape(RW_HEADS, RW_HEAD) + b.astype(jnp.float32).reshape(RW_HEADS, RW_HEAD)


def wkv7_scan(r, w, k, v, kk, a, S0):
    def step(S, inp):
        r_t, w_t, k_t, v_t, kk_t, a_t = inp
        sa = jnp.einsum('bhij,bhj->bhi', S, kk_t)
        S = S * w_t[:, :, None, :] - sa[..., None] * (kk_t * a_t)[:, :, None, :] + v_t[..., None] * k_t[:, :, None, :]
        return S, jnp.einsum('bhij,bhj->bhi', S, r_t)

    xs = tuple(jnp.moveaxis(t, 1, 0) for t in (r, w, k, v, kk, a))
    S, y = lax.scan(step, S0, xs)
    return jnp.moveaxis(y, 0, 1), S


def rwkv_mix(h, shift_prev, S0, mu, w0, w1, w2, a0, a1, a2, g1, g2, k_k, k_a, r_k,
             wr, wk, wv, wo, lnx_g, lnx_b):
    B, T, _ = h.shape
    f32 = jnp.float32
    h_prev = jnp.concatenate([shift_prev[:, None].astype(h.dtype), h[:, :-1]], axis=1)
    xx = h_prev - h
    xr, xw, xk, xv, xa, xg = [h + xx * mu[j] for j in range(6)]
    r = xr @ wr
    k = xk @ wk
    v = xv @ wv
    w = -jax.nn.softplus(-(w0 + jnp.tanh(xw @ w1) @ w2)) - 0.5
    a = jax.nn.sigmoid(a0 + (xa @ a1) @ a2)
    g = jax.nn.sigmoid(xg @ g1) @ g2

    def heads(t):
        return t.astype(f32).reshape(B, T, RW_HEADS, RW_HEAD)

    kk = heads(k * k_k)
    kk = kk * lax.rsqrt(jnp.maximum(jnp.sum(kk * kk, axis=-1, keepdims=True), 1e-24))
    ah = heads(a)
    kh = heads(k) * (1 + (ah - 1) * k_a.astype(f32).reshape(RW_HEADS, RW_HEAD))
    decay = jnp.exp(-jnp.exp(heads(w)))
    rh, vh = heads(r), heads(v)
    y, S = wkv7_scan(rh, decay, kh, vh, kk, ah, S0.astype(f32))
    y = head_group_norm(y, lnx_g, lnx_b)
    y = y + jnp.sum(rh * kh * r_k.astype(f32), axis=-1, keepdims=True) * vh
    out = (y.reshape(B, T, D_MODEL).astype(h.dtype) * g) @ wo
    return out, h[:, -1], S


def setup_inputs(seed: int = 0) -> dict:
    key = jax.random.key(seed)
    keys = iter(jax.random.split(key, 64))
    f32 = jnp.float32
    D = D_MODEL

    def nrm(shape, scale):
        return jax.random.normal(next(keys), shape, f32) * scale

    def gain(shape):
        return 1.0 + nrm(shape, 0.02)

    n_pages = PAST_LEN // PAGE_SIZE
    n_used = DEC_BATCH * n_pages
    n_pool = n_used + max(1, n_used // 4)
    page_table = jax.random.permutation(next(keys), n_pool)[:n_used].reshape(DEC_BATCH, n_pages).astype(jnp.int32)
    return {
        'x_prompt': nrm((BATCH, SEQ, D), 1.0),
        'x_sample': nrm((DEC_BATCH, DEC_SEQ, D), 1.0),
        'cache_k': nrm((N_ATT, n_pool, PAGE_SIZE, ATT_HEADS, ATT_HEAD_DIM), 1.0),
        'cache_v': nrm((N_ATT, n_pool, PAGE_SIZE, ATT_HEADS, ATT_HEAD_DIM), 1.0),
        'cache_idx_k': nrm((N_ATT, n_pool, PAGE_SIZE, IDX_DIM), 1.0),
        'state_conv': nrm((N_ATT, DEC_BATCH, CONV_W - 1, CONV_CH), 0.5),
        'state_shift': nrm((N_RW, DEC_BATCH, D), 1.0),
        'state_wkv': nrm((N_RW, DEC_BATCH, RW_HEADS, RW_HEAD, RW_HEAD), 0.3),
        'page_table': page_table,
        'c_prompt': nrm((BATCH, D), 1.0),
        'c_sample': nrm((DEC_BATCH, D), 1.0),
        'norm_g': gain((DEPTH, 3, D)),
        'ada_w': nrm((DEPTH, D, N_MOD * D), 0.3 * D ** -0.5),
        'ada_b': nrm((DEPTH, N_MOD * D), 0.02),
        'ffn_w1': nrm((DEPTH, 2, D, D_FF), D ** -0.5),
        'ffn_w3': nrm((DEPTH, 2, D, D_FF), D ** -0.5),
        'ffn_w2': nrm((DEPTH, 2, D_FF, D), D_FF ** -0.5),
        'att_w_in': nrm((N_ATT, D, IN_COLS), D ** -0.5),
        'att_w_out': nrm((N_ATT, ATT_WIDTH + CONV_CH, D), (ATT_WIDTH + CONV_CH) ** -0.5),
        'q_norm_g': gain((N_ATT, ATT_HEAD_DIM)),
        'k_norm_g': gain((N_ATT, ATT_HEAD_DIM)),
        'idx_k_norm_g': gain((N_ATT, IDX_DIM)),
        'conv_w': nrm((N_ATT, CONV_W, CONV_CH), CONV_W ** -0.5),
        'conv_b': nrm((N_ATT, CONV_CH), 0.02),
        'conv_ln_g': gain((N_ATT, CONV_CH)),
        'conv_ln_b': nrm((N_ATT, CONV_CH), 0.02),
        'rw_mu': jax.random.uniform(next(keys), (N_RW, 6, D), f32),
        'rw_w0': jax.random.uniform(next(keys), (N_RW, D), f32, -6.0, 1.0),
        'rw_w1': nrm((N_RW, D, LORA_DECAY), 0.5 * D ** -0.5),
        'rw_w2': nrm((N_RW, LORA_DECAY, D), 0.5 * LORA_DECAY ** -0.5),
        'rw_a0': nrm((N_RW, D), 0.1),
        'rw_a1': nrm((N_RW, D, LORA_A), D ** -0.5),
        'rw_a2': nrm((N_RW, LORA_A, D), 0.5 * LORA_A ** -0.5),
        'rw_g1': nrm((N_RW, D, LORA_GATE), D ** -0.5),
        'rw_g2': nrm((N_RW, LORA_GATE, D), LORA_GATE ** -0.5),
        'rw_k_k': 0.85 + nrm((N_RW, D), 0.02),
        'rw_k_a': gain((N_RW, D)),
        'rw_r_k': nrm((N_RW, RW_HEADS, RW_HEAD), 0.1),
        'rw_wr': nrm((N_RW, D, D), D ** -0.5),
        'rw_wk': nrm((N_RW, D, D), D ** -0.5),
        'rw_wv': nrm((N_RW, D, D), D ** -0.5),
        'rw_wo': nrm((N_RW, D, D), D ** -0.5),
        'rw_lnx_g': gain((N_RW, D)),
        'rw_lnx_b': nrm((N_RW, D), 0.02),
    }


def reference(x_prompt, x_sample, cache_k, cache_v, cache_idx_k, state_conv, state_shift, state_wkv,
              page_table, c_prompt, c_sample, norm_g, ada_w, ada_b, ffn_w1, ffn_w3, ffn_w2,
              att_w_in, att_w_out, q_norm_g, k_norm_g, idx_k_norm_g, conv_w, conv_b, conv_ln_g, conv_ln_b,
              rw_mu, rw_w0, rw_w1, rw_w2, rw_a0, rw_a1, rw_a2, rw_g1, rw_g2, rw_k_k, rw_k_a, rw_r_k,
              rw_wr, rw_wk, rw_wv, rw_wo, rw_lnx_g, rw_lnx_b):
    B, T = x_prompt.shape[:2]
    DB, DS = x_sample.shape[:2]
    past = page_table.shape[1] * PAGE_SIZE
    pos_p = jnp.arange(T)
    pos_s = past + jnp.arange(DS)
    xp, xs = x_prompt, x_sample
    pk, pv, pik, pconv, pshift, pwkv = [], [], [], [], [], []
    sk, sv, sik, sconv, sshift, swkv = [], [], [], [], [], []
    for l in range(DEPTH):
        i = l // 2
        ffn_l = (ffn_w1[l], ffn_w3[l], ffn_w2[l])
        xp, hp, mp = macaron_pre(xp, c_prompt, norm_g[l], ada_w[l], ada_b[l], *ffn_l)
        xs, hs, ms = macaron_pre(xs, c_sample, norm_g[l], ada_w[l], ada_b[l], *ffn_l)
        if l % 2 == 0:
            proj_w = (att_w_in[i], q_norm_g[i], k_norm_g[i], idx_k_norm_g[i])
            out_w = (att_w_out[i], conv_w[i], conv_b[i], conv_ln_g[i], conv_ln_b[i])
            q, k, v, iq, ik, iw, u = even_project(hp, pos_p, *proj_w)
            att = dsa_prompt(q, k, v, iq, ik, iw)
            conv_in = jnp.pad(u, ((0, 0), (CONV_W - 1, 0), (0, 0)))
            mixed_p = even_output(att, conv_in, *out_w)
            pk.append(k)
            pv.append(v)
            pik.append(ik)
            pconv.append(conv_in[:, -(CONV_W - 1):])
            q, k, v, iq, ik, iw, u = even_project(hs, pos_s, *proj_w)
            att = dsa_sample(q, k, v, iq, ik, iw, cache_k, cache_v, cache_idx_k, i, page_table)
            conv_in = jnp.concatenate([state_conv[i].astype(u.dtype), u], axis=1)
            mixed_s = even_output(att, conv_in, *out_w)
            sk.append(k)
            sv.append(v)
            sik.append(ik)
            sconv.append(conv_in[:, -(CONV_W - 1):])
        else:
            rw = (rw_mu[i], rw_w0[i], rw_w1[i], rw_w2[i], rw_a0[i], rw_a1[i], rw_a2[i], rw_g1[i], rw_g2[i],
                  rw_k_k[i], rw_k_a[i], rw_r_k[i], rw_wr[i], rw_wk[i], rw_wv[i], rw_wo[i], rw_lnx_g[i], rw_lnx_b[i])
            mixed_p, sh_p, S_p = rwkv_mix(hp, jnp.zeros((B, D_MODEL), hp.dtype),
                                          jnp.zeros((B, RW_HEADS, RW_HEAD, RW_HEAD), jnp.float32), *rw)
            mixed_s, sh_s, S_s = rwkv_mix(hs, state_shift[i], state_wkv[i], *rw)
            pshift.append(sh_p)
            pwkv.append(S_p)
            sshift.append(sh_s)
            swkv.append(S_s)
        xp = macaron_post(xp, mixed_p, mp, norm_g[l], *ffn_l)
        xs = macaron_post(xs, mixed_s, ms, norm_g[l], *ffn_l)
    return (xp, xs,
            jnp.stack(pk), jnp.stack(pv), jnp.stack(pik), jnp.stack(pconv), jnp.stack(pshift), jnp.stack(pwkv),
            jnp.stack(sk), jnp.stack(sv), jnp.stack(sik), jnp.stack(sconv), jnp.stack(sshift), jnp.stack(swkv))
```

```python
import functools
import math

import numpy as np
import jax
import jax.numpy as jnp
from jax import lax
from jax.experimental import pallas as pl
from jax.experimental.pallas import tpu as pltpu

F32 = jnp.float32
BF16 = jnp.bfloat16
I32 = jnp.int32

ROPE_THETA = 10000.0
NORM_EPS = 1e-6
LN_EPS = 1e-5
LNX_EPS = 64e-5
N_MOD = 9
HALF_STEP = 0.5
TOPK_MAX = 256
QBLOCK = 128
IDX_HEADS = 16

LANES = 128
SUBLANES = 8
VMEM_LIMIT_MB = 56
NEG_BIG = -1e30
INT_MIN = -(2 ** 31)

SAMPLE_T_PAD = 8
PAGES_PER_STEP = 4
IDX_PAGES_PER_STEP = 8


def _cparams(sem, vmem_mb=VMEM_LIMIT_MB):
    return pltpu.CompilerParams(dimension_semantics=sem, vmem_limit_bytes=vmem_mb * 1024 * 1024)


def _silu(x):
    return x * jax.nn.sigmoid(x)


def _dot(a, b):
    return jnp.dot(a, b, preferred_element_type=F32)


def _dot_nt(a, b):
    return lax.dot_general(a, b, (((1,), (1,)), ((), ())), preferred_element_type=F32)


def _row_tile(t, target):
    return t if t <= target else target


def _adaln_kernel(c_ref, w_ref, b_ref, o_ref):
    sc = _silu(c_ref[...]).astype(BF16)
    o_ref[...] = _dot(sc, w_ref[...].astype(BF16)) + b_ref[...]


def adaln_all(c_all, ada_w, ada_b):
    depth, d, n = ada_w.shape
    rows = c_all.shape[0]
    tn = 1024
    return pl.pallas_call(
        _adaln_kernel,
        grid=(depth, n // tn),
        in_specs=[
            pl.BlockSpec((rows, d), lambda l, j: (0, 0)),
            pl.BlockSpec((None, d, tn), lambda l, j: (l, 0, j)),
            pl.BlockSpec((None, 1, tn), lambda l, j: (l, 0, j)),
        ],
        out_specs=pl.BlockSpec((None, rows, tn), lambda l, j: (l, 0, j)),
        out_shape=jax.ShapeDtypeStruct((depth, rows, n), F32),
        compiler_params=_cparams(("arbitrary", "arbitrary")),
        name="adaln",
    )(c_all, ada_w, ada_b.reshape(depth, 1, n))


def _resid_norm_kernel(*refs, has_y, coef, emit_x, emit_h):
    refs = list(refs)
    x = refs.pop(0)[...]
    if has_y:
        y = refs.pop(0)[...]
        gate = refs.pop(0)[...]
        x = x + (coef * gate) * y
    if emit_h:
        g = refs.pop(0)[...]
        shift = refs.pop(0)[...]
        scale = refs.pop(0)[...]
    if emit_x:
        refs.pop(0)[...] = x
    if emit_h:
        h_ref = refs.pop(0)
        ms = jnp.mean(x * x, axis=-1, keepdims=True)
        h = x * lax.rsqrt(ms + NORM_EPS) * g
        h_ref[...] = (h * (1.0 + scale) + shift).astype(h_ref.dtype)


def resid_norm(x, mod, *, y=None, gate_idx=None, coef=1.0, g=None, shift_idx=None, scale_idx=None,
               emit_x=True, h_dtype=None, mod_norm=None):
    mod_norm = mod if mod_norm is None else mod_norm
    b, t, d = x.shape
    tt = _row_tile(t, 256)
    has_y = y is not None
    emit_h = h_dtype is not None
    row = pl.BlockSpec((None, tt, d), lambda bi, i: (bi, i, 0))

    def mod_spec(idx):
        return pl.BlockSpec((None, None, 1, d), lambda bi, i: (bi, idx, 0, 0))

    args, specs = [x], [row]
    if has_y:
        args += [y, mod]
        specs += [row, mod_spec(gate_idx)]
    if emit_h:
        args += [g.reshape(1, d), mod_norm, mod_norm]
        specs += [pl.BlockSpec((1, d), lambda bi, i: (0, 0)), mod_spec(shift_idx), mod_spec(scale_idx)]
    out_shape, out_specs = [], []
    if emit_x:
        out_shape.append(jax.ShapeDtypeStruct((b, t, d), F32))
        out_specs.append(row)
    if emit_h:
        out_shape.append(jax.ShapeDtypeStruct((b, t, d), h_dtype))
        out_specs.append(row)
    outs = pl.pallas_call(
        functools.partial(_resid_norm_kernel, has_y=has_y, coef=coef, emit_x=emit_x, emit_h=emit_h),
        grid=(b, t // tt),
        in_specs=specs,
        out_specs=out_specs,
        out_shape=out_shape,
        compiler_params=_cparams(("arbitrary", "arbitrary")),
        name="resid_norm",
    )(*args)
    return outs if len(outs) > 1 else outs[0]


def _ffn_kernel(h_ref, w1_ref, w3_ref, w2_ref, o_ref):
    f = pl.program_id(1)
    h = h_ref[...]
    a = _dot(h, w1_ref[...].astype(BF16))
    b = _dot(h, w3_ref[...].astype(BF16))
    z = (_silu(a) * b).astype(BF16)
    contrib = _dot(z, w2_ref[...].astype(BF16))

    @pl.when(f == 0)
    def _():
        o_ref[...] = contrib

    @pl.when(f != 0)
    def _():
        o_ref[...] += contrib


def ffn(h, w1, w3, w2, layer, slot):
    b, t, d = h.shape
    m = b * t
    d_ff = w1.shape[-1]
    tm = 1024 if m >= 1024 else m
    tf = 256 if m >= 1024 else 512
    out = pl.pallas_call(
        _ffn_kernel,
        grid=(m // tm, d_ff // tf),
        in_specs=[
            pl.BlockSpec((tm, d), lambda i, f: (i, 0)),
            pl.BlockSpec((None, None, d, tf), lambda i, f: (layer, slot, 0, f)),
            pl.BlockSpec((None, None, d, tf), lambda i, f: (layer, slot, 0, f)),
            pl.BlockSpec((None, None, tf, d), lambda i, f: (layer, slot, f, 0)),
        ],
        out_specs=pl.BlockSpec((tm, d), lambda i, f: (i, 0)),
        out_shape=jax.ShapeDtypeStruct((m, d), F32),
        compiler_params=_cparams(("arbitrary", "arbitrary")),
        name="ffn",
    )(h.reshape(m, d), w1, w3, w2)
    return out.reshape(b, t, d)


def _mm_kernel(x_ref, w_ref, o_ref):
    o_ref[...] = _dot(x_ref[...], w_ref[...].astype(BF16)).astype(o_ref.dtype)


def matmul(x, w, lead=None, n_cols=None, out_dtype=F32):
    b, t, k = x.shape
    m = b * t
    n = w.shape[-1] if n_cols is None else n_cols
    tm = 1024 if m >= 1024 else m
    tn = 512 if n % 512 == 0 else n
    if lead is None:
        w_spec = pl.BlockSpec((k, tn), lambda i, j: (0, j))
    else:
        w_spec = pl.BlockSpec((None, k, tn), lambda i, j: (lead, 0, j))
    out = pl.pallas_call(
        _mm_kernel,
        grid=(m // tm, n // tn),
        in_specs=[pl.BlockSpec((tm, k), lambda i, j: (i, 0)), w_spec],
        out_specs=pl.BlockSpec((tm, tn), lambda i, j: (i, j)),
        out_shape=jax.ShapeDtypeStruct((m, n), out_dtype),
        compiler_params=_cparams(("arbitrary", "arbitrary")),
        name="matmul",
    )(x.reshape(m, k), w)
    return out.reshape(b, t, n)


def _lora_kernel(x_ref, a_ref, b_ref, o_ref, *, act):
    h = _dot(x_ref[...], a_ref[...].astype(BF16))
    if act == "tanh":
        h = jnp.tanh(h)
    elif act == "sigmoid":
        h = jax.nn.sigmoid(h)
    o_ref[...] = _dot(h.astype(BF16), b_ref[...].astype(BF16))


def lora(x, a, bmat, layer, act):
    b, t, d = x.shape
    m = b * t
    r = a.shape[-1]
    n = bmat.shape[-1]
    tm = 512 if m >= 512 else m
    out = pl.pallas_call(
        functools.partial(_lora_kernel, act=act),
        grid=(m // tm,),
        in_specs=[
            pl.BlockSpec((tm, d), lambda i: (i, 0)),
            pl.BlockSpec((None, d, r), lambda i: (layer, 0, 0)),
            pl.BlockSpec((None, r, n), lambda i: (layer, 0, 0)),
        ],
        out_specs=pl.BlockSpec((tm, n), lambda i: (i, 0)),
        out_shape=jax.ShapeDtypeStruct((m, n), F32),
        compiler_params=_cparams(("arbitrary",)),
        name="lora",
    )(x.reshape(m, d), a, bmat)
    return out.reshape(b, t, n)


def _rope_tables(positions, head_dim):
    half = head_dim // 2
    inv = ROPE_THETA ** (-np.arange(half, dtype=np.float64) / half)
    ang = np.asarray(positions, np.float64)[:, None] * inv[None, :]
    cos = np.concatenate([np.cos(ang), np.cos(ang)], axis=-1)
    sin = np.concatenate([-np.sin(ang), np.sin(ang)], axis=-1)
    reps = LANES // head_dim
    return (jnp.asarray(np.tile(cos, (1, reps)), F32), jnp.asarray(np.tile(sin, (1, reps)), F32))


def _rope128(x, cos, sin):
    return x * cos + pltpu.roll(x, 64, 1) * sin


def _rope64(x, cos, sin, lane):
    first_half = (lane % 64) < 32
    partner = jnp.where(first_half, pltpu.roll(x, 96, 1), pltpu.roll(x, 32, 1))
    return x * cos + partner * sin


def _even_post_kernel(q_ref, k_ref, v_ref, iq_ref, small_ref, u_ref,
                      c128_ref, s128_ref, c64_ref, s64_ref, qg_ref, kg_ref, ikg_ref,
                      qo_ref, kf_ref, kb_ref, vb_ref, iqo_ref, smallo_ref, ikd_ref, uo_ref,
                      *, att_heads, idx_dim, idx_heads):
    c128, s128 = c128_ref[...], s128_ref[...]
    c64, s64 = c64_ref[...], s64_ref[...]
    tt = c128.shape[0]
    lane = lax.broadcasted_iota(I32, (tt, LANES), 1)
    qg, kg = qg_ref[...], kg_ref[...]
    for h in range(att_heads):
        sl = slice(h * LANES, (h + 1) * LANES)
        q = q_ref[:, sl]
        q = q * lax.rsqrt(jnp.mean(q * q, axis=-1, keepdims=True) + NORM_EPS) * qg
        qo_ref[:, sl] = _rope128(q, c128, s128).astype(BF16)
        k = k_ref[:, sl]
        k = k * lax.rsqrt(jnp.mean(k * k, axis=-1, keepdims=True) + NORM_EPS) * kg
        k = _rope128(k, c128, s128)
        kf_ref[:, sl] = k
        kb_ref[:, sl] = k.astype(BF16)
        vb_ref[:, sl] = v_ref[:, sl].astype(BF16)
    for p in range(idx_heads * idx_dim // LANES):
        sl = slice(p * LANES, (p + 1) * LANES)
        iqo_ref[:, sl] = _rope64(iq_ref[:, sl], c64, s64, lane).astype(BF16)
    small = small_ref[...]
    is_ik = lane < idx_dim
    ik = jnp.where(is_ik, small, 0.0)
    ms = jnp.sum(ik * ik, axis=-1, keepdims=True) * (1.0 / idx_dim)
    ik = _rope64(ik * lax.rsqrt(ms + NORM_EPS) * ikg_ref[...], c64, s64, lane)
    ik = jnp.where(is_ik, ik, 0.0)
    smallo_ref[...] = jnp.where(is_ik, ik, small * (idx_heads ** -0.5))
    ikd_ref[...] = (ik + pltpu.roll(ik, 64, 1)).astype(BF16)
    cc = u_ref.shape[-1] // 2
    uo_ref[...] = u_ref[:, :cc] * jax.nn.sigmoid(u_ref[:, cc:])


def even_post(qkvi, small, u, tables, q_g, k_g, ik_g, att_heads, idx_dim):
    b, t, _ = qkvi.shape
    aw = att_heads * LANES
    iw = IDX_HEADS * idx_dim
    cc = u.shape[-1] // 2
    tt = _row_tile(t, 256)
    c128, s128, c64, s64 = tables

    def row(width, col=0):
        return pl.BlockSpec((None, tt, width), lambda bi, i: (bi, i, col))

    tab = pl.BlockSpec((tt, LANES), lambda bi, i: (i, 0))
    vec = pl.BlockSpec((1, LANES), lambda bi, i: (0, 0))
    ikg_pad = jnp.zeros((1, LANES), F32).at[0, :idx_dim].set(ik_g)
    assert aw == iw, "q/k/v/indexer-q column groups are addressed as equal-width blocks"
    return pl.pallas_call(
        functools.partial(_even_post_kernel, att_heads=att_heads, idx_dim=idx_dim, idx_heads=IDX_HEADS),
        grid=(b, t // tt),
        in_specs=[row(aw, 0), row(aw, 1), row(aw, 2), row(iw, 3), row(LANES), row(2 * cc),
                  tab, tab, tab, tab, vec, vec, vec],
        out_specs=[row(aw), row(aw), row(aw), row(aw), row(iw), row(LANES), row(LANES), row(cc)],
        out_shape=[
            jax.ShapeDtypeStruct((b, t, aw), BF16),
            jax.ShapeDtypeStruct((b, t, aw), F32),
            jax.ShapeDtypeStruct((b, t, aw), BF16),
            jax.ShapeDtypeStruct((b, t, aw), BF16),
            jax.ShapeDtypeStruct((b, t, iw), BF16),
            jax.ShapeDtypeStruct((b, t, LANES), F32),
            jax.ShapeDtypeStruct((b, t, LANES), BF16),
            jax.ShapeDtypeStruct((b, t, cc), F32),
        ],
        compiler_params=_cparams(("arbitrary", "arbitrary")),
        name="even_post",
    )(qkvi, qkvi, qkvi, qkvi, small, u, c128, s128, c64, s64,
      q_g.reshape(1, LANES), k_g.reshape(1, LANES), ikg_pad)


def _order_key(score):
    bits = pltpu.bitcast(score, I32)
    return jnp.where(bits < 0, bits ^ 0x7FFFFFFF, bits)


def _kth_largest_key(count_ge, top_k, shape):
    tau = jnp.where(count_ge(jnp.zeros(shape, I32)) >= top_k, 0, INT_MIN).astype(I32)

    def body(i, tau):
        cand = tau | jnp.left_shift(jnp.int32(1), 30 - i)
        return jnp.where(count_ge(cand) >= top_k, cand, tau)

    return lax.fori_loop(0, 31, body, tau)


def _idx_head_lhs(iq, h, lane):
    slab = iq[:, (h // 2) * LANES:(h // 2 + 1) * LANES]
    keep = (lane < 64) if h % 2 == 0 else (lane >= 64)
    return jnp.where(keep, slab, jnp.zeros_like(slab))


def _idx_weight(small, h, lane, idx_dim):
    return jnp.sum(jnp.where(lane == idx_dim + h, small, 0.0), axis=-1, keepdims=True)


def _dsa_prompt_kernel(q_ref, iq_ref, small_ref, k_ref, v_ref, ikd_ref, o_ref,
                       *, top_k, att_heads, idx_dim):
    n = pl.program_id(1)
    t_all = k_ref.shape[0]
    qb = q_ref.shape[0]
    lane = lax.broadcasted_iota(I32, (qb, LANES), 1)
    iq = iq_ref[...]
    small = small_ref[...]
    ikd = ikd_ref[...]
    score = jnp.zeros((qb, t_all), F32)
    for h in range(IDX_HEADS):
        d = _dot_nt(_idx_head_lhs(iq, h, lane), ikd) * (idx_dim ** -0.5)
        score = score + _idx_weight(small, h, lane, idx_dim) * jnp.maximum(d, 0.0)
    col = lax.broadcasted_iota(I32, (qb, t_all), 1)
    rowpos = lax.broadcasted_iota(I32, (qb, t_all), 0) + n * qb
    admissible = col <= rowpos
    key = jnp.where(admissible, _order_key(score), INT_MIN)

    def count_ge(cand):
        return jnp.sum((key >= cand).astype(I32), axis=-1, keepdims=True)

    tau = _kth_largest_key(count_ge, top_k, (qb, 1))
    sel = (key >= tau) & admissible
    q = q_ref[...]
    scale = LANES ** -0.5
    for h in range(att_heads):
        sl = slice(h * LANES, (h + 1) * LANES)
        s = _dot_nt(q[:, sl], k_ref[:, sl]) * scale
        s = jnp.where(sel, s, NEG_BIG)
        m = jnp.max(s, axis=-1, keepdims=True)
        p = jnp.where(sel, jnp.exp(s - m), 0.0)
        l = jnp.sum(p, axis=-1, keepdims=True)
        o = _dot(p.astype(BF16), v_ref[:, sl]) / l
        o_ref[:, sl] = o.astype(o_ref.dtype)


def dsa_prompt(q_bf, iq_bf, small, k_bf, v_bf, ikd_bf, idx_dim):
    b, t, aw = q_bf.shape
    top_k = min(TOPK_MAX, t // 4)
    qrow = lambda w: pl.BlockSpec((None, QBLOCK, w), lambda bi, n: (bi, n, 0))
    full = lambda w: pl.BlockSpec((None, t, w), lambda bi, n: (bi, 0, 0))
    return pl.pallas_call(
        functools.partial(_dsa_prompt_kernel, top_k=top_k, att_heads=aw // LANES, idx_dim=idx_dim),
        grid=(b, t // QBLOCK),
        in_specs=[qrow(aw), qrow(iq_bf.shape[-1]), qrow(LANES), full(aw), full(aw), full(LANES)],
        out_specs=qrow(aw),
        out_shape=jax.ShapeDtypeStruct((b, t, aw), BF16),
        compiler_params=_cparams(("arbitrary", "arbitrary")),
        name="dsa_prompt",
    )(q_bf, iq_bf, small, k_bf, v_bf, ikd_bf)


def _idx_lhs_all_heads(iq, lane):
    iq = iq.astype(F32)
    return jnp.concatenate([_idx_head_lhs(iq, h, lane) for h in range(IDX_HEADS)], axis=0).astype(BF16)


def _idx_scores(lhs, small, keys_dup, lane, idx_dim):
    d = _dot_nt(lhs, keys_dup) * (idx_dim ** -0.5)
    score = jnp.zeros((SAMPLE_T_PAD, keys_dup.shape[0]), F32)
    for h in range(IDX_HEADS):
        dh = d[h * SAMPLE_T_PAD:(h + 1) * SAMPLE_T_PAD, :]
        score = score + _idx_weight(small, h, lane, idx_dim) * jnp.maximum(dh, 0.0)
    return score


def _sample_idx_kernel(pt_ref, iq_ref, small_ref, *rest, idx_dim):
    page_refs, o_ref = rest[:-1], rest[-1]
    lane = lax.broadcasted_iota(I32, (SAMPLE_T_PAD, LANES), 1)
    keys = jnp.concatenate([r[...] for r in page_refs], axis=0)
    keys_dup = jnp.concatenate([keys, keys], axis=-1).astype(BF16)
    lhs = _idx_lhs_all_heads(iq_ref[...], lane)
    o_ref[...] = _idx_scores(lhs, small_ref[...], keys_dup, lane, idx_dim)


def sample_idx_scores(page_table, iq_bf, small, cache_idx_k, layer):
    b, n_pages = page_table.shape
    page, idx_dim = cache_idx_k.shape[-2:]
    g = IDX_PAGES_PER_STEP if n_pages % IDX_PAGES_PER_STEP == 0 else 1

    def page_spec(j):
        return pl.BlockSpec((None, None, page, idx_dim),
                            lambda bi, s, pt: (layer, pt[bi * n_pages + s * g + j], 0, 0))

    row = lambda w: pl.BlockSpec((None, SAMPLE_T_PAD, w), lambda bi, s, pt: (bi, 0, 0))
    grid_spec = pltpu.PrefetchScalarGridSpec(
        num_scalar_prefetch=1,
        grid=(b, n_pages // g),
        in_specs=[row(iq_bf.shape[-1]), row(LANES)] + [page_spec(j) for j in range(g)],
        out_specs=pl.BlockSpec((None, SAMPLE_T_PAD, g * page), lambda bi, s, pt: (bi, 0, s)),
    )
    return pl.pallas_call(
        functools.partial(_sample_idx_kernel, idx_dim=idx_dim),
        grid_spec=grid_spec,
        out_shape=jax.ShapeDtypeStruct((b, SAMPLE_T_PAD, n_pages * page), F32),
        compiler_params=_cparams(("arbitrary", "arbitrary")),
        name="sample_idx",
    )(page_table.reshape(-1), iq_bf, small, *([cache_idx_k] * g))


def _sample_attn_kernel(pt_ref, sc_all_ref, sc_ref, q_ref, iq_ref, small_ref, ikn_ref, kn_ref, vn_ref,
                        *rest, top_k, n_new, att_heads, idx_dim, n_groups):
    g = (len(rest) - 7) // 2
    k_pages, v_pages = rest[:g], rest[g:2 * g]
    o_ref, qbd_ref, tau_ref, keyn_ref, m_ref, l_ref, acc_ref = rest[2 * g:]
    s_idx = pl.program_id(1)
    rows = att_heads * SAMPLE_T_PAD
    aw = att_heads * LANES
    scale = LANES ** -0.5

    @pl.when(s_idx == 0)
    def _():
        lane = lax.broadcasted_iota(I32, (SAMPLE_T_PAD, LANES), 1)
        tok = lax.broadcasted_iota(I32, (SAMPLE_T_PAD, LANES), 0)
        lhs = _idx_lhs_all_heads(iq_ref[...], lane)
        sc_new = _idx_scores(lhs, small_ref[...], ikn_ref[...], lane, idx_dim)
        new_ok = (lane <= tok) & (lane < n_new)
        key_new = jnp.where(new_ok, _order_key(sc_new), INT_MIN)
        key_past = _order_key(sc_all_ref[...])

        def count_ge(cand):
            return (jnp.sum((key_past >= cand).astype(I32), axis=-1, keepdims=True)
                    + jnp.sum((key_new >= cand).astype(I32), axis=-1, keepdims=True))

        tau = _kth_largest_key(count_ge, top_k, (SAMPLE_T_PAD, 1))
        tau_ref[...] = jnp.broadcast_to(tau, (SAMPLE_T_PAD, LANES))
        keyn_ref[...] = key_new
        q_rep = jnp.concatenate([q_ref[...].astype(F32)] * att_heads, axis=0)
        r_head = lax.broadcasted_iota(I32, (rows, aw), 0) // SAMPLE_T_PAD
        c_head = lax.broadcasted_iota(I32, (rows, aw), 1) // LANES
        qbd_ref[...] = jnp.where(r_head == c_head, q_rep, 0.0).astype(BF16)
        m_ref[...] = jnp.full(m_ref.shape, NEG_BIG, F32)
        l_ref[...] = jnp.zeros(l_ref.shape, F32)
        acc_ref[...] = jnp.zeros(acc_ref.shape, F32)

    tau = tau_ref[:, 0:1]

    def attend(sel8, keys_bf, vals_bf):
        sel = jnp.concatenate([sel8.astype(F32)] * att_heads, axis=0) > 0.5
        s = _dot_nt(qbd_ref[...], keys_bf) * scale
        s = jnp.where(sel, s, NEG_BIG)
        m_old = m_ref[:, 0:1]
        m_new = jnp.maximum(m_old, jnp.max(s, axis=-1, keepdims=True))
        alpha = jnp.exp(m_old - m_new)
        p = jnp.where(sel, jnp.exp(s - m_new), 0.0)
        l_ref[...] = jnp.broadcast_to(alpha * l_ref[:, 0:1] + jnp.sum(p, axis=-1, keepdims=True), l_ref.shape)
        acc_ref[...] = alpha * acc_ref[...] + _dot(p.astype(BF16), vals_bf)
        m_ref[...] = jnp.broadcast_to(m_new, m_ref.shape)

    keys = jnp.concatenate([r[...] for r in k_pages], axis=0).astype(BF16)
    vals = jnp.concatenate([r[...] for r in v_pages], axis=0).astype(BF16)
    attend(_order_key(sc_ref[...]) >= tau, keys, vals)

    @pl.when(s_idx == n_groups - 1)
    def _():
        attend(keyn_ref[...] >= tau, kn_ref[...], vn_ref[...])
        inv_l = 1.0 / l_ref[:, 0:1]
        for h in range(att_heads):
            rs = slice(h * SAMPLE_T_PAD, (h + 1) * SAMPLE_T_PAD)
            cs = slice(h * LANES, (h + 1) * LANES)
            o_ref[:, cs] = (acc_ref[rs, cs] * inv_l[rs, :]).astype(o_ref.dtype)


def sample_attention(page_table, scores, q_bf, iq_bf, small, ikn_pad, kn_pad, vn_pad,
                     cache_k, cache_v, layer, n_new, idx_dim):
    b, n_pages = page_table.shape
    page = cache_k.shape[2]
    aw = q_bf.shape[-1]
    att_heads = aw // LANES
    past = n_pages * page
    top_k = min(TOPK_MAX, (past + n_new) // 4)
    g = PAGES_PER_STEP if n_pages % PAGES_PER_STEP == 0 else 1
    n_groups = n_pages // g
    rows = att_heads * SAMPLE_T_PAD

    def page_spec(j):
        return pl.BlockSpec((None, None, page, aw),
                            lambda bi, s, pt: (layer, pt[bi * n_pages + s * g + j], 0, 0))

    def per_seq(r, w):
        return pl.BlockSpec((None, r, w), lambda bi, s, pt: (bi, 0, 0))

    grid_spec = pltpu.PrefetchScalarGridSpec(
        num_scalar_prefetch=1,
        grid=(b, n_groups),
        in_specs=[per_seq(SAMPLE_T_PAD, past),
                  pl.BlockSpec((None, SAMPLE_T_PAD, g * page), lambda bi, s, pt: (bi, 0, s)),
                  per_seq(SAMPLE_T_PAD, aw), per_seq(SAMPLE_T_PAD, iq_bf.shape[-1]),
                  per_seq(SAMPLE_T_PAD, LANES),
                  per_seq(LANES, LANES), per_seq(LANES, aw), per_seq(LANES, aw)]
                 + [page_spec(j) for j in range(g)] * 2,
        out_specs=per_seq(SAMPLE_T_PAD, aw),
        scratch_shapes=[
            pltpu.VMEM((rows, aw), BF16),
            pltpu.VMEM((SAMPLE_T_PAD, LANES), I32),
            pltpu.VMEM((SAMPLE_T_PAD, LANES), I32),
            pltpu.VMEM((rows, LANES), F32),
            pltpu.VMEM((rows, LANES), F32),
            pltpu.VMEM((rows, aw), F32),
        ],
    )
    cache_k2 = cache_k.reshape(cache_k.shape[0], cache_k.shape[1], page, aw)
    cache_v2 = cache_v.reshape(cache_v.shape[0], cache_v.shape[1], page, aw)
    return pl.pallas_call(
        functools.partial(_sample_attn_kernel, top_k=top_k, n_new=n_new, att_heads=att_heads,
                          idx_dim=idx_dim, n_groups=n_groups),
        grid_spec=grid_spec,
        out_shape=jax.ShapeDtypeStruct((b, SAMPLE_T_PAD, aw), BF16),
        compiler_params=_cparams(("arbitrary", "arbitrary")),
        name="sample_attn",
    )(page_table.reshape(-1), scores, scores, q_bf, iq_bf, small, ikn_pad, kn_pad, vn_pad,
      *([cache_k2] * g), *([cache_v2] * g))


HIST_ROWS = 32
CONV_ROWS = 32


def _conv_kernel(*refs, width, multi_tile):
    if multi_tile:
        u_ref, prev_ref, hist_ref, cw_ref, cb_ref, g_ref, b_ref, o_ref, buf = refs
    else:
        u_ref, hist_ref, cw_ref, cb_ref, g_ref, b_ref, o_ref, buf = refs
    tt = u_ref.shape[0]
    buf[HIST_ROWS:HIST_ROWS + tt, :] = u_ref[...]
    if multi_tile:
        buf[0:HIST_ROWS, :] = jnp.where(pl.program_id(1) == 0, hist_ref[...], prev_ref[...])
    else:
        buf[0:HIST_ROWS, :] = hist_ref[...]
    first = HIST_ROWS - (width - 1)
    rows = min(CONV_ROWS, tt)
    cb, g, b = cb_ref[...], g_ref[...], b_ref[...]
    for r0 in range(0, tt, rows):
        acc = jnp.zeros((rows, u_ref.shape[1]), F32)
        for j in range(width):
            acc = acc + buf[r0 + first + j:r0 + first + j + rows, :] * cw_ref[j:j + 1, :]
        y = acc + cb
        yc = y - jnp.mean(y, axis=-1, keepdims=True)
        var = jnp.mean(yc * yc, axis=-1, keepdims=True)
        y = yc * lax.rsqrt(var + LN_EPS) * g + b
        o_ref[r0:r0 + rows, :] = _silu(y).astype(o_ref.dtype)


def conv_branch(u, hist, cw, cb, ln_g, ln_b):
    b, t, c = u.shape
    width = cw.shape[0]
    tt = _row_tile(t, 256)
    multi_tile = t > tt
    row = pl.BlockSpec((None, tt, c), lambda bi, i: (bi, i, 0))
    vec = pl.BlockSpec((1, c), lambda bi, i: (0, 0))
    specs, args = [row], [u]
    if multi_tile:
        per = tt // HIST_ROWS
        specs.append(pl.BlockSpec((None, HIST_ROWS, c), lambda bi, i: (bi, jnp.maximum(i * per - 1, 0), 0)))
        args.append(u)
    specs += [pl.BlockSpec((None, HIST_ROWS, c), lambda bi, i: (bi, 0, 0)),
              pl.BlockSpec((width, c), lambda bi, i: (0, 0)), vec, vec, vec]
    args += [hist, cw, cb.reshape(1, c), ln_g.reshape(1, c), ln_b.reshape(1, c)]
    return pl.pallas_call(
        functools.partial(_conv_kernel, width=width, multi_tile=multi_tile),
        grid=(b, t // tt),
        in_specs=specs,
        out_specs=row,
        out_shape=jax.ShapeDtypeStruct((b, t, c), BF16),
        scratch_shapes=[pltpu.VMEM((HIST_ROWS + tt, c), F32)],
        compiler_params=_cparams(("arbitrary", "arbitrary")),
        name="conv_branch",
    )(*args)


def _rw_mix_kernel(h_ref, prev_ref, shift_ref, mu_ref, o_ref, buf):
    tt = h_ref.shape[0]
    h = h_ref[...]
    buf[SUBLANES:SUBLANES + tt, :] = h
    buf[SUBLANES - 1:SUBLANES, :] = jnp.where(pl.program_id(1) == 0, shift_ref[...],
                                              prev_ref[SUBLANES - 1:SUBLANES, :])
    xx = buf[SUBLANES - 1:SUBLANES - 1 + tt, :] - h
    for j in range(o_ref.shape[0]):
        o_ref[j] = (h + xx * mu_ref[j:j + 1, :]).astype(o_ref.dtype)


def rw_mix(h, shift_prev, mu):
    b, t, d = h.shape
    n_mix = mu.shape[0]
    tt = _row_tile(t, 256)
    per = tt // SUBLANES
    return pl.pallas_call(
        _rw_mix_kernel,
        grid=(b, t // tt),
        in_specs=[
            pl.BlockSpec((None, tt, d), lambda bi, i: (bi, i, 0)),
            pl.BlockSpec((None, SUBLANES, d), lambda bi, i: (bi, jnp.maximum(i * per - 1, 0), 0)),
            pl.BlockSpec((None, 1, d), lambda bi, i: (bi, 0, 0)),
            pl.BlockSpec((n_mix, d), lambda bi, i: (0, 0)),
        ],
        out_specs=pl.BlockSpec((n_mix, None, tt, d), lambda bi, i: (0, bi, i, 0)),
        out_shape=jax.ShapeDtypeStruct((n_mix, b, t, d), BF16),
        scratch_shapes=[pltpu.VMEM((SUBLANES + tt, d), F32)],
        compiler_params=_cparams(("arbitrary", "arbitrary")),
        name="rw_mix",
    )(h, h, shift_prev.reshape(b, 1, d), mu)


RW_HEAD = 64


def _seg_sum(x, ones_blk):
    outs = []
    for s in range(x.shape[1] // LANES):
        xs = x[:, s * LANES:(s + 1) * LANES]
        hi = xs.astype(BF16)
        lo = (xs - hi.astype(F32)).astype(BF16)
        outs.append(_dot(hi, ones_blk) + _dot(lo, ones_blk))
    return jnp.concatenate(outs, axis=-1)


def _wkv_kernel(r_ref, k_ref, v_ref, wl_ref, al_ref, g_ref,
                w0_ref, a0_ref, kk_ref, ka_ref, rk_ref, lg_ref, lb_ref, s0_ref,
                o_ref, s_ref, dec_sc, kk_sc, b_sc, kh_sc, y_sc, *, n_steps):
    i = pl.program_id(1)
    tc, d = r_ref.shape
    rb = lax.broadcasted_iota(I32, (LANES, LANES), 0) // RW_HEAD
    cb = lax.broadcasted_iota(I32, (LANES, LANES), 1) // RW_HEAD
    ones_blk = jnp.where(rb == cb, 1.0, 0.0).astype(BF16)
    eye2 = (lax.broadcasted_iota(I32, (RW_HEAD, LANES), 0)
            == lax.broadcasted_iota(I32, (RW_HEAD, LANES), 1) % RW_HEAD)

    @pl.when(i == 0)
    def _():
        s_ref[...] = s0_ref[...]

    k = k_ref[...]
    z = -(w0_ref[...] + wl_ref[...])
    softplus = jnp.maximum(z, 0.0) + jnp.log(1.0 + jnp.exp(-jnp.abs(z)))
    dec_sc[...] = jnp.exp(-jnp.exp(-softplus - 0.5))
    a = jax.nn.sigmoid(a0_ref[...] + al_ref[...])
    kk = k * kk_ref[...]
    kk = kk * lax.rsqrt(jnp.maximum(_seg_sum(kk * kk, ones_blk), 1e-24))
    kk_sc[...] = kk
    b_sc[...] = kk * a
    kh_sc[...] = k * (1.0 + (a - 1.0) * ka_ref[...])
    assert -(-n_steps // SUBLANES) * SUBLANES == tc, "the recurrence blocks must cover the whole time tile"

    steps_per_block = min(SUBLANES, n_steps)

    def block(blk, carry):
        rows = pl.ds(pl.multiple_of(blk * SUBLANES, SUBLANES), SUBLANES)
        for s in range(d // LANES):
            sl = slice(s * LANES, (s + 1) * LANES)
            kk_t, dec_t, b_t, kh_t = kk_sc[rows, sl], dec_sc[rows, sl], b_sc[rows, sl], kh_sc[rows, sl]
            v_t, r_t = v_ref[rows, sl], r_ref[rows, sl]
            st = s_ref[:, sl]
            y_rows = []
            for j in range(steps_per_block):
                row = slice(j, j + 1)
                sa = _dot((st * kk_t[row]).astype(BF16), ones_blk)
                vb = _dot(jnp.where(eye2, v_t[row], 0.0).astype(BF16), ones_blk)
                st = st * dec_t[row] - sa * b_t[row] + vb * kh_t[row]
                yb = _dot((st * r_t[row]).astype(BF16), ones_blk)
                y_rows.append(jnp.sum(jnp.where(eye2, yb, 0.0), axis=0, keepdims=True))
            s_ref[:, sl] = st
            y_rows += [jnp.zeros((1, LANES), F32)] * (SUBLANES - steps_per_block)
            y_sc[rows, sl] = jnp.concatenate(y_rows, axis=0)
        return carry

    lax.fori_loop(0, -(-n_steps // SUBLANES), block, 0)

    y = y_sc[...]
    inv_n = 1.0 / RW_HEAD
    yc = y - _seg_sum(y, ones_blk) * inv_n
    var = _seg_sum(yc * yc, ones_blk) * inv_n
    y = yc * lax.rsqrt(var + LNX_EPS) * lg_ref[...] + lb_ref[...]
    r = r_ref[...]
    y = y + _seg_sum(r * kh_sc[...] * rk_ref[...], ones_blk) * v_ref[...]
    o_ref[...] = (y * g_ref[...]).astype(o_ref.dtype)


def wkv(r, k, v, wl, al, g, w0, a0, k_k, k_a, r_k, lnx_g, lnx_b, s0, n_steps):
    b, t, d = r.shape
    tc = _row_tile(t, 128)
    steps = tc if t > tc else n_steps
    row = pl.BlockSpec((None, tc, d), lambda bi, i: (bi, i, 0))
    vec = pl.BlockSpec((1, d), lambda bi, i: (0, 0))
    st = pl.BlockSpec((None, RW_HEAD, d), lambda bi, i: (bi, 0, 0))
    vecs = [x.reshape(1, d) for x in (w0, a0, k_k, k_a, r_k, lnx_g, lnx_b)]
    return pl.pallas_call(
        functools.partial(_wkv_kernel, n_steps=steps),
        grid=(b, t // tc),
        in_specs=[row] * 6 + [vec] * 7 + [st],
        out_specs=[row, st],
        out_shape=[jax.ShapeDtypeStruct((b, t, d), BF16), jax.ShapeDtypeStruct((b, RW_HEAD, d), F32)],
        scratch_shapes=[pltpu.VMEM((tc, d), F32)] * 5,
        compiler_params=_cparams(("arbitrary", "arbitrary")),
        name="wkv",
    )(r, k, v, wl, al, g, *vecs, s0)


def _state_to_kernel_layout(s):
    b, h, n, _ = s.shape
    return jnp.transpose(s, (0, 2, 1, 3)).reshape(b, n, h * n)


def _state_from_kernel_layout(s, heads):
    b, n, _ = s.shape
    return jnp.transpose(s.reshape(b, n, heads, n), (0, 2, 1, 3))


def _run_group(x, mod_all, t_real, positions, sample_ctx, weights):
    (norm_g, ffn_w1, ffn_w3, ffn_w2, att_w_in, att_w_out, q_norm_g, k_norm_g, idx_k_norm_g,
     conv_w, conv_b, conv_ln_g, conv_ln_b, rw_mu, rw_w0, rw_w1, rw_w2, rw_a0, rw_a1, rw_a2, rw_g1, rw_g2,
     rw_k_k, rw_k_a, rw_r_k, rw_wr, rw_wk, rw_wv, rw_wo, rw_lnx_g, rw_lnx_b) = weights
    b, t, d = x.shape
    depth = norm_g.shape[0]
    aw = att_w_out.shape[1] - conv_w.shape[2]
    att_heads = aw // LANES
    cc = conv_w.shape[2]
    conv_width = conv_w.shape[1]
    idx_dim = idx_k_norm_g.shape[1]
    iq_w = IDX_HEADS * idx_dim
    rw_heads = d // RW_HEAD
    tables = _rope_tables(positions, LANES) + _rope_tables(positions, idx_dim)

    outs = dict(k=[], v=[], ik=[], conv=[], shift=[], wkv=[])
    h = resid_norm(x, mod_all[0], g=norm_g[0, 0], shift_idx=0, scale_idx=1, emit_x=False, h_dtype=BF16)
    for l in range(depth):
        i = l // 2
        mod = mod_all[l]
        even = l % 2 == 0
        y = ffn(h, ffn_w1, ffn_w3, ffn_w2, l, 0)
        x, h = resid_norm(x, mod, y=y, gate_idx=2, coef=HALF_STEP, g=norm_g[l, 1], shift_idx=3, scale_idx=4,
                          h_dtype=BF16 if even else F32)
        if even:
            qkvi = matmul(h, att_w_in, lead=i, n_cols=3 * aw + iq_w)
            w_small = jnp.pad(att_w_in[i][:, 3 * aw + iq_w:3 * aw + iq_w + idx_dim + IDX_HEADS],
                              ((0, 0), (0, LANES - idx_dim - IDX_HEADS)))
            small = matmul(h, w_small)
            u = matmul(h, att_w_in[i][:, 3 * aw + iq_w + idx_dim + IDX_HEADS:])
            q_bf, k_f, k_bf, v_bf, iq_bf, small, ikd_bf, glu = even_post(
                qkvi, small, u, tables, q_norm_g[i], k_norm_g[i], idx_k_norm_g[i], att_heads, idx_dim)
            if sample_ctx is None:
                att = dsa_prompt(q_bf, iq_bf, small, k_bf, v_bf, ikd_bf, idx_dim)
                hist = jnp.zeros((b, HIST_ROWS, cc), F32)
                outs["conv"].append(glu[:, t - (conv_width - 1):])
            else:
                pt = sample_ctx["page_table"]
                scores = sample_idx_scores(pt, iq_bf, small, sample_ctx["cache_idx_k"], i)
                pad_rows = lambda a: jnp.pad(a, ((0, 0), (0, LANES - t), (0, 0)))
                att = sample_attention(pt, scores, q_bf, iq_bf, small, pad_rows(ikd_bf), pad_rows(k_bf),
                                       pad_rows(v_bf), sample_ctx["cache_k"], sample_ctx["cache_v"], i,
                                       t_real, idx_dim)
                state = sample_ctx["state_conv"][i]
                hist = jnp.pad(state, ((0, 0), (HIST_ROWS - (conv_width - 1), 0), (0, 0)))
                outs["conv"].append(jnp.concatenate([state, glu[:, :t_real]], axis=1)[:, -(conv_width - 1):])
            conv_y = conv_branch(glu, hist, conv_w[i], conv_b[i], conv_ln_g[i], conv_ln_b[i])
            mixed = matmul(jnp.concatenate([att, conv_y], axis=-1), att_w_out, lead=i)
            outs["k"].append(k_f[:, :t_real].reshape(b, t_real, att_heads, LANES))
            outs["v"].append(qkvi[:, :t_real, 2 * aw:3 * aw].reshape(b, t_real, att_heads, LANES))
            outs["ik"].append(small[:, :t_real, :idx_dim])
        else:
            if sample_ctx is None:
                shift_prev = jnp.zeros((b, d), F32)
                s0 = jnp.zeros((b, RW_HEAD, d), F32)
            else:
                shift_prev = sample_ctx["state_shift"][i]
                s0 = _state_to_kernel_layout(sample_ctx["state_wkv"][i])
            xs = rw_mix(h, shift_prev, rw_mu[i])
            r = matmul(xs[0], rw_wr, lead=i)
            wl = lora(xs[1], rw_w1, rw_w2, i, "tanh")
            k = matmul(xs[2], rw_wk, lead=i)
            v = matmul(xs[3], rw_wv, lead=i)
            al = lora(xs[4], rw_a1, rw_a2, i, "none")
            g = lora(xs[5], rw_g1, rw_g2, i, "sigmoid")
            yg, s_fin = wkv(r, k, v, wl, al, g, rw_w0[i], rw_a0[i], rw_k_k[i], rw_k_a[i], rw_r_k[i],
                            rw_lnx_g[i], rw_lnx_b[i], s0, t_real)
            mixed = matmul(yg, rw_wo, lead=i)
            outs["shift"].append(h[:, t_real - 1])
            outs["wkv"].append(_state_from_kernel_layout(s_fin, rw_heads))
        x, h = resid_norm(x, mod, y=mixed, gate_idx=5, coef=1.0, g=norm_g[l, 2], shift_idx=6, scale_idx=7,
                          h_dtype=BF16)
        y = ffn(h, ffn_w1, ffn_w3, ffn_w2, l, 1)
        if l + 1 < depth:
            x, h = resid_norm(x, mod, y=y, gate_idx=8, coef=HALF_STEP, g=norm_g[l + 1, 0],
                              shift_idx=0, scale_idx=1, h_dtype=BF16, mod_norm=mod_all[l + 1])
        else:
            x = resid_norm(x, mod, y=y, gate_idx=8, coef=HALF_STEP)
    return x[:, :t_real], outs


def kernel(x_prompt, x_sample, cache_k, cache_v, cache_idx_k, state_conv, state_shift, state_wkv, page_table, c_prompt, c_sample, norm_g, ada_w, ada_b, ffn_w1, ffn_w3, ffn_w2, att_w_in, att_w_out, q_norm_g, k_norm_g, idx_k_norm_g, conv_w, conv_b, conv_ln_g, conv_ln_b, rw_mu, rw_w0, rw_w1, rw_w2, rw_a0, rw_a1, rw_a2, rw_g1, rw_g2, rw_k_k, rw_k_a, rw_r_k, rw_wr, rw_wk, rw_wv, rw_wo, rw_lnx_g, rw_lnx_b):
    bp, tp, d = x_prompt.shape
    bs, ts, _ = x_sample.shape
    depth = norm_g.shape[0]
    past = page_table.shape[1] * cache_k.shape[2]

    n_c = bp + bs
    c_rows = -(-n_c // 16) * 16
    c_all = jnp.pad(jnp.concatenate([c_prompt, c_sample], axis=0), ((0, c_rows - n_c), (0, 0)))
    mod = adaln_all(c_all, ada_w, ada_b).reshape(depth, c_rows, N_MOD, 1, d)
    mod_p, mod_s = mod[:, :bp], mod[:, bp:n_c]

    weights = (norm_g, ffn_w1, ffn_w3, ffn_w2, att_w_in, att_w_out, q_norm_g, k_norm_g, idx_k_norm_g,
               conv_w, conv_b, conv_ln_g, conv_ln_b, rw_mu, rw_w0, rw_w1, rw_w2, rw_a0, rw_a1, rw_a2,
               rw_g1, rw_g2, rw_k_k, rw_k_a, rw_r_k.reshape(rw_r_k.shape[0], -1), rw_wr, rw_wk, rw_wv, rw_wo,
               rw_lnx_g, rw_lnx_b)

    yp, op = _run_group(x_prompt, mod_p, tp, np.arange(tp), None, weights)
    xs_pad = jnp.pad(x_sample, ((0, 0), (0, SAMPLE_T_PAD - ts), (0, 0)))
    sample_ctx = dict(page_table=page_table, cache_k=cache_k, cache_v=cache_v, cache_idx_k=cache_idx_k,
                      state_conv=state_conv, state_shift=state_shift, state_wkv=state_wkv)
    ys, os_ = _run_group(xs_pad, mod_s, ts, past + np.arange(SAMPLE_T_PAD), sample_ctx, weights)

    st = lambda xs: jnp.stack(xs)
    return (yp, ys,
            st(op["k"]), st(op["v"]), st(op["ik"]), st(op["conv"]), st(op["shift"]), st(op["wkv"]),
            st(os_["k"]), st(os_["v"]), st(os_["ik"]), st(os_["conv"]), st(os_["shift"]), st(os_["wkv"]))
```

```python
import functools
import math

import numpy as np
import jax
import jax.numpy as jnp
from jax import lax
from jax.experimental import pallas as pl
from jax.experimental.pallas import tpu as pltpu

F32 = jnp.float32
BF16 = jnp.bfloat16
I32 = jnp.int32

ROPE_THETA = 10000.0
NORM_EPS = 1e-6
LN_EPS = 1e-5
LNX_EPS = 64e-5
N_MOD = 9
HALF_STEP = 0.5
TOPK_MAX = 256
QBLOCK = 128
IDX_HEADS = 16

LANES = 128
SUBLANES = 8
VMEM_LIMIT_MB = 56
NEG_BIG = -1e30
INT_MIN = -(2 ** 31)

SAMPLE_T_PAD = 8
PAGES_PER_STEP = 4
IDX_PAGES_PER_STEP = 8


def _cparams(sem, vmem_mb=VMEM_LIMIT_MB):
    return pltpu.CompilerParams(dimension_semantics=sem, vmem_limit_bytes=vmem_mb * 1024 * 1024)


def _silu(x):
    return x * jax.nn.sigmoid(x)


def _dot(a, b):
    return jnp.dot(a, b, preferred_element_type=F32)


def _dot_nt(a, b):
    return lax.dot_general(a, b, (((1,), (1,)), ((), ())), preferred_element_type=F32)


def _row_tile(t, target):
    return t if t <= target else target


def _adaln_kernel(c_ref, w_ref, b_ref, o_ref):
    sc = _silu(c_ref[...]).astype(BF16)
    o_ref[...] = _dot(sc, w_ref[...].astype(BF16)) + b_ref[...]


def adaln_all(c_all, ada_w, ada_b):
    depth, d, n = ada_w.shape
    rows = c_all.shape[0]
    tn = 1024
    return pl.pallas_call(
        _adaln_kernel,
        grid=(depth, n // tn),
        in_specs=[
            pl.BlockSpec((rows, d), lambda l, j: (0, 0)),
            pl.BlockSpec((None, d, tn), lambda l, j: (l, 0, j)),
            pl.BlockSpec((None, 1, tn), lambda l, j: (l, 0, j)),
        ],
        out_specs=pl.BlockSpec((None, rows, tn), lambda l, j: (l, 0, j)),
        out_shape=jax.ShapeDtypeStruct((depth, rows, n), F32),
        compiler_params=_cparams(("arbitrary", "arbitrary")),
        name="adaln",
    )(c_all, ada_w, ada_b.reshape(depth, 1, n))


def _resid_norm_kernel(*refs, has_y, coef, emit_x, emit_h):
    refs = list(refs)
    x = refs.pop(0)[...]
    if has_y:
        y = refs.pop(0)[...]
        gate = refs.pop(0)[...]
        x = x + (coef * gate) * y
    if emit_h:
        g = refs.pop(0)[...]
        shift = refs.pop(0)[...]
        scale = refs.pop(0)[...]
    if emit_x:
        refs.pop(0)[...] = x
    if emit_h:
        h_ref = refs.pop(0)
        ms = jnp.mean(x * x, axis=-1, keepdims=True)
        h = x * lax.rsqrt(ms + NORM_EPS) * g
        h_ref[...] = (h * (1.0 + scale) + shift).astype(h_ref.dtype)


def resid_norm(x, mod, *, y=None, gate_idx=None, coef=1.0, g=None, shift_idx=None, scale_idx=None,
               emit_x=True, h_dtype=None, mod_norm=None):
    mod_norm = mod if mod_norm is None else mod_norm
    b, t, d = x.shape
    tt = _row_tile(t, 256)
    has_y = y is not None
    emit_h = h_dtype is not None
    row = pl.BlockSpec((None, tt, d), lambda bi, i: (bi, i, 0))

    def mod_spec(idx):
        return pl.BlockSpec((None, None, 1, d), lambda bi, i: (bi, idx, 0, 0))

    args, specs = [x], [row]
    if has_y:
        args += [y, mod]
        specs += [row, mod_spec(gate_idx)]
    if emit_h:
        args += [g.reshape(1, d), mod_norm, mod_norm]
        specs += [pl.BlockSpec((1, d), lambda bi, i: (0, 0)), mod_spec(shift_idx), mod_spec(scale_idx)]
    out_shape, out_specs = [], []
    if emit_x:
        out_shape.append(jax.ShapeDtypeStruct((b, t, d), F32))
        out_specs.append(row)
    if emit_h:
        out_shape.append(jax.ShapeDtypeStruct((b, t, d), h_dtype))
        out_specs.append(row)
    outs = pl.pallas_call(
        functools.partial(_resid_norm_kernel, has_y=has_y, coef=coef, emit_x=emit_x, emit_h=emit_h),
        grid=(b, t // tt),
        in_specs=specs,
        out_specs=out_specs,
        out_shape=out_shape,
        compiler_params=_cparams(("arbitrary", "arbitrary")),
        name="resid_norm",
    )(*args)
    return outs if len(outs) > 1 else outs[0]


def _ffn_kernel(h_ref, w1_ref, w3_ref, w2_ref, o_ref):
    f = pl.program_id(1)
    h = h_ref[...]
    a = _dot(h, w1_ref[...].astype(BF16))
    b = _dot(h, w3_ref[...].astype(BF16))
    z = (_silu(a) * b).astype(BF16)
    contrib = _dot(z, w2_ref[...].astype(BF16))

    @pl.when(f == 0)
    def _():
        o_ref[...] = contrib

    @pl.when(f != 0)
    def _():
        o_ref[...] += contrib


def ffn(h, w1, w3, w2, layer, slot):
    b, t, d = h.shape
    m = b * t
    d_ff = w1.shape[-1]
    tm = 1024 if m >= 1024 else m
    tf = 256 if m >= 1024 else 512
    out = pl.pallas_call(
        _ffn_kernel,
        grid=(m // tm, d_ff // tf),
        in_specs=[
            pl.BlockSpec((tm, d), lambda i, f: (i, 0)),
            pl.BlockSpec((None, None, d, tf), lambda i, f: (layer, slot, 0, f)),
            pl.BlockSpec((None, None, d, tf), lambda i, f: (layer, slot, 0, f)),
            pl.BlockSpec((None, None, tf, d), lambda i, f: (layer, slot, f, 0)),
        ],
        out_specs=pl.BlockSpec((tm, d), lambda i, f: (i, 0)),
        out_shape=jax.ShapeDtypeStruct((m, d), F32),
        compiler_params=_cparams(("arbitrary", "arbitrary")),
        name="ffn",
    )(h.reshape(m, d), w1, w3, w2)
    return out.reshape(b, t, d)


def _mm_kernel(x_ref, w_ref, o_ref):
    o_ref[...] = _dot(x_ref[...], w_ref[...].astype(BF16)).astype(o_ref.dtype)


def matmul(x, w, lead=None, n_cols=None, out_dtype=F32):
    b, t, k = x.shape
    m = b * t
    n = w.shape[-1] if n_cols is None else n_cols
    tm = 1024 if m >= 1024 else m
    tn = 512 if n % 512 == 0 else n
    if lead is None:
        w_spec = pl.BlockSpec((k, tn), lambda i, j: (0, j))
    else:
        w_spec = pl.BlockSpec((None, k, tn), lambda i, j: (lead, 0, j))
    out = pl.pallas_call(
        _mm_kernel,
        grid=(m // tm, n // tn),
        in_specs=[pl.BlockSpec((tm, k), lambda i, j: (i, 0)), w_spec],
        out_specs=pl.BlockSpec((tm, tn), lambda i, j: (i, j)),
        out_shape=jax.ShapeDtypeStruct((m, n), out_dtype),
        compiler_params=_cparams(("arbitrary", "arbitrary")),
        name="matmul",
    )(x.reshape(m, k), w)
    return out.reshape(b, t, n)


def _lora_kernel(x_ref, a_ref, b_ref, o_ref, *, act):
    h = _dot(x_ref[...], a_ref[...].astype(BF16))
    if act == "tanh":
        h = jnp.tanh(h)
    elif act == "sigmoid":
        h = jax.nn.sigmoid(h)
    o_ref[...] = _dot(h.astype(BF16), b_ref[...].astype(BF16))


def lora(x, a, bmat, layer, act):
    b, t, d = x.shape
    m = b * t
    r = a.shape[-1]
    n = bmat.shape[-1]
    tm = 512 if m >= 512 else m
    out = pl.pallas_call(
        functools.partial(_lora_kernel, act=act),
        grid=(m // tm,),
        in_specs=[
            pl.BlockSpec((tm, d), lambda i: (i, 0)),
            pl.BlockSpec((None, d, r), lambda i: (layer, 0, 0)),
            pl.BlockSpec((None, r, n), lambda i: (layer, 0, 0)),
        ],
        out_specs=pl.BlockSpec((tm, n), lambda i: (i, 0)),
        out_shape=jax.ShapeDtypeStruct((m, n), F32),
        compiler_params=_cparams(("arbitrary",)),
        name="lora",
    )(x.reshape(m, d), a, bmat)
    return out.reshape(b, t, n)


def _rope_tables(positions, head_dim):
    half = head_dim // 2
    inv = ROPE_THETA ** (-np.arange(half, dtype=np.float64) / half)
    ang = np.asarray(positions, np.float64)[:, None] * inv[None, :]
    cos = np.concatenate([np.cos(ang), np.cos(ang)], axis=-1)
    sin = np.concatenate([-np.sin(ang), np.sin(ang)], axis=-1)
    reps = LANES // head_dim
    return (jnp.asarray(np.tile(cos, (1, reps)), F32), jnp.asarray(np.tile(sin, (1, reps)), F32))


def _rope128(x, cos, sin):
    return x * cos + pltpu.roll(x, 64, 1) * sin


def _rope64(x, cos, sin, lane):
    first_half = (lane % 64) < 32
    partner = jnp.where(first_half, pltpu.roll(x, 96, 1), pltpu.roll(x, 32, 1))
    return x * cos + partner * sin


def _even_post_kernel(q_ref, k_ref, v_ref, iq_ref, small_ref, u_ref,
                      c128_ref, s128_ref, c64_ref, s64_ref, qg_ref, kg_ref, ikg_ref,
                      qo_ref, kf_ref, kb_ref, vb_ref, iqo_ref, smallo_ref, ikd_ref, uo_ref,
                      *, att_heads, idx_dim, idx_heads):
    c128, s128 = c128_ref[...], s128_ref[...]
    c64, s64 = c64_ref[...], s64_ref[...]
    tt = c128.shape[0]
    lane = lax.broadcasted_iota(I32, (tt, LANES), 1)
    qg, kg = qg_ref[...], kg_ref[...]
    for h in range(att_heads):
        sl = slice(h * LANES, (h + 1) * LANES)
        q = q_ref[:, sl]
        q = q * lax.rsqrt(jnp.mean(q * q, axis=-1, keepdims=True) + NORM_EPS) * qg
        qo_ref[:, sl] = _rope128(q, c128, s128).astype(BF16)
        k = k_ref[:, sl]
        k = k * lax.rsqrt(jnp.mean(k * k, axis=-1, keepdims=True) + NORM_EPS) * kg
        k = _rope128(k, c128, s128)
        kf_ref[:, sl] = k
        kb_ref[:, sl] = k.astype(BF16)
        vb_ref[:, sl] = v_ref[:, sl].astype(BF16)
    for p in range(idx_heads * idx_dim // LANES):
        sl = slice(p * LANES, (p + 1) * LANES)
        iqo_ref[:, sl] = _rope64(iq_ref[:, sl], c64, s64, lane).astype(BF16)
    small = small_ref[...]
    is_ik = lane < idx_dim
    ik = jnp.where(is_ik, small, 0.0)
    ms = jnp.sum(ik * ik, axis=-1, keepdims=True) * (1.0 / idx_dim)
    ik = _rope64(ik * lax.rsqrt(ms + NORM_EPS) * ikg_ref[...], c64, s64, lane)
    ik = jnp.where(is_ik, ik, 0.0)
    smallo_ref[...] = jnp.where(is_ik, ik, small * (idx_heads ** -0.5))
    ikd_ref[...] = (ik + pltpu.roll(ik, 64, 1)).astype(BF16)
    cc = u_ref.shape[-1] // 2
    uo_ref[...] = u_ref[:, :cc] * jax.nn.sigmoid(u_ref[:, cc:])


def even_post(qkvi, small, u, tables, q_g, k_g, ik_g, att_heads, idx_dim):
    b, t, _ = qkvi.shape
    aw = att_heads * LANES
    iw = IDX_HEADS * idx_dim
    cc = u.shape[-1] // 2
    tt = _row_tile(t, 256)
    c128, s128, c64, s64 = tables

    def row(width, col=0):
        return pl.BlockSpec((None, tt, width), lambda bi, i: (bi, i, col))

    tab = pl.BlockSpec((tt, LANES), lambda bi, i: (i, 0))
    vec = pl.BlockSpec((1, LANES), lambda bi, i: (0, 0))
    ikg_pad = jnp.zeros((1, LANES), F32).at[0, :idx_dim].set(ik_g)
    assert aw == iw, "q/k/v/indexer-q column groups are addressed as equal-width blocks"
    return pl.pallas_call(
        functools.partial(_even_post_kernel, att_heads=att_heads, idx_dim=idx_dim, idx_heads=IDX_HEADS),
        grid=(b, t // tt),
        in_specs=[row(aw, 0), row(aw, 1), row(aw, 2), row(iw, 3), row(LANES), row(2 * cc),
                  tab, tab, tab, tab, vec, vec, vec],
        out_specs=[row(aw), row(aw), row(aw), row(aw), row(iw), row(LANES), row(LANES), row(cc)],
        out_shape=[
            jax.ShapeDtypeStruct((b, t, aw), BF16),
            jax.ShapeDtypeStruct((b, t, aw), F32),
            jax.ShapeDtypeStruct((b, t, aw), BF16),
            jax.ShapeDtypeStruct((b, t, aw), BF16),
            jax.ShapeDtypeStruct((b, t, iw), BF16),
            jax.ShapeDtypeStruct((b, t, LANES), F32),
            jax.ShapeDtypeStruct((b, t, LANES), BF16),
            jax.ShapeDtypeStruct((b, t, cc), F32),
        ],
        compiler_params=_cparams(("arbitrary", "arbitrary")),
        name="even_post",
    )(qkvi, qkvi, qkvi, qkvi, small, u, c128, s128, c64, s64,
      q_g.reshape(1, LANES), k_g.reshape(1, LANES), ikg_pad)


def _order_key(score):
    bits = pltpu.bitcast(score, I32)
    return jnp.where(bits < 0, bits ^ 0x7FFFFFFF, bits)


def _kth_largest_key(count_ge, top_k, shape):
    tau = jnp.where(count_ge(jnp.zeros(shape, I32)) >= top_k, 0, INT_MIN).astype(I32)

    def body(i, tau):
        cand = tau | jnp.left_shift(jnp.int32(1), 30 - i)
        return jnp.where(count_ge(cand) >= top_k, cand, tau)

    return lax.fori_loop(0, 31, body, tau)


def _idx_head_lhs(iq, h, lane):
    slab = iq[:, (h // 2) * LANES:(h // 2 + 1) * LANES]
    keep = (lane < 64) if h % 2 == 0 else (lane >= 64)
    return jnp.where(keep, slab, jnp.zeros_like(slab))


def _idx_weight(small, h, lane, idx_dim):
    return jnp.sum(jnp.where(lane == idx_dim + h, small, 0.0), axis=-1, keepdims=True)


def _dsa_prompt_kernel(q_ref, iq_ref, small_ref, k_ref, v_ref, ikd_ref, o_ref,
                       *, top_k, att_heads, idx_dim, n_classes):
    n = pl.program_id(1)
    per = k_ref.shape[0] // q_ref.shape[0] // n_classes
    for c in range(n_classes):
        pl.when(n // per == c)(functools.partial(
            _dsa_prompt_body, q_ref, iq_ref, small_ref, k_ref, v_ref, ikd_ref, o_ref,
            t_all=(c + 1) * per * q_ref.shape[0], top_k=top_k, att_heads=att_heads, idx_dim=idx_dim))


def _dsa_prompt_body(q_ref, iq_ref, small_ref, k_ref, v_ref, ikd_ref, o_ref,
                     *, t_all, top_k, att_heads, idx_dim):
    n = pl.program_id(1)
    qb = q_ref.shape[0]
    lane = lax.broadcasted_iota(I32, (qb, LANES), 1)
    iq = iq_ref[...]
    small = small_ref[...]
    ikd = ikd_ref[0:t_all, :]
    score = jnp.zeros((qb, t_all), F32)
    for h in range(IDX_HEADS):
        d = _dot_nt(_idx_head_lhs(iq, h, lane), ikd) * (idx_dim ** -0.5)
        score = score + _idx_weight(small, h, lane, idx_dim) * jnp.maximum(d, 0.0)
    col = lax.broadcasted_iota(I32, (qb, t_all), 1)
    rowpos = lax.broadcasted_iota(I32, (qb, t_all), 0) + n * qb
    admissible = col <= rowpos
    key = jnp.where(admissible, _order_key(score), INT_MIN)

    def count_ge(cand):
        return jnp.sum((key >= cand).astype(I32), axis=-1, keepdims=True)

    tau = _kth_largest_key(count_ge, top_k, (qb, 1))
    sel = (key >= tau) & admissible
    q = q_ref[...]
    scale = LANES ** -0.5
    for h in range(att_heads):
        sl = slice(h * LANES, (h + 1) * LANES)
        s = _dot_nt(q[:, sl], k_ref[0:t_all, sl]) * scale
        s = jnp.where(sel, s, NEG_BIG)
        m = jnp.max(s, axis=-1, keepdims=True)
        p = jnp.where(sel, jnp.exp(s - m), 0.0)
        l = jnp.sum(p, axis=-1, keepdims=True)
        o = _dot(p.astype(BF16), v_ref[0:t_all, sl]) / l
        o_ref[:, sl] = o.astype(o_ref.dtype)


def dsa_prompt(q_bf, iq_bf, small, k_bf, v_bf, ikd_bf, idx_dim):
    b, t, aw = q_bf.shape
    top_k = min(TOPK_MAX, t // 4)
    qrow = lambda w: pl.BlockSpec((None, QBLOCK, w), lambda bi, n: (bi, n, 0))
    full = lambda w: pl.BlockSpec((None, t, w), lambda bi, n: (bi, 0, 0))
    n_classes = 4 if (t // QBLOCK) % 4 == 0 else 1
    return pl.pallas_call(
        functools.partial(_dsa_prompt_kernel, top_k=top_k, att_heads=aw // LANES, idx_dim=idx_dim,
                          n_classes=n_classes),
        grid=(b, t // QBLOCK),
        in_specs=[qrow(aw), qrow(iq_bf.shape[-1]), qrow(LANES), full(aw), full(aw), full(LANES)],
        out_specs=qrow(aw),
        out_shape=jax.ShapeDtypeStruct((b, t, aw), BF16),
        compiler_params=_cparams(("arbitrary", "arbitrary")),
        name="dsa_prompt",
    )(q_bf, iq_bf, small, k_bf, v_bf, ikd_bf)


def _idx_lhs_all_heads(iq, lane):
    iq = iq.astype(F32)
    return jnp.concatenate([_idx_head_lhs(iq, h, lane) for h in range(IDX_HEADS)], axis=0).astype(BF16)


def _idx_scores(lhs, small, keys_dup, lane, idx_dim):
    d = _dot_nt(lhs, keys_dup) * (idx_dim ** -0.5)
    score = jnp.zeros((SAMPLE_T_PAD, keys_dup.shape[0]), F32)
    for h in range(IDX_HEADS):
        dh = d[h * SAMPLE_T_PAD:(h + 1) * SAMPLE_T_PAD, :]
        score = score + _idx_weight(small, h, lane, idx_dim) * jnp.maximum(dh, 0.0)
    return score


def _sample_idx_kernel(pt_ref, iq_ref, small_ref, *rest, idx_dim):
    page_refs, o_ref = rest[:-1], rest[-1]
    lane = lax.broadcasted_iota(I32, (SAMPLE_T_PAD, LANES), 1)
    keys = jnp.concatenate([r[...] for r in page_refs], axis=0)
    keys_dup = jnp.concatenate([keys, keys], axis=-1).astype(BF16)
    lhs = _idx_lhs_all_heads(iq_ref[...], lane)
    o_ref[...] = _idx_scores(lhs, small_ref[...], keys_dup, lane, idx_dim)


def sample_idx_scores(page_table, iq_bf, small, cache_idx_k, layer):
    b, n_pages = page_table.shape
    page, idx_dim = cache_idx_k.shape[-2:]
    g = IDX_PAGES_PER_STEP if n_pages % IDX_PAGES_PER_STEP == 0 else 1

    def page_spec(j):
        return pl.BlockSpec((None, None, page, idx_dim),
                            lambda bi, s, pt: (layer, pt[bi * n_pages + s * g + j], 0, 0))

    row = lambda w: pl.BlockSpec((None, SAMPLE_T_PAD, w), lambda bi, s, pt: (bi, 0, 0))
    grid_spec = pltpu.PrefetchScalarGridSpec(
        num_scalar_prefetch=1,
        grid=(b, n_pages // g),
        in_specs=[row(iq_bf.shape[-1]), row(LANES)] + [page_spec(j) for j in range(g)],
        out_specs=pl.BlockSpec((None, SAMPLE_T_PAD, g * page), lambda bi, s, pt: (bi, 0, s)),
    )
    return pl.pallas_call(
        functools.partial(_sample_idx_kernel, idx_dim=idx_dim),
        grid_spec=grid_spec,
        out_shape=jax.ShapeDtypeStruct((b, SAMPLE_T_PAD, n_pages * page), F32),
        compiler_params=_cparams(("arbitrary", "arbitrary")),
        name="sample_idx",
    )(page_table.reshape(-1), iq_bf, small, *([cache_idx_k] * g))


def _sample_attn_kernel(pt_ref, sc_all_ref, sc_ref, q_ref, iq_ref, small_ref, ikn_ref, kn_ref, vn_ref,
                        *rest, top_k, n_new, att_heads, idx_dim, n_groups):
    g = (len(rest) - 7) // 2
    k_pages, v_pages = rest[:g], rest[g:2 * g]
    o_ref, qbd_ref, tau_ref, keyn_ref, m_ref, l_ref, acc_ref = rest[2 * g:]
    s_idx = pl.program_id(1)
    rows = att_heads * SAMPLE_T_PAD
    aw = att_heads * LANES
    scale = LANES ** -0.5

    @pl.when(s_idx == 0)
    def _():
        lane = lax.broadcasted_iota(I32, (SAMPLE_T_PAD, LANES), 1)
        tok = lax.broadcasted_iota(I32, (SAMPLE_T_PAD, LANES), 0)
        lhs = _idx_lhs_all_heads(iq_ref[...], lane)
        sc_new = _idx_scores(lhs, small_ref[...], ikn_ref[...], lane, idx_dim)
        new_ok = (lane <= tok) & (lane < n_new)
        key_new = jnp.where(new_ok, _order_key(sc_new), INT_MIN)
        key_past = _order_key(sc_all_ref[...])

        def count_ge(cand):
            return (jnp.sum((key_past >= cand).astype(I32), axis=-1, keepdims=True)
                    + jnp.sum((key_new >= cand).astype(I32), axis=-1, keepdims=True))

        tau = _kth_largest_key(count_ge, top_k, (SAMPLE_T_PAD, 1))
        tau_ref[...] = jnp.broadcast_to(tau, (SAMPLE_T_PAD, LANES))
        keyn_ref[...] = key_new
        q_rep = jnp.concatenate([q_ref[...].astype(F32)] * att_heads, axis=0)
        r_head = lax.broadcasted_iota(I32, (rows, aw), 0) // SAMPLE_T_PAD
        c_head = lax.broadcasted_iota(I32, (rows, aw), 1) // LANES
        qbd_ref[...] = jnp.where(r_head == c_head, q_rep, 0.0).astype(BF16)
        m_ref[...] = jnp.full(m_ref.shape, NEG_BIG, F32)
        l_ref[...] = jnp.zeros(l_ref.shape, F32)
        acc_ref[...] = jnp.zeros(acc_ref.shape, F32)

    tau = tau_ref[:, 0:1]

    def attend(sel8, key_of_head, val_of_head):
        sel = jnp.concatenate([sel8.astype(F32)] * att_heads, axis=0) > 0.5
        s = _dot_nt(qbd_ref[:, 0:LANES], key_of_head(0))
        for h in range(1, att_heads):
            s = s + _dot_nt(qbd_ref[:, h * LANES:(h + 1) * LANES], key_of_head(h))
        s = jnp.where(sel, s * scale, NEG_BIG)
        m_old = m_ref[:, 0:1]
        m_new = jnp.maximum(m_old, jnp.max(s, axis=-1, keepdims=True))
        alpha = jnp.exp(m_old - m_new)
        p = jnp.where(sel, jnp.exp(s - m_new), 0.0)
        l_ref[...] = jnp.broadcast_to(alpha * l_ref[:, 0:1] + jnp.sum(p, axis=-1, keepdims=True), l_ref.shape)
        p = p.astype(BF16)
        for h in range(att_heads):
            cs = slice(h * LANES, (h + 1) * LANES)
            acc_ref[:, cs] = alpha * acc_ref[:, cs] + _dot(p, val_of_head(h))
        m_ref[...] = jnp.broadcast_to(m_new, m_ref.shape)

    def paged(pages):
        return lambda h: jnp.concatenate([r[:, h, :] for r in pages], axis=0).astype(BF16)

    attend(_order_key(sc_ref[...]) >= tau, paged(k_pages), paged(v_pages))

    @pl.when(s_idx == n_groups - 1)
    def _():
        attend(keyn_ref[...] >= tau,
               lambda h: kn_ref[:, h * LANES:(h + 1) * LANES], lambda h: vn_ref[:, h * LANES:(h + 1) * LANES])
        inv_l = 1.0 / l_ref[:, 0:1]
        for h in range(att_heads):
            rs = slice(h * SAMPLE_T_PAD, (h + 1) * SAMPLE_T_PAD)
            cs = slice(h * LANES, (h + 1) * LANES)
            o_ref[:, cs] = (acc_ref[rs, cs] * inv_l[rs, :]).astype(o_ref.dtype)


def sample_attention(page_table, scores, q_bf, iq_bf, small, ikn_pad, kn_pad, vn_pad,
                     cache_k, cache_v, layer, n_new, idx_dim):
    b, n_pages = page_table.shape
    page = cache_k.shape[2]
    aw = q_bf.shape[-1]
    att_heads = aw // LANES
    past = n_pages * page
    top_k = min(TOPK_MAX, (past + n_new) // 4)
    g = PAGES_PER_STEP if n_pages % PAGES_PER_STEP == 0 else 1
    n_groups = n_pages // g
    rows = att_heads * SAMPLE_T_PAD

    def page_spec(j):
        return pl.BlockSpec((None, None, page, att_heads, LANES),
                            lambda bi, s, pt: (layer, pt[bi * n_pages + s * g + j], 0, 0, 0))

    def per_seq(r, w):
        return pl.BlockSpec((None, r, w), lambda bi, s, pt: (bi, 0, 0))

    grid_spec = pltpu.PrefetchScalarGridSpec(
        num_scalar_prefetch=1,
        grid=(b, n_groups),
        in_specs=[per_seq(SAMPLE_T_PAD, past),
                  pl.BlockSpec((None, SAMPLE_T_PAD, g * page), lambda bi, s, pt: (bi, 0, s)),
                  per_seq(SAMPLE_T_PAD, aw), per_seq(SAMPLE_T_PAD, iq_bf.shape[-1]),
                  per_seq(SAMPLE_T_PAD, LANES),
                  per_seq(LANES, LANES), per_seq(LANES, aw), per_seq(LANES, aw)]
                 + [page_spec(j) for j in range(g)] * 2,
        out_specs=per_seq(SAMPLE_T_PAD, aw),
        scratch_shapes=[
            pltpu.VMEM((rows, aw), BF16),
            pltpu.VMEM((SAMPLE_T_PAD, LANES), I32),
            pltpu.VMEM((SAMPLE_T_PAD, LANES), I32),
            pltpu.VMEM((rows, LANES), F32),
            pltpu.VMEM((rows, LANES), F32),
            pltpu.VMEM((rows, aw), F32),
        ],
    )
    return pl.pallas_call(
        functools.partial(_sample_attn_kernel, top_k=top_k, n_new=n_new, att_heads=att_heads,
                          idx_dim=idx_dim, n_groups=n_groups),
        grid_spec=grid_spec,
        out_shape=jax.ShapeDtypeStruct((b, SAMPLE_T_PAD, aw), BF16),
        compiler_params=_cparams(("arbitrary", "arbitrary")),
        name="sample_attn",
    )(page_table.reshape(-1), scores, scores, q_bf, iq_bf, small, ikn_pad, kn_pad, vn_pad,
      *([cache_k] * g), *([cache_v] * g))


HIST_ROWS = 32
CONV_ROWS = 32


def _conv_kernel(*refs, width, multi_tile):
    if multi_tile:
        u_ref, prev_ref, hist_ref, cw_ref, cb_ref, g_ref, b_ref, o_ref, buf = refs
    else:
        u_ref, hist_ref, cw_ref, cb_ref, g_ref, b_ref, o_ref, buf = refs
    tt = u_ref.shape[0]
    buf[HIST_ROWS:HIST_ROWS + tt, :] = u_ref[...]
    if multi_tile:
        buf[0:HIST_ROWS, :] = jnp.where(pl.program_id(1) == 0, hist_ref[...], prev_ref[...])
    else:
        buf[0:HIST_ROWS, :] = hist_ref[...]
    first = HIST_ROWS - (width - 1)
    rows = min(CONV_ROWS, tt)
    cb, g, b = cb_ref[...], g_ref[...], b_ref[...]
    for r0 in range(0, tt, rows):
        acc = jnp.zeros((rows, u_ref.shape[1]), F32)
        for j in range(width):
            acc = acc + buf[r0 + first + j:r0 + first + j + rows, :] * cw_ref[j:j + 1, :]
        y = acc + cb
        yc = y - jnp.mean(y, axis=-1, keepdims=True)
        var = jnp.mean(yc * yc, axis=-1, keepdims=True)
        y = yc * lax.rsqrt(var + LN_EPS) * g + b
        o_ref[r0:r0 + rows, :] = _silu(y).astype(o_ref.dtype)


def conv_branch(u, hist, cw, cb, ln_g, ln_b):
    b, t, c = u.shape
    width = cw.shape[0]
    tt = _row_tile(t, 256)
    multi_tile = t > tt
    row = pl.BlockSpec((None, tt, c), lambda bi, i: (bi, i, 0))
    vec = pl.BlockSpec((1, c), lambda bi, i: (0, 0))
    specs, args = [row], [u]
    if multi_tile:
        per = tt // HIST_ROWS
        specs.append(pl.BlockSpec((None, HIST_ROWS, c), lambda bi, i: (bi, jnp.maximum(i * per - 1, 0), 0)))
        args.append(u)
    specs += [pl.BlockSpec((None, HIST_ROWS, c), lambda bi, i: (bi, 0, 0)),
              pl.BlockSpec((width, c), lambda bi, i: (0, 0)), vec, vec, vec]
    args += [hist, cw, cb.reshape(1, c), ln_g.reshape(1, c), ln_b.reshape(1, c)]
    return pl.pallas_call(
        functools.partial(_conv_kernel, width=width, multi_tile=multi_tile),
        grid=(b, t // tt),
        in_specs=specs,
        out_specs=row,
        out_shape=jax.ShapeDtypeStruct((b, t, c), BF16),
        scratch_shapes=[pltpu.VMEM((HIST_ROWS + tt, c), F32)],
        compiler_params=_cparams(("arbitrary", "arbitrary")),
        name="conv_branch",
    )(*args)


def _rw_mix_kernel(h_ref, prev_ref, shift_ref, mu_ref, o_ref, buf):
    tt = h_ref.shape[0]
    h = h_ref[...]
    buf[SUBLANES:SUBLANES + tt, :] = h
    buf[SUBLANES - 1:SUBLANES, :] = jnp.where(pl.program_id(1) == 0, shift_ref[...],
                                              prev_ref[SUBLANES - 1:SUBLANES, :])
    xx = buf[SUBLANES - 1:SUBLANES - 1 + tt, :] - h
    for j in range(o_ref.shape[0]):
        o_ref[j] = (h + xx * mu_ref[j:j + 1, :]).astype(o_ref.dtype)


def rw_mix(h, shift_prev, mu):
    b, t, d = h.shape
    n_mix = mu.shape[0]
    tt = _row_tile(t, 256)
    per = tt // SUBLANES
    return pl.pallas_call(
        _rw_mix_kernel,
        grid=(b, t // tt),
        in_specs=[
            pl.BlockSpec((None, tt, d), lambda bi, i: (bi, i, 0)),
            pl.BlockSpec((None, SUBLANES, d), lambda bi, i: (bi, jnp.maximum(i * per - 1, 0), 0)),
            pl.BlockSpec((None, 1, d), lambda bi, i: (bi, 0, 0)),
            pl.BlockSpec((n_mix, d), lambda bi, i: (0, 0)),
        ],
        out_specs=pl.BlockSpec((n_mix, None, tt, d), lambda bi, i: (0, bi, i, 0)),
        out_shape=jax.ShapeDtypeStruct((n_mix, b, t, d), BF16),
        scratch_shapes=[pltpu.VMEM((SUBLANES + tt, d), F32)],
        compiler_params=_cparams(("arbitrary", "arbitrary")),
        name="rw_mix",
    )(h, h, shift_prev.reshape(b, 1, d), mu)


RW_HEAD = 64


def _seg_sum(x, ones_blk):
    outs = []
    for s in range(x.shape[1] // LANES):
        xs = x[:, s * LANES:(s + 1) * LANES]
        hi = xs.astype(BF16)
        lo = (xs - hi.astype(F32)).astype(BF16)
        outs.append(_dot(hi, ones_blk) + _dot(lo, ones_blk))
    return jnp.concatenate(outs, axis=-1)


def _wkv_kernel(r_ref, k_ref, v_ref, wl_ref, al_ref, g_ref,
                w0_ref, a0_ref, kk_ref, ka_ref, rk_ref, lg_ref, lb_ref, s0_ref,
                o_ref, s_ref, dec_sc, kk_sc, b_sc, kh_sc, y_sc, *, n_steps):
    i = pl.program_id(1)
    tc, d = r_ref.shape
    rb = lax.broadcasted_iota(I32, (LANES, LANES), 0) // RW_HEAD
    cb = lax.broadcasted_iota(I32, (LANES, LANES), 1) // RW_HEAD
    ones_blk = jnp.where(rb == cb, 1.0, 0.0).astype(BF16)
    eye2 = (lax.broadcasted_iota(I32, (RW_HEAD, LANES), 0)
            == lax.broadcasted_iota(I32, (RW_HEAD, LANES), 1) % RW_HEAD)

    @pl.when(i == 0)
    def _():
        s_ref[...] = s0_ref[...]

    k = k_ref[...]
    z = -(w0_ref[...] + wl_ref[...])
    softplus = jnp.maximum(z, 0.0) + jnp.log(1.0 + jnp.exp(-jnp.abs(z)))
    dec_sc[...] = jnp.exp(-jnp.exp(-softplus - 0.5))
    a = jax.nn.sigmoid(a0_ref[...] + al_ref[...])
    kk = k * kk_ref[...]
    kk = kk * lax.rsqrt(jnp.maximum(_seg_sum(kk * kk, ones_blk), 1e-24))
    kk_sc[...] = kk
    b_sc[...] = kk * a
    kh_sc[...] = k * (1.0 + (a - 1.0) * ka_ref[...])
    assert -(-n_steps // SUBLANES) * SUBLANES == tc, "the recurrence blocks must cover the whole time tile"

    steps_per_block = min(SUBLANES, n_steps)

    slabs = [slice(s * LANES, (s + 1) * LANES) for s in range(d // LANES)]
    head_sel = (lax.broadcasted_iota(I32, (SUBLANES, LANES), 0)
                == lax.broadcasted_iota(I32, (SUBLANES, LANES), 1) // RW_HEAD).astype(BF16)

    def block(blk, carry):
        rows = pl.ds(pl.multiple_of(blk * SUBLANES, SUBLANES), SUBLANES)
        st = [s_ref[:, sl] for sl in slabs]
        y_rows = [[] for _ in slabs]
        for j in range(steps_per_block):
            sa, vb = [], []
            for s, sl in enumerate(slabs):
                sa.append(_dot((st[s] * kk_sc[rows, sl][j:j + 1]).astype(BF16), ones_blk))
                vb.append(_dot(jnp.where(eye2, v_ref[rows, sl][j:j + 1], 0.0).astype(BF16), ones_blk))
            for s, sl in enumerate(slabs):
                st[s] = (st[s] * dec_sc[rows, sl][j:j + 1] - sa[s] * b_sc[rows, sl][j:j + 1]
                         + vb[s] * kh_sc[rows, sl][j:j + 1])
            for s, sl in enumerate(slabs):
                yh = _dot_nt(head_sel, (st[s] * r_ref[rows, sl][j:j + 1]).astype(BF16))
                y_rows[s].append(jnp.concatenate([yh[0:1], yh[1:2]], axis=-1))
        for s, sl in enumerate(slabs):
            s_ref[:, sl] = st[s]
            pad = [jnp.zeros((1, LANES), F32)] * (SUBLANES - steps_per_block)
            y_sc[rows, sl] = jnp.concatenate(y_rows[s] + pad, axis=0)
        return carry

    lax.fori_loop(0, -(-n_steps // SUBLANES), block, 0)

    y = y_sc[...]
    inv_n = 1.0 / RW_HEAD
    yc = y - _seg_sum(y, ones_blk) * inv_n
    var = _seg_sum(yc * yc, ones_blk) * inv_n
    y = yc * lax.rsqrt(var + LNX_EPS) * lg_ref[...] + lb_ref[...]
    r = r_ref[...]
    y = y + _seg_sum(r * kh_sc[...] * rk_ref[...], ones_blk) * v_ref[...]
    o_ref[...] = (y * g_ref[...]).astype(o_ref.dtype)


def wkv(r, k, v, wl, al, g, w0, a0, k_k, k_a, r_k, lnx_g, lnx_b, s0, n_steps):
    b, t, d = r.shape
    tc = _row_tile(t, 128)
    steps = tc if t > tc else n_steps
    row = pl.BlockSpec((None, tc, d), lambda bi, i: (bi, i, 0))
    vec = pl.BlockSpec((1, d), lambda bi, i: (0, 0))
    st = pl.BlockSpec((None, RW_HEAD, d), lambda bi, i: (bi, 0, 0))
    vecs = [x.reshape(1, d) for x in (w0, a0, k_k, k_a, r_k, lnx_g, lnx_b)]
    return pl.pallas_call(
        functools.partial(_wkv_kernel, n_steps=steps),
        grid=(b, t // tc),
        in_specs=[row] * 6 + [vec] * 7 + [st],
        out_specs=[row, st],
        out_shape=[jax.ShapeDtypeStruct((b, t, d), BF16), jax.ShapeDtypeStruct((b, RW_HEAD, d), F32)],
        scratch_shapes=[pltpu.VMEM((tc, d), F32)] * 5,
        compiler_params=_cparams(("arbitrary", "arbitrary")),
        name="wkv",
    )(r, k, v, wl, al, g, *vecs, s0)


def _state_to_kernel_layout(s):
    b, h, n, _ = s.shape
    return jnp.transpose(s, (0, 2, 1, 3)).reshape(b, n, h * n)


def _state_from_kernel_layout(s, heads):
    b, n, _ = s.shape
    return jnp.transpose(s.reshape(b, n, heads, n), (0, 2, 1, 3))


def _run_group(x, mod_all, t_real, positions, sample_ctx, weights):
    (norm_g, ffn_w1, ffn_w3, ffn_w2, att_w_in, att_w_out, q_norm_g, k_norm_g, idx_k_norm_g,
     conv_w, conv_b, conv_ln_g, conv_ln_b, rw_mu, rw_w0, rw_w1, rw_w2, rw_a0, rw_a1, rw_a2, rw_g1, rw_g2,
     rw_k_k, rw_k_a, rw_r_k, rw_wr, rw_wk, rw_wv, rw_wo, rw_lnx_g, rw_lnx_b) = weights
    b, t, d = x.shape
    depth = norm_g.shape[0]
    aw = att_w_out.shape[1] - conv_w.shape[2]
    att_heads = aw // LANES
    cc = conv_w.shape[2]
    conv_width = conv_w.shape[1]
    idx_dim = idx_k_norm_g.shape[1]
    iq_w = IDX_HEADS * idx_dim
    rw_heads = d // RW_HEAD
    tables = _rope_tables(positions, LANES) + _rope_tables(positions, idx_dim)

    outs = dict(k=[], v=[], ik=[], conv=[], shift=[], wkv=[])
    h = resid_norm(x, mod_all[0], g=norm_g[0, 0], shift_idx=0, scale_idx=1, emit_x=False, h_dtype=BF16)
    for l in range(depth):
        i = l // 2
        mod = mod_all[l]
        even = l % 2 == 0
        y = ffn(h, ffn_w1, ffn_w3, ffn_w2, l, 0)
        x, h = resid_norm(x, mod, y=y, gate_idx=2, coef=HALF_STEP, g=norm_g[l, 1], shift_idx=3, scale_idx=4,
                          h_dtype=BF16 if even else F32)
        if even:
            qkvi = matmul(h, att_w_in, lead=i, n_cols=3 * aw + iq_w)
            w_small = jnp.pad(att_w_in[i][:, 3 * aw + iq_w:3 * aw + iq_w + idx_dim + IDX_HEADS],
                              ((0, 0), (0, LANES - idx_dim - IDX_HEADS)))
            small = matmul(h, w_small)
            u = matmul(h, att_w_in[i][:, 3 * aw + iq_w + idx_dim + IDX_HEADS:])
            q_bf, k_f, k_bf, v_bf, iq_bf, small, ikd_bf, glu = even_post(
                qkvi, small, u, tables, q_norm_g[i], k_norm_g[i], idx_k_norm_g[i], att_heads, idx_dim)
            if sample_ctx is None:
                att = dsa_prompt(q_bf, iq_bf, small, k_bf, v_bf, ikd_bf, idx_dim)
                hist = jnp.zeros((b, HIST_ROWS, cc), F32)
                outs["conv"].append(glu[:, t - (conv_width - 1):])
            else:
                pt = sample_ctx["page_table"]
                scores = sample_idx_scores(pt, iq_bf, small, sample_ctx["cache_idx_k"], i)
                pad_rows = lambda a: jnp.pad(a, ((0, 0), (0, LANES - t), (0, 0)))
                att = sample_attention(pt, scores, q_bf, iq_bf, small, pad_rows(ikd_bf), pad_rows(k_bf),
                                       pad_rows(v_bf), sample_ctx["cache_k"], sample_ctx["cache_v"], i,
                                       t_real, idx_dim)
                state = sample_ctx["state_conv"][i]
                hist = jnp.pad(state, ((0, 0), (HIST_ROWS - (conv_width - 1), 0), (0, 0)))
                outs["conv"].append(jnp.concatenate([state, glu[:, :t_real]], axis=1)[:, -(conv_width - 1):])
            conv_y = conv_branch(glu, hist, conv_w[i], conv_b[i], conv_ln_g[i], conv_ln_b[i])
            mixed = matmul(jnp.concatenate([att, conv_y], axis=-1), att_w_out, lead=i)
            outs["k"].append(k_f[:, :t_real].reshape(b, t_real, att_heads, LANES))
            outs["v"].append(qkvi[:, :t_real, 2 * aw:3 * aw].reshape(b, t_real, att_heads, LANES))
            outs["ik"].append(small[:, :t_real, :idx_dim])
        else:
            if sample_ctx is None:
                shift_prev = jnp.zeros((b, d), F32)
                s0 = jnp.zeros((b, RW_HEAD, d), F32)
            else:
                shift_prev = sample_ctx["state_shift"][i]
                s0 = _state_to_kernel_layout(sample_ctx["state_wkv"][i])
            xs = rw_mix(h, shift_prev, rw_mu[i])
            r = matmul(xs[0], rw_wr, lead=i)
            wl = lora(xs[1], rw_w1, rw_w2, i, "tanh")
            k = matmul(xs[2], rw_wk, lead=i)
            v = matmul(xs[3], rw_wv, lead=i)
            al = lora(xs[4], rw_a1, rw_a2, i, "none")
            g = lora(xs[5], rw_g1, rw_g2, i, "sigmoid")
            yg, s_fin = wkv(r, k, v, wl, al, g, rw_w0[i], rw_a0[i], rw_k_k[i], rw_k_a[i], rw_r_k[i],
                            rw_lnx_g[i], rw_lnx_b[i], s0, t_real)
            mixed = matmul(yg, rw_wo, lead=i)
            outs["shift"].append(h[:, t_real - 1])
            outs["wkv"].append(_state_from_kernel_layout(s_fin, rw_heads))
        x, h = resid_norm(x, mod, y=mixed, gate_idx=5, coef=1.0, g=norm_g[l, 2], shift_idx=6, scale_idx=7,
                          h_dtype=BF16)
        y = ffn(h, ffn_w1, ffn_w3, ffn_w2, l, 1)
        if l + 1 < depth:
            x, h = resid_norm(x, mod, y=y, gate_idx=8, coef=HALF_STEP, g=norm_g[l + 1, 0],
                              shift_idx=0, scale_idx=1, h_dtype=BF16, mod_norm=mod_all[l + 1])
        else:
            x = resid_norm(x, mod, y=y, gate_idx=8, coef=HALF_STEP)
    return x[:, :t_real], outs


def kernel(x_prompt, x_sample, cache_k, cache_v, cache_idx_k, state_conv, state_shift, state_wkv, page_table, c_prompt, c_sample, norm_g, ada_w, ada_b, ffn_w1, ffn_w3, ffn_w2, att_w_in, att_w_out, q_norm_g, k_norm_g, idx_k_norm_g, conv_w, conv_b, conv_ln_g, conv_ln_b, rw_mu, rw_w0, rw_w1, rw_w2, rw_a0, rw_a1, rw_a2, rw_g1, rw_g2, rw_k_k, rw_k_a, rw_r_k, rw_wr, rw_wk, rw_wv, rw_wo, rw_lnx_g, rw_lnx_b):
    bp, tp, d = x_prompt.shape
    bs, ts, _ = x_sample.shape
    depth = norm_g.shape[0]
    past = page_table.shape[1] * cache_k.shape[2]

    n_c = bp + bs
    c_rows = -(-n_c // 16) * 16
    c_all = jnp.pad(jnp.concatenate([c_prompt, c_sample], axis=0), ((0, c_rows - n_c), (0, 0)))
    mod = adaln_all(c_all, ada_w, ada_b).reshape(depth, c_rows, N_MOD, 1, d)
    mod_p, mod_s = mod[:, :bp], mod[:, bp:n_c]

    weights = (norm_g, ffn_w1, ffn_w3, ffn_w2, att_w_in, att_w_out, q_norm_g, k_norm_g, idx_k_norm_g,
               conv_w, conv_b, conv_ln_g, conv_ln_b, rw_mu, rw_w0, rw_w1, rw_w2, rw_a0, rw_a1, rw_a2,
               rw_g1, rw_g2, rw_k_k, rw_k_a, rw_r_k.reshape(rw_r_k.shape[0], -1), rw_wr, rw_wk, rw_wv, rw_wo,
               rw_lnx_g, rw_lnx_b)

    yp, op = _run_group(x_prompt, mod_p, tp, np.arange(tp), None, weights)
    xs_pad = jnp.pad(x_sample, ((0, 0), (0, SAMPLE_T_PAD - ts), (0, 0)))
    sample_ctx = dict(page_table=page_table, cache_k=cache_k, cache_v=cache_v, cache_idx_k=cache_idx_k,
                      state_conv=state_conv, state_shift=state_shift, state_wkv=state_wkv)
    ys, os_ = _run_group(xs_pad, mod_s, ts, past + np.arange(SAMPLE_T_PAD), sample_ctx, weights)

    st = lambda xs: jnp.stack(xs)
    return (yp, ys,
            st(op["k"]), st(op["v"]), st(op["ik"]), st(op["conv"]), st(op["shift"]), st(op["wkv"]),
            st(os_["k"]), st(os_["v"]), st(os_["ik"]), st(os_["conv"]), st(os_["shift"]), st(os_["wkv"]))
```

```python
import functools
import math

import numpy as np
import jax
import jax.numpy as jnp
from jax import lax
from jax.experimental import pallas as pl
from jax.experimental.pallas import tpu as pltpu

F32 = jnp.float32
BF16 = jnp.bfloat16
I32 = jnp.int32

ROPE_THETA = 10000.0
NORM_EPS = 1e-6
LN_EPS = 1e-5
LNX_EPS = 64e-5
N_MOD = 9
HALF_STEP = 0.5
TOPK_MAX = 256
QBLOCK = 128
IDX_HEADS = 16

LANES = 128
SUBLANES = 8
VMEM_LIMIT_MB = 56
NEG_BIG = -1e30
INT_MIN = -(2 ** 31)

SAMPLE_T_PAD = 8
PAGES_PER_STEP = 4
IDX_PAGES_PER_STEP = 8


def _cparams(sem, vmem_mb=VMEM_LIMIT_MB):
    return pltpu.CompilerParams(dimension_semantics=sem, vmem_limit_bytes=vmem_mb * 1024 * 1024)


def _silu(x):
    return x * jax.nn.sigmoid(x)


def _dot(a, b):
    return jnp.dot(a, b, preferred_element_type=F32)


def _dot_nt(a, b):
    return lax.dot_general(a, b, (((1,), (1,)), ((), ())), preferred_element_type=F32)


def _row_tile(t, target):
    return t if t <= target else target


def _adaln_kernel(c_ref, w_ref, b_ref, o_ref):
    sc = _silu(c_ref[...]).astype(BF16)
    o_ref[...] = _dot(sc, w_ref[...].astype(BF16)) + b_ref[...]


def adaln_all(c_all, ada_w, ada_b):
    depth, d, n = ada_w.shape
    rows = c_all.shape[0]
    tn = 1024
    return pl.pallas_call(
        _adaln_kernel,
        grid=(depth, n // tn),
        in_specs=[
            pl.BlockSpec((rows, d), lambda l, j: (0, 0)),
            pl.BlockSpec((None, d, tn), lambda l, j: (l, 0, j)),
            pl.BlockSpec((None, 1, tn), lambda l, j: (l, 0, j)),
        ],
        out_specs=pl.BlockSpec((None, rows, tn), lambda l, j: (l, 0, j)),
        out_shape=jax.ShapeDtypeStruct((depth, rows, n), F32),
        compiler_params=_cparams(("arbitrary", "arbitrary")),
        name="adaln",
    )(c_all, ada_w, ada_b.reshape(depth, 1, n))


def _resid_norm_kernel(*refs, has_y, coef, emit_x, emit_h):
    refs = list(refs)
    x = refs.pop(0)[...]
    if has_y:
        y = refs.pop(0)[...]
        gate = refs.pop(0)[...]
        x = x + (coef * gate) * y
    if emit_h:
        g = refs.pop(0)[...]
        shift = refs.pop(0)[...]
        scale = refs.pop(0)[...]
    if emit_x:
        refs.pop(0)[...] = x
    if emit_h:
        h_ref = refs.pop(0)
        ms = jnp.mean(x * x, axis=-1, keepdims=True)
        h = x * lax.rsqrt(ms + NORM_EPS) * g
        h_ref[...] = (h * (1.0 + scale) + shift).astype(h_ref.dtype)


def resid_norm(x, mod, *, y=None, gate_idx=None, coef=1.0, g=None, shift_idx=None, scale_idx=None,
               emit_x=True, h_dtype=None, mod_norm=None):
    mod_norm = mod if mod_norm is None else mod_norm
    b, t, d = x.shape
    tt = _row_tile(t, 256)
    has_y = y is not None
    emit_h = h_dtype is not None
    row = pl.BlockSpec((None, tt, d), lambda bi, i: (bi, i, 0))

    def mod_spec(idx):
        return pl.BlockSpec((None, None, 1, d), lambda bi, i: (bi, idx, 0, 0))

    args, specs = [x], [row]
    if has_y:
        args += [y, mod]
        specs += [row, mod_spec(gate_idx)]
    if emit_h:
        args += [g.reshape(1, d), mod_norm, mod_norm]
        specs += [pl.BlockSpec((1, d), lambda bi, i: (0, 0)), mod_spec(shift_idx), mod_spec(scale_idx)]
    out_shape, out_specs = [], []
    if emit_x:
        out_shape.append(jax.ShapeDtypeStruct((b, t, d), F32))
        out_specs.append(row)
    if emit_h:
        out_shape.append(jax.ShapeDtypeStruct((b, t, d), h_dtype))
        out_specs.append(row)
    outs = pl.pallas_call(
        functools.partial(_resid_norm_kernel, has_y=has_y, coef=coef, emit_x=emit_x, emit_h=emit_h),
        grid=(b, t // tt),
        in_specs=specs,
        out_specs=out_specs,
        out_shape=out_shape,
        compiler_params=_cparams(("arbitrary", "arbitrary")),
        name="resid_norm",
    )(*args)
    return outs if len(outs) > 1 else outs[0]


def _ffn_kernel(h_ref, w1_ref, w3_ref, w2_ref, o_ref):
    f = pl.program_id(1)
    h = h_ref[...]
    a = _dot(h, w1_ref[...].astype(BF16))
    b = _dot(h, w3_ref[...].astype(BF16))
    z = (_silu(a) * b).astype(BF16)

    @pl.when(f == 0)
    def _():
        o_ref[...] = jnp.zeros(o_ref.shape, o_ref.dtype)

    o_ref[...] += _dot(z, w2_ref[...].astype(BF16))


def ffn(h, w1, w3, w2, layer, slot):
    b, t, d = h.shape
    m = b * t
    d_ff = w1.shape[-1]
    tm = 1024 if m >= 1024 else m
    tf = 256 if m >= 1024 else 512
    out = pl.pallas_call(
        _ffn_kernel,
        grid=(m // tm, d_ff // tf),
        in_specs=[
            pl.BlockSpec((tm, d), lambda i, f: (i, 0)),
            pl.BlockSpec((None, None, d, tf), lambda i, f: (layer, slot, 0, f)),
            pl.BlockSpec((None, None, d, tf), lambda i, f: (layer, slot, 0, f)),
            pl.BlockSpec((None, None, tf, d), lambda i, f: (layer, slot, f, 0)),
        ],
        out_specs=pl.BlockSpec((tm, d), lambda i, f: (i, 0)),
        out_shape=jax.ShapeDtypeStruct((m, d), F32),
        compiler_params=_cparams(("arbitrary", "arbitrary")),
        name="ffn",
    )(h.reshape(m, d), w1, w3, w2)
    return out.reshape(b, t, d)


def _mm_kernel(x_ref, w_ref, o_ref):
    o_ref[...] = _dot(x_ref[...], w_ref[...].astype(BF16)).astype(o_ref.dtype)


def matmul(x, w, lead=None, n_cols=None, out_dtype=F32):
    b, t, k = x.shape
    m = b * t
    n = w.shape[-1] if n_cols is None else n_cols
    tm = 1024 if m >= 1024 else m
    tn = 512 if n % 512 == 0 else n
    if lead is None:
        w_spec = pl.BlockSpec((k, tn), lambda i, j: (0, j))
    else:
        w_spec = pl.BlockSpec((None, k, tn), lambda i, j: (lead, 0, j))
    out = pl.pallas_call(
        _mm_kernel,
        grid=(m // tm, n // tn),
        in_specs=[pl.BlockSpec((tm, k), lambda i, j: (i, 0)), w_spec],
        out_specs=pl.BlockSpec((tm, tn), lambda i, j: (i, j)),
        out_shape=jax.ShapeDtypeStruct((m, n), out_dtype),
        compiler_params=_cparams(("arbitrary", "arbitrary")),
        name="matmul",
    )(x.reshape(m, k), w)
    return out.reshape(b, t, n)


def _lora_kernel(x_ref, a_ref, b_ref, o_ref, *, act):
    h = _dot(x_ref[...], a_ref[...].astype(BF16))
    if act == "tanh":
        h = jnp.tanh(h)
    elif act == "sigmoid":
        h = jax.nn.sigmoid(h)
    o_ref[...] = _dot(h.astype(BF16), b_ref[...].astype(BF16))


def lora(x, a, bmat, layer, act):
    b, t, d = x.shape
    m = b * t
    r = a.shape[-1]
    n = bmat.shape[-1]
    tm = 512 if m >= 512 else m
    out = pl.pallas_call(
        functools.partial(_lora_kernel, act=act),
        grid=(m // tm,),
        in_specs=[
            pl.BlockSpec((tm, d), lambda i: (i, 0)),
            pl.BlockSpec((None, d, r), lambda i: (layer, 0, 0)),
            pl.BlockSpec((None, r, n), lambda i: (layer, 0, 0)),
        ],
        out_specs=pl.BlockSpec((tm, n), lambda i: (i, 0)),
        out_shape=jax.ShapeDtypeStruct((m, n), F32),
        compiler_params=_cparams(("arbitrary",)),
        name="lora",
    )(x.reshape(m, d), a, bmat)
    return out.reshape(b, t, n)


def _rope_tables(positions, head_dim):
    half = head_dim // 2
    inv = ROPE_THETA ** (-np.arange(half, dtype=np.float64) / half)
    ang = np.asarray(positions, np.float64)[:, None] * inv[None, :]
    cos = np.concatenate([np.cos(ang), np.cos(ang)], axis=-1)
    sin = np.concatenate([-np.sin(ang), np.sin(ang)], axis=-1)
    reps = LANES // head_dim
    return (jnp.asarray(np.tile(cos, (1, reps)), F32), jnp.asarray(np.tile(sin, (1, reps)), F32))


def _rope128(x, cos, sin):
    return x * cos + pltpu.roll(x, 64, 1) * sin


def _rope64(x, cos, sin, lane):
    first_half = (lane % 64) < 32
    partner = jnp.where(first_half, pltpu.roll(x, 96, 1), pltpu.roll(x, 32, 1))
    return x * cos + partner * sin


def _even_post_kernel(q_ref, k_ref, v_ref, iq_ref, small_ref, u_ref,
                      c128_ref, s128_ref, c64_ref, s64_ref, qg_ref, kg_ref, ikg_ref,
                      qo_ref, kf_ref, kb_ref, vb_ref, iqo_ref, smallo_ref, ikd_ref, uo_ref,
                      *, att_heads, idx_dim, idx_heads):
    c128, s128 = c128_ref[...], s128_ref[...]
    c64, s64 = c64_ref[...], s64_ref[...]
    tt = c128.shape[0]
    lane = lax.broadcasted_iota(I32, (tt, LANES), 1)
    qg, kg = qg_ref[...], kg_ref[...]
    for h in range(att_heads):
        sl = slice(h * LANES, (h + 1) * LANES)
        q = q_ref[:, sl]
        q = q * lax.rsqrt(jnp.mean(q * q, axis=-1, keepdims=True) + NORM_EPS) * qg
        qo_ref[:, sl] = _rope128(q, c128, s128).astype(BF16)
        k = k_ref[:, sl]
        k = k * lax.rsqrt(jnp.mean(k * k, axis=-1, keepdims=True) + NORM_EPS) * kg
        k = _rope128(k, c128, s128)
        kf_ref[:, sl] = k
        kb_ref[:, sl] = k.astype(BF16)
        vb_ref[:, sl] = v_ref[:, sl].astype(BF16)
    for p in range(idx_heads * idx_dim // LANES):
        sl = slice(p * LANES, (p + 1) * LANES)
        iqo_ref[:, sl] = _rope64(iq_ref[:, sl], c64, s64, lane).astype(BF16)
    small = small_ref[...]
    is_ik = lane < idx_dim
    ik = jnp.where(is_ik, small, 0.0)
    ms = jnp.sum(ik * ik, axis=-1, keepdims=True) * (1.0 / idx_dim)
    ik = _rope64(ik * lax.rsqrt(ms + NORM_EPS) * ikg_ref[...], c64, s64, lane)
    ik = jnp.where(is_ik, ik, 0.0)
    smallo_ref[...] = jnp.where(is_ik, ik, small * (idx_heads ** -0.5))
    ikd_ref[...] = (ik + pltpu.roll(ik, 64, 1)).astype(BF16)
    cc = u_ref.shape[-1] // 2
    uo_ref[...] = u_ref[:, :cc] * jax.nn.sigmoid(u_ref[:, cc:])


def even_post(qkvi, small, u, tables, q_g, k_g, ik_g, att_heads, idx_dim):
    b, t, _ = qkvi.shape
    aw = att_heads * LANES
    iw = IDX_HEADS * idx_dim
    cc = u.shape[-1] // 2
    tt = _row_tile(t, 256)
    c128, s128, c64, s64 = tables

    def row(width, col=0):
        return pl.BlockSpec((None, tt, width), lambda bi, i: (bi, i, col))

    tab = pl.BlockSpec((tt, LANES), lambda bi, i: (i, 0))
    vec = pl.BlockSpec((1, LANES), lambda bi, i: (0, 0))
    ikg_pad = jnp.zeros((1, LANES), F32).at[0, :idx_dim].set(ik_g)
    assert aw == iw, "q/k/v/indexer-q column groups are addressed as equal-width blocks"
    return pl.pallas_call(
        functools.partial(_even_post_kernel, att_heads=att_heads, idx_dim=idx_dim, idx_heads=IDX_HEADS),
        grid=(b, t // tt),
        in_specs=[row(aw, 0), row(aw, 1), row(aw, 2), row(iw, 3), row(LANES), row(2 * cc),
                  tab, tab, tab, tab, vec, vec, vec],
        out_specs=[row(aw), row(aw), row(aw), row(aw), row(iw), row(LANES), row(LANES), row(cc)],
        out_shape=[
            jax.ShapeDtypeStruct((b, t, aw), BF16),
            jax.ShapeDtypeStruct((b, t, aw), F32),
            jax.ShapeDtypeStruct((b, t, aw), BF16),
            jax.ShapeDtypeStruct((b, t, aw), BF16),
            jax.ShapeDtypeStruct((b, t, iw), BF16),
            jax.ShapeDtypeStruct((b, t, LANES), F32),
            jax.ShapeDtypeStruct((b, t, LANES), BF16),
            jax.ShapeDtypeStruct((b, t, cc), F32),
        ],
        compiler_params=_cparams(("arbitrary", "arbitrary")),
        name="even_post",
    )(qkvi, qkvi, qkvi, qkvi, small, u, c128, s128, c64, s64,
      q_g.reshape(1, LANES), k_g.reshape(1, LANES), ikg_pad)


def _order_key(score):
    bits = pltpu.bitcast(score, I32)
    return jnp.where(bits < 0, bits ^ 0x7FFFFFFF, bits)


def _kth_largest_key(count_ge, top_k, shape):
    tau = jnp.where(count_ge(jnp.zeros(shape, I32)) >= top_k, 0, INT_MIN).astype(I32)

    def body(i, tau):
        cand = tau | jnp.left_shift(jnp.int32(1), 30 - i)
        return jnp.where(count_ge(cand) >= top_k, cand, tau)

    return lax.fori_loop(0, 31, body, tau)


def _idx_head_lhs(iq, h, lane):
    slab = iq[:, (h // 2) * LANES:(h // 2 + 1) * LANES]
    keep = (lane < 64) if h % 2 == 0 else (lane >= 64)
    return jnp.where(keep, slab, jnp.zeros_like(slab))


def _idx_weight(small, h, lane, idx_dim):
    return jnp.sum(jnp.where(lane == idx_dim + h, small, 0.0), axis=-1, keepdims=True)


KEY_CHUNK = 512


def _dsa_prompt_kernel(q_ref, iq_ref, small_t_ref, k_ref, v_t_ref, ikd_ref, o_ref, lhs_sc, key_sc,
                       *, top_k, att_heads, idx_dim):
    n = pl.program_id(1)
    qb = q_ref.shape[0]
    n_chunks = ((n + 1) * qb + KEY_CHUNK - 1) // KEY_CHUNK
    lane = lax.broadcasted_iota(I32, (qb, LANES), 1)
    iq = iq_ref[...]
    for h in range(IDX_HEADS):
        lhs_sc[h // 2, (h % 2) * qb:(h % 2 + 1) * qb, :] = _idx_head_lhs(iq, h, lane)
    small_t = small_t_ref[...]
    weights = [small_t[idx_dim + h:idx_dim + h + 1, :] * (idx_dim ** -0.5) for h in range(IDX_HEADS)]
    qpos = lax.broadcasted_iota(I32, (KEY_CHUNK, qb), 1) + n * qb
    krow = lax.broadcasted_iota(I32, (KEY_CHUNK, qb), 0)

    def chunk_rows(c):
        return pl.ds(pl.multiple_of(c * KEY_CHUNK, KEY_CHUNK), KEY_CHUNK)

    def score_chunk(c, carry):
        ikd = ikd_ref[chunk_rows(c), :]
        score = jnp.zeros((KEY_CHUNK, qb), F32)
        for pair in range(IDX_HEADS // 2):
            d = jnp.maximum(_dot_nt(ikd, lhs_sc[pair]), 0.0)
            score = score + weights[2 * pair] * d[:, :qb] + weights[2 * pair + 1] * d[:, qb:]
        admissible = krow + c * KEY_CHUNK <= qpos
        key_sc[c] = jnp.where(admissible, _order_key(score), INT_MIN)
        return carry

    lax.fori_loop(0, n_chunks, score_chunk, 0)

    def count_ge(cand):
        return lax.fori_loop(
            0, n_chunks,
            lambda c, acc: acc + jnp.sum((key_sc[c] >= cand).astype(I32), axis=0, keepdims=True),
            jnp.zeros((1, qb), I32))

    tau = jnp.maximum(_kth_largest_key(count_ge, top_k, (1, qb)), INT_MIN + 1)
    scale = LANES ** -0.5
    heads = [slice(h * LANES, (h + 1) * LANES) for h in range(att_heads)]

    def attend_chunk(c, carry):
        sel = key_sc[c] >= tau
        out = []
        for sl, (m_old, l_old, acc) in zip(heads, carry):
            s = jnp.where(sel, _dot_nt(k_ref[chunk_rows(c), sl], q_ref[:, sl]) * scale, NEG_BIG)
            m_new = jnp.maximum(m_old, jnp.max(s, axis=0, keepdims=True))
            alpha = jnp.exp(m_old - m_new)
            p = jnp.exp(s - m_new)
            l_new = alpha * l_old + jnp.sum(p, axis=0, keepdims=True)
            out.append((m_new, l_new, alpha * acc + _dot(v_t_ref[c, sl, :], p.astype(BF16))))
        return tuple(out)

    init = tuple((jnp.full((1, qb), NEG_BIG, F32), jnp.zeros((1, qb), F32), jnp.zeros((LANES, qb), F32))
                 for _ in heads)
    for sl, (_, l_fin, acc) in zip(heads, lax.fori_loop(0, n_chunks, attend_chunk, init)):
        o_ref[:, sl] = (acc / l_fin).T.astype(o_ref.dtype)


def dsa_prompt(q_bf, iq_bf, small, k_bf, v_bf, ikd_bf, idx_dim):
    b, t, aw = q_bf.shape
    assert t % KEY_CHUNK == 0 and KEY_CHUNK % QBLOCK == 0
    top_k = min(TOPK_MAX, t // 4)
    n_kc = t // KEY_CHUNK
    small_t = jnp.swapaxes(small, 1, 2)
    v_t = jnp.swapaxes(v_bf.reshape(b, n_kc, KEY_CHUNK, aw), 2, 3)
    qrow = lambda w: pl.BlockSpec((None, QBLOCK, w), lambda bi, n: (bi, n, 0))
    full = lambda w: pl.BlockSpec((None, t, w), lambda bi, n: (bi, 0, 0))
    return pl.pallas_call(
        functools.partial(_dsa_prompt_kernel, top_k=top_k, att_heads=aw // LANES, idx_dim=idx_dim),
        grid=(b, t // QBLOCK),
        in_specs=[qrow(aw), qrow(iq_bf.shape[-1]),
                  pl.BlockSpec((None, LANES, QBLOCK), lambda bi, n: (bi, 0, n)),
                  full(aw),
                  pl.BlockSpec((None, n_kc, aw, KEY_CHUNK), lambda bi, n: (bi, 0, 0, 0)),
                  full(LANES)],
        out_specs=qrow(aw),
        out_shape=jax.ShapeDtypeStruct((b, t, aw), BF16),
        scratch_shapes=[pltpu.VMEM((IDX_HEADS // 2, 2 * QBLOCK, LANES), BF16),
                        pltpu.VMEM((n_kc, KEY_CHUNK, QBLOCK), I32)],
        compiler_params=_cparams(("arbitrary", "arbitrary")),
        name="dsa_prompt",
    )(q_bf, iq_bf, small_t, k_bf, v_t, ikd_bf)


def _idx_lhs_all_heads(iq, lane):
    iq = iq.astype(F32)
    return jnp.concatenate([_idx_head_lhs(iq, h, lane) for h in range(IDX_HEADS)], axis=0).astype(BF16)


def _idx_scores(lhs, small, keys_dup, lane, idx_dim):
    d = _dot_nt(lhs, keys_dup) * (idx_dim ** -0.5)
    score = jnp.zeros((SAMPLE_T_PAD, keys_dup.shape[0]), F32)
    for h in range(IDX_HEADS):
        dh = d[h * SAMPLE_T_PAD:(h + 1) * SAMPLE_T_PAD, :]
        score = score + _idx_weight(small, h, lane, idx_dim) * jnp.maximum(dh, 0.0)
    return score


def _sample_idx_kernel(pt_ref, iq_ref, small_ref, *rest, idx_dim):
    page_refs, o_ref = rest[:-1], rest[-1]
    lane = lax.broadcasted_iota(I32, (SAMPLE_T_PAD, LANES), 1)
    keys = jnp.concatenate([r[...] for r in page_refs], axis=0)
    keys_dup = jnp.concatenate([keys, keys], axis=-1).astype(BF16)
    lhs = _idx_lhs_all_heads(iq_ref[...], lane)
    o_ref[...] = _idx_scores(lhs, small_ref[...], keys_dup, lane, idx_dim)


def sample_idx_scores(page_table, iq_bf, small, cache_idx_k, layer):
    b, n_pages = page_table.shape
    page, idx_dim = cache_idx_k.shape[-2:]
    g = IDX_PAGES_PER_STEP if n_pages % IDX_PAGES_PER_STEP == 0 else 1

    def page_spec(j):
        return pl.BlockSpec((None, None, page, idx_dim),
                            lambda bi, s, pt: (layer, pt[bi * n_pages + s * g + j], 0, 0))

    row = lambda w: pl.BlockSpec((None, SAMPLE_T_PAD, w), lambda bi, s, pt: (bi, 0, 0))
    grid_spec = pltpu.PrefetchScalarGridSpec(
        num_scalar_prefetch=1,
        grid=(b, n_pages // g),
        in_specs=[row(iq_bf.shape[-1]), row(LANES)] + [page_spec(j) for j in range(g)],
        out_specs=pl.BlockSpec((None, SAMPLE_T_PAD, g * page), lambda bi, s, pt: (bi, 0, s)),
    )
    return pl.pallas_call(
        functools.partial(_sample_idx_kernel, idx_dim=idx_dim),
        grid_spec=grid_spec,
        out_shape=jax.ShapeDtypeStruct((b, SAMPLE_T_PAD, n_pages * page), F32),
        compiler_params=_cparams(("arbitrary", "arbitrary")),
        name="sample_idx",
    )(page_table.reshape(-1), iq_bf, small, *([cache_idx_k] * g))


def _sample_attn_kernel(pt_ref, sc_all_ref, sc_ref, q_ref, iq_ref, small_ref, ikn_ref, kn_ref, vn_ref,
                        *rest, top_k, n_new, att_heads, idx_dim, n_groups):
    g = (len(rest) - 7) // 2
    k_pages, v_pages = rest[:g], rest[g:2 * g]
    o_ref, qbd_ref, tau_ref, keyn_ref, m_ref, l_ref, acc_ref = rest[2 * g:]
    s_idx = pl.program_id(1)
    rows = att_heads * SAMPLE_T_PAD
    aw = att_heads * LANES
    scale = LANES ** -0.5

    @pl.when(s_idx == 0)
    def _():
        lane = lax.broadcasted_iota(I32, (SAMPLE_T_PAD, LANES), 1)
        tok = lax.broadcasted_iota(I32, (SAMPLE_T_PAD, LANES), 0)
        lhs = _idx_lhs_all_heads(iq_ref[...], lane)
        sc_new = _idx_scores(lhs, small_ref[...], ikn_ref[...], lane, idx_dim)
        new_ok = (lane <= tok) & (lane < n_new)
        key_new = jnp.where(new_ok, _order_key(sc_new), INT_MIN)
        key_past = _order_key(sc_all_ref[...])

        def count_ge(cand):
            return (jnp.sum((key_past >= cand).astype(I32), axis=-1, keepdims=True)
                    + jnp.sum((key_new >= cand).astype(I32), axis=-1, keepdims=True))

        tau = _kth_largest_key(count_ge, top_k, (SAMPLE_T_PAD, 1))
        tau_ref[...] = jnp.broadcast_to(tau, (SAMPLE_T_PAD, LANES))
        keyn_ref[...] = key_new
        q_rep = jnp.concatenate([q_ref[...].astype(F32)] * att_heads, axis=0)
        r_head = lax.broadcasted_iota(I32, (rows, aw), 0) // SAMPLE_T_PAD
        c_head = lax.broadcasted_iota(I32, (rows, aw), 1) // LANES
        qbd_ref[...] = jnp.where(r_head == c_head, q_rep, 0.0).astype(BF16)
        m_ref[...] = jnp.full(m_ref.shape, NEG_BIG, F32)
        l_ref[...] = jnp.zeros(l_ref.shape, F32)
        acc_ref[...] = jnp.zeros(acc_ref.shape, F32)

    tau = tau_ref[:, 0:1]

    def attend(sel8, key_of_head, val_of_head):
        sel = jnp.concatenate([sel8.astype(F32)] * att_heads, axis=0) > 0.5
        s = _dot_nt(qbd_ref[:, 0:LANES], key_of_head(0))
        for h in range(1, att_heads):
            s = s + _dot_nt(qbd_ref[:, h * LANES:(h + 1) * LANES], key_of_head(h))
        s = jnp.where(sel, s * scale, NEG_BIG)
        m_old = m_ref[:, 0:1]
        m_new = jnp.maximum(m_old, jnp.max(s, axis=-1, keepdims=True))
        alpha = jnp.exp(m_old - m_new)
        p = jnp.where(sel, jnp.exp(s - m_new), 0.0)
        l_ref[...] = jnp.broadcast_to(alpha * l_ref[:, 0:1] + jnp.sum(p, axis=-1, keepdims=True), l_ref.shape)
        p = p.astype(BF16)
        for h in range(att_heads):
            cs = slice(h * LANES, (h + 1) * LANES)
            acc_ref[:, cs] = alpha * acc_ref[:, cs] + _dot(p, val_of_head(h))
        m_ref[...] = jnp.broadcast_to(m_new, m_ref.shape)

    def paged(pages):
        return lambda h: jnp.concatenate([r[:, h, :] for r in pages], axis=0).astype(BF16)

    attend(_order_key(sc_ref[...]) >= tau, paged(k_pages), paged(v_pages))

    @pl.when(s_idx == n_groups - 1)
    def _():
        attend(keyn_ref[...] >= tau,
               lambda h: kn_ref[:, h * LANES:(h + 1) * LANES], lambda h: vn_ref[:, h * LANES:(h + 1) * LANES])
        inv_l = 1.0 / l_ref[:, 0:1]
        for h in range(att_heads):
            rs = slice(h * SAMPLE_T_PAD, (h + 1) * SAMPLE_T_PAD)
            cs = slice(h * LANES, (h + 1) * LANES)
            o_ref[:, cs] = (acc_ref[rs, cs] * inv_l[rs, :]).astype(o_ref.dtype)


def sample_attention(page_table, scores, q_bf, iq_bf, small, ikn_pad, kn_pad, vn_pad,
                     cache_k, cache_v, layer, n_new, idx_dim):
    b, n_pages = page_table.shape
    page = cache_k.shape[2]
    aw = q_bf.shape[-1]
    att_heads = aw // LANES
    past = n_pages * page
    top_k = min(TOPK_MAX, (past + n_new) // 4)
    g = PAGES_PER_STEP if n_pages % PAGES_PER_STEP == 0 else 1
    n_groups = n_pages // g
    rows = att_heads * SAMPLE_T_PAD

    def page_spec(j):
        return pl.BlockSpec((None, None, page, att_heads, LANES),
                            lambda bi, s, pt: (layer, pt[bi * n_pages + s * g + j], 0, 0, 0))

    def per_seq(r, w):
        return pl.BlockSpec((None, r, w), lambda bi, s, pt: (bi, 0, 0))

    grid_spec = pltpu.PrefetchScalarGridSpec(
        num_scalar_prefetch=1,
        grid=(b, n_groups),
        in_specs=[per_seq(SAMPLE_T_PAD, past),
                  pl.BlockSpec((None, SAMPLE_T_PAD, g * page), lambda bi, s, pt: (bi, 0, s)),
                  per_seq(SAMPLE_T_PAD, aw), per_seq(SAMPLE_T_PAD, iq_bf.shape[-1]),
                  per_seq(SAMPLE_T_PAD, LANES),
                  per_seq(LANES, LANES), per_seq(LANES, aw), per_seq(LANES, aw)]
                 + [page_spec(j) for j in range(g)] * 2,
        out_specs=per_seq(SAMPLE_T_PAD, aw),
        scratch_shapes=[
            pltpu.VMEM((rows, aw), BF16),
            pltpu.VMEM((SAMPLE_T_PAD, LANES), I32),
            pltpu.VMEM((SAMPLE_T_PAD, LANES), I32),
            pltpu.VMEM((rows, LANES), F32),
            pltpu.VMEM((rows, LANES), F32),
            pltpu.VMEM((rows, aw), F32),
        ],
    )
    return pl.pallas_call(
        functools.partial(_sample_attn_kernel, top_k=top_k, n_new=n_new, att_heads=att_heads,
                          idx_dim=idx_dim, n_groups=n_groups),
        grid_spec=grid_spec,
        out_shape=jax.ShapeDtypeStruct((b, SAMPLE_T_PAD, aw), BF16),
        compiler_params=_cparams(("arbitrary", "arbitrary")),
        name="sample_attn",
    )(page_table.reshape(-1), scores, scores, q_bf, iq_bf, small, ikn_pad, kn_pad, vn_pad,
      *([cache_k] * g), *([cache_v] * g))


HIST_ROWS = 32
CONV_ROWS = 32


def _conv_kernel(*refs, width, multi_tile):
    if multi_tile:
        u_ref, prev_ref, hist_ref, cw_ref, cb_ref, g_ref, b_ref, o_ref, buf = refs
    else:
        u_ref, hist_ref, cw_ref, cb_ref, g_ref, b_ref, o_ref, buf = refs
    tt = u_ref.shape[0]
    buf[HIST_ROWS:HIST_ROWS + tt, :] = u_ref[...]
    if multi_tile:
        buf[0:HIST_ROWS, :] = jnp.where(pl.program_id(1) == 0, hist_ref[...], prev_ref[...])
    else:
        buf[0:HIST_ROWS, :] = hist_ref[...]
    first = HIST_ROWS - (width - 1)
    rows = min(CONV_ROWS, tt)
    cb, g, b = cb_ref[...], g_ref[...], b_ref[...]
    for r0 in range(0, tt, rows):
        acc = jnp.zeros((rows, u_ref.shape[1]), F32)
        for j in range(width):
            acc = acc + buf[r0 + first + j:r0 + first + j + rows, :] * cw_ref[j:j + 1, :]
        y = acc + cb
        yc = y - jnp.mean(y, axis=-1, keepdims=True)
        var = jnp.mean(yc * yc, axis=-1, keepdims=True)
        y = yc * lax.rsqrt(var + LN_EPS) * g + b
        o_ref[r0:r0 + rows, :] = _silu(y).astype(o_ref.dtype)


def conv_branch(u, hist, cw, cb, ln_g, ln_b):
    b, t, c = u.shape
    width = cw.shape[0]
    tt = _row_tile(t, 256)
    multi_tile = t > tt
    row = pl.BlockSpec((None, tt, c), lambda bi, i: (bi, i, 0))
    vec = pl.BlockSpec((1, c), lambda bi, i: (0, 0))
    specs, args = [row], [u]
    if multi_tile:
        per = tt // HIST_ROWS
        specs.append(pl.BlockSpec((None, HIST_ROWS, c), lambda bi, i: (bi, jnp.maximum(i * per - 1, 0), 0)))
        args.append(u)
    specs += [pl.BlockSpec((None, HIST_ROWS, c), lambda bi, i: (bi, 0, 0)),
              pl.BlockSpec((width, c), lambda bi, i: (0, 0)), vec, vec, vec]
    args += [hist, cw, cb.reshape(1, c), ln_g.reshape(1, c), ln_b.reshape(1, c)]
    return pl.pallas_call(
        functools.partial(_conv_kernel, width=width, multi_tile=multi_tile),
        grid=(b, t // tt),
        in_specs=specs,
        out_specs=row,
        out_shape=jax.ShapeDtypeStruct((b, t, c), BF16),
        scratch_shapes=[pltpu.VMEM((HIST_ROWS + tt, c), F32)],
        compiler_params=_cparams(("arbitrary", "arbitrary")),
        name="conv_branch",
    )(*args)


def _rw_mix_kernel(h_ref, prev_ref, shift_ref, mu_ref, o_ref, buf):
    tt = h_ref.shape[0]
    h = h_ref[...]
    buf[SUBLANES:SUBLANES + tt, :] = h
    buf[SUBLANES - 1:SUBLANES, :] = jnp.where(pl.program_id(1) == 0, shift_ref[...],
                                              prev_ref[SUBLANES - 1:SUBLANES, :])
    xx = buf[SUBLANES - 1:SUBLANES - 1 + tt, :] - h
    for j in range(o_ref.shape[0]):
        o_ref[j] = (h + xx * mu_ref[j:j + 1, :]).astype(o_ref.dtype)


def rw_mix(h, shift_prev, mu):
    b, t, d = h.shape
    n_mix = mu.shape[0]
    tt = _row_tile(t, 256)
    per = tt // SUBLANES
    return pl.pallas_call(
        _rw_mix_kernel,
        grid=(b, t // tt),
        in_specs=[
            pl.BlockSpec((None, tt, d), lambda bi, i: (bi, i, 0)),
            pl.BlockSpec((None, SUBLANES, d), lambda bi, i: (bi, jnp.maximum(i * per - 1, 0), 0)),
            pl.BlockSpec((None, 1, d), lambda bi, i: (bi, 0, 0)),
            pl.BlockSpec((n_mix, d), lambda bi, i: (0, 0)),
        ],
        out_specs=pl.BlockSpec((n_mix, None, tt, d), lambda bi, i: (0, bi, i, 0)),
        out_shape=jax.ShapeDtypeStruct((n_mix, b, t, d), BF16),
        scratch_shapes=[pltpu.VMEM((SUBLANES + tt, d), F32)],
        compiler_params=_cparams(("arbitrary", "arbitrary")),
        name="rw_mix",
    )(h, h, shift_prev.reshape(b, 1, d), mu)


RW_HEAD = 64


def _seg_sum(x, ones_blk):
    outs = []
    for s in range(x.shape[1] // LANES):
        xs = x[:, s * LANES:(s + 1) * LANES]
        hi = xs.astype(BF16)
        lo = (xs - hi.astype(F32)).astype(BF16)
        outs.append(_dot(hi, ones_blk) + _dot(lo, ones_blk))
    return jnp.concatenate(outs, axis=-1)


def _wkv_kernel(r_ref, k_ref, v_ref, wl_ref, al_ref, g_ref,
                w0_ref, a0_ref, kk_ref, ka_ref, rk_ref, lg_ref, lb_ref, s0_ref,
                o_ref, s_ref, dec_sc, kk_sc, b_sc, kh_sc, y_sc, *, n_steps):
    i = pl.program_id(1)
    tc, d = r_ref.shape
    rb = lax.broadcasted_iota(I32, (LANES, LANES), 0) // RW_HEAD
    cb = lax.broadcasted_iota(I32, (LANES, LANES), 1) // RW_HEAD
    ones_blk = jnp.where(rb == cb, 1.0, 0.0).astype(BF16)
    eye2 = (lax.broadcasted_iota(I32, (RW_HEAD, LANES), 0)
            == lax.broadcasted_iota(I32, (RW_HEAD, LANES), 1) % RW_HEAD)

    @pl.when(i == 0)
    def _():
        s_ref[...] = s0_ref[...]

    k = k_ref[...]
    z = -(w0_ref[...] + wl_ref[...])
    softplus = jnp.maximum(z, 0.0) + jnp.log(1.0 + jnp.exp(-jnp.abs(z)))
    dec_sc[...] = jnp.exp(-jnp.exp(-softplus - 0.5))
    a = jax.nn.sigmoid(a0_ref[...] + al_ref[...])
    kk = k * kk_ref[...]
    kk = kk * lax.rsqrt(jnp.maximum(_seg_sum(kk * kk, ones_blk), 1e-24))
    kk_sc[...] = kk
    b_sc[...] = kk * a
    kh_sc[...] = k * (1.0 + (a - 1.0) * ka_ref[...])
    assert -(-n_steps // SUBLANES) * SUBLANES == tc, "the recurrence blocks must cover the whole time tile"

    steps_per_block = min(SUBLANES, n_steps)

    slabs = [slice(s * LANES, (s + 1) * LANES) for s in range(d // LANES)]
    head_sel = (lax.broadcasted_iota(I32, (SUBLANES, LANES), 0)
                == lax.broadcasted_iota(I32, (SUBLANES, LANES), 1) // RW_HEAD).astype(BF16)

    def block(blk, carry):
        rows = pl.ds(pl.multiple_of(blk * SUBLANES, SUBLANES), SUBLANES)
        st = [s_ref[:, sl] for sl in slabs]
        y_rows = [[] for _ in slabs]
        for j in range(steps_per_block):
            sa, vb = [], []
            for s, sl in enumerate(slabs):
                lhs = jnp.concatenate([st[s] * kk_sc[rows, sl][j:j + 1],
                                       jnp.where(eye2, v_ref[rows, sl][j:j + 1], 0.0)], axis=0)
                both = _dot(lhs.astype(BF16), ones_blk)
                sa.append(both[:RW_HEAD])
                vb.append(both[RW_HEAD:])
            for s, sl in enumerate(slabs):
                st[s] = (st[s] * dec_sc[rows, sl][j:j + 1] - sa[s] * b_sc[rows, sl][j:j + 1]
                         + vb[s] * kh_sc[rows, sl][j:j + 1])
            for s, sl in enumerate(slabs):
                yh = _dot_nt(head_sel, (st[s] * r_ref[rows, sl][j:j + 1]).astype(BF16))
                y_rows[s].append(jnp.concatenate([yh[0:1], yh[1:2]], axis=-1))
        for s, sl in enumerate(slabs):
            s_ref[:, sl] = st[s]
            pad = [jnp.zeros((1, LANES), F32)] * (SUBLANES - steps_per_block)
            y_sc[rows, sl] = jnp.concatenate(y_rows[s] + pad, axis=0)
        return carry

    lax.fori_loop(0, -(-n_steps // SUBLANES), block, 0)

    y = y_sc[...]
    inv_n = 1.0 / RW_HEAD
    yc = y - _seg_sum(y, ones_blk) * inv_n
    var = _seg_sum(yc * yc, ones_blk) * inv_n
    y = yc * lax.rsqrt(var + LNX_EPS) * lg_ref[...] + lb_ref[...]
    r = r_ref[...]
    y = y + _seg_sum(r * kh_sc[...] * rk_ref[...], ones_blk) * v_ref[...]
    o_ref[...] = (y * g_ref[...]).astype(o_ref.dtype)


def wkv(r, k, v, wl, al, g, w0, a0, k_k, k_a, r_k, lnx_g, lnx_b, s0, n_steps):
    b, t, d = r.shape
    tc = _row_tile(t, 128)
    steps = tc if t > tc else n_steps
    row = pl.BlockSpec((None, tc, d), lambda bi, i: (bi, i, 0))
    vec = pl.BlockSpec((1, d), lambda bi, i: (0, 0))
    st = pl.BlockSpec((None, RW_HEAD, d), lambda bi, i: (bi, 0, 0))
    vecs = [x.reshape(1, d) for x in (w0, a0, k_k, k_a, r_k, lnx_g, lnx_b)]
    return pl.pallas_call(
        functools.partial(_wkv_kernel, n_steps=steps),
        grid=(b, t // tc),
        in_specs=[row] * 6 + [vec] * 7 + [st],
        out_specs=[row, st],
        out_shape=[jax.ShapeDtypeStruct((b, t, d), BF16), jax.ShapeDtypeStruct((b, RW_HEAD, d), F32)],
        scratch_shapes=[pltpu.VMEM((tc, d), F32)] * 5,
        compiler_params=_cparams(("arbitrary", "arbitrary")),
        name="wkv",
    )(r, k, v, wl, al, g, *vecs, s0)


def _state_to_kernel_layout(s):
    b, h, n, _ = s.shape
    return jnp.transpose(s, (0, 2, 1, 3)).reshape(b, n, h * n)


def _state_from_kernel_layout(s, heads):
    b, n, _ = s.shape
    return jnp.transpose(s.reshape(b, n, heads, n), (0, 2, 1, 3))


def _run_group(x, mod_all, t_real, positions, sample_ctx, weights):
    (norm_g, ffn_w1, ffn_w3, ffn_w2, att_w_in, att_w_out, q_norm_g, k_norm_g, idx_k_norm_g,
     conv_w, conv_b, conv_ln_g, conv_ln_b, rw_mu, rw_w0, rw_w1, rw_w2, rw_a0, rw_a1, rw_a2, rw_g1, rw_g2,
     rw_k_k, rw_k_a, rw_r_k, rw_wr, rw_wk, rw_wv, rw_wo, rw_lnx_g, rw_lnx_b) = weights
    b, t, d = x.shape
    depth = norm_g.shape[0]
    aw = att_w_out.shape[1] - conv_w.shape[2]
    att_heads = aw // LANES
    cc = conv_w.shape[2]
    conv_width = conv_w.shape[1]
    idx_dim = idx_k_norm_g.shape[1]
    iq_w = IDX_HEADS * idx_dim
    rw_heads = d // RW_HEAD
    tables = _rope_tables(positions, LANES) + _rope_tables(positions, idx_dim)

    outs = dict(k=[], v=[], ik=[], conv=[], shift=[], wkv=[])
    h = resid_norm(x, mod_all[0], g=norm_g[0, 0], shift_idx=0, scale_idx=1, emit_x=False, h_dtype=BF16)
    for l in range(depth):
        i = l // 2
        mod = mod_all[l]
        even = l % 2 == 0
        y = ffn(h, ffn_w1, ffn_w3, ffn_w2, l, 0)
        x, h = resid_norm(x, mod, y=y, gate_idx=2, coef=HALF_STEP, g=norm_g[l, 1], shift_idx=3, scale_idx=4,
                          h_dtype=BF16 if even else F32)
        if even:
            qkvi = matmul(h, att_w_in, lead=i, n_cols=3 * aw + iq_w)
            w_small = jnp.pad(att_w_in[i][:, 3 * aw + iq_w:3 * aw + iq_w + idx_dim + IDX_HEADS],
                              ((0, 0), (0, LANES - idx_dim - IDX_HEADS)))
            small = matmul(h, w_small)
            u = matmul(h, att_w_in[i][:, 3 * aw + iq_w + idx_dim + IDX_HEADS:])
            q_bf, k_f, k_bf, v_bf, iq_bf, small, ikd_bf, glu = even_post(
                qkvi, small, u, tables, q_norm_g[i], k_norm_g[i], idx_k_norm_g[i], att_heads, idx_dim)
            if sample_ctx is None:
                att = dsa_prompt(q_bf, iq_bf, small, k_bf, v_bf, ikd_bf, idx_dim)
                hist = jnp.zeros((b, HIST_ROWS, cc), F32)
                outs["conv"].append(glu[:, t - (conv_width - 1):])
            else:
                pt = sample_ctx["page_table"]
                scores = sample_idx_scores(pt, iq_bf, small, sample_ctx["cache_idx_k"], i)
                pad_rows = lambda a: jnp.pad(a, ((0, 0), (0, LANES - t), (0, 0)))
                att = sample_attention(pt, scores, q_bf, iq_bf, small, pad_rows(ikd_bf), pad_rows(k_bf),
                                       pad_rows(v_bf), sample_ctx["cache_k"], sample_ctx["cache_v"], i,
                                       t_real, idx_dim)
                state = sample_ctx["state_conv"][i]
                hist = jnp.pad(state, ((0, 0), (HIST_ROWS - (conv_width - 1), 0), (0, 0)))
                outs["conv"].append(jnp.concatenate([state, glu[:, :t_real]], axis=1)[:, -(conv_width - 1):])
            conv_y = conv_branch(glu, hist, conv_w[i], conv_b[i], conv_ln_g[i], conv_ln_b[i])
            mixed = matmul(jnp.concatenate([att, conv_y], axis=-1), att_w_out, lead=i)
            outs["k"].append(k_f[:, :t_real].reshape(b, t_real, att_heads, LANES))
            outs["v"].append(qkvi[:, :t_real, 2 * aw:3 * aw].reshape(b, t_real, att_heads, LANES))
            outs["ik"].append(small[:, :t_real, :idx_dim])
        else:
            if sample_ctx is None:
                shift_prev = jnp.zeros((b, d), F32)
                s0 = jnp.zeros((b, RW_HEAD, d), F32)
            else:
                shift_prev = sample_ctx["state_shift"][i]
                s0 = _state_to_kernel_layout(sample_ctx["state_wkv"][i])
            xs = rw_mix(h, shift_prev, rw_mu[i])
            r = matmul(xs[0], rw_wr, lead=i)
            wl = lora(xs[1], rw_w1, rw_w2, i, "tanh")
            k = matmul(xs[2], rw_wk, lead=i)
            v = matmul(xs[3], rw_wv, lead=i)
            al = lora(xs[4], rw_a1, rw_a2, i, "none")
            g = lora(xs[5], rw_g1, rw_g2, i, "sigmoid")
            yg, s_fin = wkv(r, k, v, wl, al, g, rw_w0[i], rw_a0[i], rw_k_k[i], rw_k_a[i], rw_r_k[i],
                            rw_lnx_g[i], rw_lnx_b[i], s0, t_real)
            mixed = matmul(yg, rw_wo, lead=i)
            outs["shift"].append(h[:, t_real - 1])
            outs["wkv"].append(_state_from_kernel_layout(s_fin, rw_heads))
        x, h = resid_norm(x, mod, y=mixed, gate_idx=5, coef=1.0, g=norm_g[l, 2], shift_idx=6, scale_idx=7,
                          h_dtype=BF16)
        y = ffn(h, ffn_w1, ffn_w3, ffn_w2, l, 1)
        if l + 1 < depth:
            x, h = resid_norm(x, mod, y=y, gate_idx=8, coef=HALF_STEP, g=norm_g[l + 1, 0],
                              shift_idx=0, scale_idx=1, h_dtype=BF16, mod_norm=mod_all[l + 1])
        else:
            x = resid_norm(x, mod, y=y, gate_idx=8, coef=HALF_STEP)
    return x[:, :t_real], outs


def kernel(x_prompt, x_sample, cache_k, cache_v, cache_idx_k, state_conv, state_shift, state_wkv, page_table, c_prompt, c_sample, norm_g, ada_w, ada_b, ffn_w1, ffn_w3, ffn_w2, att_w_in, att_w_out, q_norm_g, k_norm_g, idx_k_norm_g, conv_w, conv_b, conv_ln_g, conv_ln_b, rw_mu, rw_w0, rw_w1, rw_w2, rw_a0, rw_a1, rw_a2, rw_g1, rw_g2, rw_k_k, rw_k_a, rw_r_k, rw_wr, rw_wk, rw_wv, rw_wo, rw_lnx_g, rw_lnx_b):
    bp, tp, d = x_prompt.shape
    bs, ts, _ = x_sample.shape
    depth = norm_g.shape[0]
    past = page_table.shape[1] * cache_k.shape[2]

    n_c = bp + bs
    c_rows = -(-n_c // 16) * 16
    c_all = jnp.pad(jnp.concatenate([c_prompt, c_sample], axis=0), ((0, c_rows - n_c), (0, 0)))
    mod = adaln_all(c_all, ada_w, ada_b).reshape(depth, c_rows, N_MOD, 1, d)
    mod_p, mod_s = mod[:, :bp], mod[:, bp:n_c]

    weights = (norm_g, ffn_w1, ffn_w3, ffn_w2, att_w_in, att_w_out, q_norm_g, k_norm_g, idx_k_norm_g,
               conv_w, conv_b, conv_ln_g, conv_ln_b, rw_mu, rw_w0, rw_w1, rw_w2, rw_a0, rw_a1, rw_a2,
               rw_g1, rw_g2, rw_k_k, rw_k_a, rw_r_k.reshape(rw_r_k.shape[0], -1), rw_wr, rw_wk, rw_wv, rw_wo,
               rw_lnx_g, rw_lnx_b)

    yp, op = _run_group(x_prompt, mod_p, tp, np.arange(tp), None, weights)
    xs_pad = jnp.pad(x_sample, ((0, 0), (0, SAMPLE_T_PAD - ts), (0, 0)))
    sample_ctx = dict(page_table=page_table, cache_k=cache_k, cache_v=cache_v, cache_idx_k=cache_idx_k,
                      state_conv=state_conv, state_shift=state_shift, state_wkv=state_wkv)
    ys, os_ = _run_group(xs_pad, mod_s, ts, past + np.arange(SAMPLE_T_PAD), sample_ctx, weights)

    st = lambda xs: jnp.stack(xs)
    return (yp, ys,
            st(op["k"]), st(op["v"]), st(op["ik"]), st(op["conv"]), st(op["shift"]), st(op["wkv"]),
            st(os_["k"]), st(os_["v"]), st(os_["ik"]), st(os_["conv"]), st(os_["shift"]), st(os_["wkv"]))
```

```python
import functools
import math

import numpy as np
import jax
import jax.numpy as jnp
from jax import lax
from jax.experimental import pallas as pl
from jax.experimental.pallas import tpu as pltpu

F32 = jnp.float32
BF16 = jnp.bfloat16
I32 = jnp.int32

ROPE_THETA = 10000.0
NORM_EPS = 1e-6
LN_EPS = 1e-5
LNX_EPS = 64e-5
N_MOD = 9
HALF_STEP = 0.5
TOPK_MAX = 256
QBLOCK = 128
IDX_HEADS = 16

LANES = 128
SUBLANES = 8
VMEM_LIMIT_MB = 56
NEG_BIG = -1e30
INT_MIN = -(2 ** 31)

SAMPLE_T_PAD = 8
PAGES_PER_STEP = 4
IDX_PAGES_PER_STEP = 8


def _cparams(sem, vmem_mb=VMEM_LIMIT_MB):
    return pltpu.CompilerParams(dimension_semantics=sem, vmem_limit_bytes=vmem_mb * 1024 * 1024)


def _silu(x):
    return x * jax.nn.sigmoid(x)


def _dot(a, b):
    return jnp.dot(a, b, preferred_element_type=F32)


def _dot_nt(a, b):
    return lax.dot_general(a, b, (((1,), (1,)), ((), ())), preferred_element_type=F32)


def _row_tile(t, target):
    return t if t <= target else target


def _adaln_kernel(c_ref, w_ref, b_ref, o_ref):
    sc = _silu(c_ref[...]).astype(BF16)
    o_ref[...] = _dot(sc, w_ref[...].astype(BF16)) + b_ref[...]


def adaln_all(c_all, ada_w, ada_b):
    depth, d, n = ada_w.shape
    rows = c_all.shape[0]
    tn = 1024
    return pl.pallas_call(
        _adaln_kernel,
        grid=(depth, n // tn),
        in_specs=[
            pl.BlockSpec((rows, d), lambda l, j: (0, 0)),
            pl.BlockSpec((None, d, tn), lambda l, j: (l, 0, j)),
            pl.BlockSpec((None, 1, tn), lambda l, j: (l, 0, j)),
        ],
        out_specs=pl.BlockSpec((None, rows, tn), lambda l, j: (l, 0, j)),
        out_shape=jax.ShapeDtypeStruct((depth, rows, n), F32),
        compiler_params=_cparams(("arbitrary", "arbitrary")),
        name="adaln",
    )(c_all, ada_w, ada_b.reshape(depth, 1, n))


def _resid_norm_kernel(*refs, has_y, coef, emit_x, emit_h):
    refs = list(refs)
    x = refs.pop(0)[...]
    if has_y:
        y = refs.pop(0)[...]
        gate = refs.pop(0)[...]
        x = x + (coef * gate) * y
    if emit_h:
        g = refs.pop(0)[...]
        shift = refs.pop(0)[...]
        scale = refs.pop(0)[...]
    if emit_x:
        refs.pop(0)[...] = x
    if emit_h:
        h_ref = refs.pop(0)
        ms = jnp.mean(x * x, axis=-1, keepdims=True)
        h = x * lax.rsqrt(ms + NORM_EPS) * g
        h_ref[...] = (h * (1.0 + scale) + shift).astype(h_ref.dtype)


def resid_norm(x, mod, *, y=None, gate_idx=None, coef=1.0, g=None, shift_idx=None, scale_idx=None,
               emit_x=True, h_dtype=None, mod_norm=None):
    mod_norm = mod if mod_norm is None else mod_norm
    b, t, d = x.shape
    tt = _row_tile(t, 256)
    has_y = y is not None
    emit_h = h_dtype is not None
    row = pl.BlockSpec((None, tt, d), lambda bi, i: (bi, i, 0))

    def mod_spec(idx):
        return pl.BlockSpec((None, None, 1, d), lambda bi, i: (bi, idx, 0, 0))

    args, specs = [x], [row]
    if has_y:
        args += [y, mod]
        specs += [row, mod_spec(gate_idx)]
    if emit_h:
        args += [g.reshape(1, d), mod_norm, mod_norm]
        specs += [pl.BlockSpec((1, d), lambda bi, i: (0, 0)), mod_spec(shift_idx), mod_spec(scale_idx)]
    out_shape, out_specs = [], []
    if emit_x:
        out_shape.append(jax.ShapeDtypeStruct((b, t, d), F32))
        out_specs.append(row)
    if emit_h:
        out_shape.append(jax.ShapeDtypeStruct((b, t, d), h_dtype))
        out_specs.append(row)
    outs = pl.pallas_call(
        functools.partial(_resid_norm_kernel, has_y=has_y, coef=coef, emit_x=emit_x, emit_h=emit_h),
        grid=(b, t // tt),
        in_specs=specs,
        out_specs=out_specs,
        out_shape=out_shape,
        compiler_params=_cparams(("arbitrary", "arbitrary")),
        name="resid_norm",
    )(*args)
    return outs if len(outs) > 1 else outs[0]


def _ffn_kernel(h_ref, w1_ref, w3_ref, w2_ref, o_ref):
    f = pl.program_id(1)
    h = h_ref[...]
    a = _dot(h, w1_ref[...].astype(BF16))
    b = _dot(h, w3_ref[...].astype(BF16))
    z = (_silu(a) * b).astype(BF16)

    @pl.when(f == 0)
    def _():
        o_ref[...] = jnp.zeros(o_ref.shape, o_ref.dtype)

    o_ref[...] += _dot(z, w2_ref[...].astype(BF16))


def ffn(h, w1, w3, w2, layer, slot):
    b, t, d = h.shape
    m = b * t
    d_ff = w1.shape[-1]
    tm = 1024 if m >= 1024 else m
    tf = 256 if m >= 1024 else 512
    out = pl.pallas_call(
        _ffn_kernel,
        grid=(m // tm, d_ff // tf),
        in_specs=[
            pl.BlockSpec((tm, d), lambda i, f: (i, 0)),
            pl.BlockSpec((None, None, d, tf), lambda i, f: (layer, slot, 0, f)),
            pl.BlockSpec((None, None, d, tf), lambda i, f: (layer, slot, 0, f)),
            pl.BlockSpec((None, None, tf, d), lambda i, f: (layer, slot, f, 0)),
        ],
        out_specs=pl.BlockSpec((tm, d), lambda i, f: (i, 0)),
        out_shape=jax.ShapeDtypeStruct((m, d), F32),
        compiler_params=_cparams(("arbitrary", "arbitrary")),
        name="ffn",
    )(h.reshape(m, d), w1, w3, w2)
    return out.reshape(b, t, d)


def _mm_kernel(x_ref, w_ref, o_ref):
    o_ref[...] = _dot(x_ref[...], w_ref[...].astype(BF16)).astype(o_ref.dtype)


def matmul(x, w, lead=None, n_cols=None, out_dtype=F32):
    b, t, k = x.shape
    m = b * t
    n = w.shape[-1] if n_cols is None else n_cols
    tm = 1024 if m >= 1024 else m
    tn = 512 if n % 512 == 0 else n
    if lead is None:
        w_spec = pl.BlockSpec((k, tn), lambda i, j: (0, j))
    else:
        w_spec = pl.BlockSpec((None, k, tn), lambda i, j: (lead, 0, j))
    out = pl.pallas_call(
        _mm_kernel,
        grid=(m // tm, n // tn),
        in_specs=[pl.BlockSpec((tm, k), lambda i, j: (i, 0)), w_spec],
        out_specs=pl.BlockSpec((tm, tn), lambda i, j: (i, j)),
        out_shape=jax.ShapeDtypeStruct((m, n), out_dtype),
        compiler_params=_cparams(("arbitrary", "arbitrary")),
        name="matmul",
    )(x.reshape(m, k), w)
    return out.reshape(b, t, n)


def _lora_kernel(x_ref, a_ref, b_ref, o_ref, *, act):
    h = _dot(x_ref[...], a_ref[...].astype(BF16))
    if act == "tanh":
        h = jnp.tanh(h)
    elif act == "sigmoid":
        h = jax.nn.sigmoid(h)
    o_ref[...] = _dot(h.astype(BF16), b_ref[...].astype(BF16))


def lora(x, a, bmat, layer, act):
    b, t, d = x.shape
    m = b * t
    r = a.shape[-1]
    n = bmat.shape[-1]
    tm = 512 if m >= 512 else m
    out = pl.pallas_call(
        functools.partial(_lora_kernel, act=act),
        grid=(m // tm,),
        in_specs=[
            pl.BlockSpec((tm, d), lambda i: (i, 0)),
            pl.BlockSpec((None, d, r), lambda i: (layer, 0, 0)),
            pl.BlockSpec((None, r, n), lambda i: (layer, 0, 0)),
        ],
        out_specs=pl.BlockSpec((tm, n), lambda i: (i, 0)),
        out_shape=jax.ShapeDtypeStruct((m, n), F32),
        compiler_params=_cparams(("arbitrary",)),
        name="lora",
    )(x.reshape(m, d), a, bmat)
    return out.reshape(b, t, n)


def _rope_tables(positions, head_dim):
    half = head_dim // 2
    inv = ROPE_THETA ** (-np.arange(half, dtype=np.float64) / half)
    ang = np.asarray(positions, np.float64)[:, None] * inv[None, :]
    cos = np.concatenate([np.cos(ang), np.cos(ang)], axis=-1)
    sin = np.concatenate([-np.sin(ang), np.sin(ang)], axis=-1)
    reps = LANES // head_dim
    return (jnp.asarray(np.tile(cos, (1, reps)), F32), jnp.asarray(np.tile(sin, (1, reps)), F32))


def _rope128(x, cos, sin):
    return x * cos + pltpu.roll(x, 64, 1) * sin


def _rope64(x, cos, sin, lane):
    first_half = (lane % 64) < 32
    partner = jnp.where(first_half, pltpu.roll(x, 96, 1), pltpu.roll(x, 32, 1))
    return x * cos + partner * sin


def _even_post_kernel(q_ref, k_ref, v_ref, iq_ref, small_ref, u_ref,
                      c128_ref, s128_ref, c64_ref, s64_ref, qg_ref, kg_ref, ikg_ref,
                      qo_ref, kf_ref, kb_ref, vb_ref, iqo_ref, smallo_ref, ikd_ref, uo_ref,
                      *, att_heads, idx_dim, idx_heads):
    c128, s128 = c128_ref[...], s128_ref[...]
    c64, s64 = c64_ref[...], s64_ref[...]
    tt = c128.shape[0]
    lane = lax.broadcasted_iota(I32, (tt, LANES), 1)
    qg, kg = qg_ref[...], kg_ref[...]
    for h in range(att_heads):
        sl = slice(h * LANES, (h + 1) * LANES)
        q = q_ref[:, sl]
        q = q * lax.rsqrt(jnp.mean(q * q, axis=-1, keepdims=True) + NORM_EPS) * qg
        qo_ref[:, sl] = _rope128(q, c128, s128).astype(BF16)
        k = k_ref[:, sl]
        k = k * lax.rsqrt(jnp.mean(k * k, axis=-1, keepdims=True) + NORM_EPS) * kg
        k = _rope128(k, c128, s128)
        kf_ref[:, sl] = k
        kb_ref[:, sl] = k.astype(BF16)
        vb_ref[:, sl] = v_ref[:, sl].astype(BF16)
    for p in range(idx_heads * idx_dim // LANES):
        sl = slice(p * LANES, (p + 1) * LANES)
        iqo_ref[:, sl] = _rope64(iq_ref[:, sl], c64, s64, lane).astype(BF16)
    small = small_ref[...]
    is_ik = lane < idx_dim
    ik = jnp.where(is_ik, small, 0.0)
    ms = jnp.sum(ik * ik, axis=-1, keepdims=True) * (1.0 / idx_dim)
    ik = _rope64(ik * lax.rsqrt(ms + NORM_EPS) * ikg_ref[...], c64, s64, lane)
    ik = jnp.where(is_ik, ik, 0.0)
    smallo_ref[...] = jnp.where(is_ik, ik, small * (idx_heads ** -0.5))
    ikd_ref[...] = (ik + pltpu.roll(ik, 64, 1)).astype(BF16)
    cc = u_ref.shape[-1] // 2
    uo_ref[...] = u_ref[:, :cc] * jax.nn.sigmoid(u_ref[:, cc:])


def even_post(qkvi, small, u, tables, q_g, k_g, ik_g, att_heads, idx_dim):
    b, t, _ = qkvi.shape
    aw = att_heads * LANES
    iw = IDX_HEADS * idx_dim
    cc = u.shape[-1] // 2
    tt = _row_tile(t, 256)
    c128, s128, c64, s64 = tables

    def row(width, col=0):
        return pl.BlockSpec((None, tt, width), lambda bi, i: (bi, i, col))

    tab = pl.BlockSpec((tt, LANES), lambda bi, i: (i, 0))
    vec = pl.BlockSpec((1, LANES), lambda bi, i: (0, 0))
    ikg_pad = jnp.zeros((1, LANES), F32).at[0, :idx_dim].set(ik_g)
    assert aw == iw, "q/k/v/indexer-q column groups are addressed as equal-width blocks"
    return pl.pallas_call(
        functools.partial(_even_post_kernel, att_heads=att_heads, idx_dim=idx_dim, idx_heads=IDX_HEADS),
        grid=(b, t // tt),
        in_specs=[row(aw, 0), row(aw, 1), row(aw, 2), row(iw, 3), row(LANES), row(2 * cc),
                  tab, tab, tab, tab, vec, vec, vec],
        out_specs=[row(aw), row(aw), row(aw), row(aw), row(iw), row(LANES), row(LANES), row(cc)],
        out_shape=[
            jax.ShapeDtypeStruct((b, t, aw), BF16),
            jax.ShapeDtypeStruct((b, t, aw), F32),
            jax.ShapeDtypeStruct((b, t, aw), BF16),
            jax.ShapeDtypeStruct((b, t, aw), BF16),
            jax.ShapeDtypeStruct((b, t, iw), BF16),
            jax.ShapeDtypeStruct((b, t, LANES), F32),
            jax.ShapeDtypeStruct((b, t, LANES), BF16),
            jax.ShapeDtypeStruct((b, t, cc), F32),
        ],
        compiler_params=_cparams(("arbitrary", "arbitrary")),
        name="even_post",
    )(qkvi, qkvi, qkvi, qkvi, small, u, c128, s128, c64, s64,
      q_g.reshape(1, LANES), k_g.reshape(1, LANES), ikg_pad)


def _order_key(score):
    bits = pltpu.bitcast(score, I32)
    return jnp.where(bits < 0, bits ^ 0x7FFFFFFF, bits)


def _kth_largest_key(count_ge, top_k, shape):
    tau = jnp.where(count_ge(jnp.zeros(shape, I32)) >= top_k, 0, INT_MIN).astype(I32)

    def body(i, tau):
        cand = tau | jnp.left_shift(jnp.int32(1), 30 - i)
        return jnp.where(count_ge(cand) >= top_k, cand, tau)

    return lax.fori_loop(0, 31, body, tau)


def _idx_head_lhs(iq, h, lane):
    slab = iq[:, (h // 2) * LANES:(h // 2 + 1) * LANES]
    keep = (lane < 64) if h % 2 == 0 else (lane >= 64)
    return jnp.where(keep, slab, jnp.zeros_like(slab))


def _idx_weight(small, h, lane, idx_dim):
    return jnp.sum(jnp.where(lane == idx_dim + h, small, 0.0), axis=-1, keepdims=True)


KEY_CHUNK = 512


def _dsa_prompt_kernel(q_ref, iq_ref, small_t_ref, k_ref, v_t_ref, ikd_ref, o_ref, lhs_sc, key_sc,
                       *, top_k, att_heads, idx_dim):
    n = pl.program_id(1)
    qb = q_ref.shape[0]
    n_chunks = ((n + 1) * qb + KEY_CHUNK - 1) // KEY_CHUNK
    lane = lax.broadcasted_iota(I32, (qb, LANES), 1)
    iq = iq_ref[...]
    for h in range(IDX_HEADS):
        lhs_sc[h // 2, (h % 2) * qb:(h % 2 + 1) * qb, :] = _idx_head_lhs(iq, h, lane)
    small_t = small_t_ref[...]
    weights = [small_t[idx_dim + h:idx_dim + h + 1, :] * (idx_dim ** -0.5) for h in range(IDX_HEADS)]
    qpos = lax.broadcasted_iota(I32, (KEY_CHUNK, qb), 1) + n * qb
    krow = lax.broadcasted_iota(I32, (KEY_CHUNK, qb), 0)

    def chunk_rows(c):
        return pl.ds(pl.multiple_of(c * KEY_CHUNK, KEY_CHUNK), KEY_CHUNK)

    def score_chunk(c, carry):
        ikd = ikd_ref[chunk_rows(c), :]
        score = jnp.zeros((KEY_CHUNK, qb), F32)
        for pair in range(IDX_HEADS // 2):
            d = jnp.maximum(_dot_nt(ikd, lhs_sc[pair]), 0.0)
            score = score + weights[2 * pair] * d[:, :qb] + weights[2 * pair + 1] * d[:, qb:]
        admissible = krow + c * KEY_CHUNK <= qpos
        key_sc[c] = jnp.where(admissible, _order_key(score), INT_MIN)
        return carry

    lax.fori_loop(0, n_chunks, score_chunk, 0)

    def count_ge(cand):
        return lax.fori_loop(
            0, n_chunks,
            lambda c, acc: acc + jnp.sum((key_sc[c] >= cand).astype(I32), axis=0, keepdims=True),
            jnp.zeros((1, qb), I32))

    tau = jnp.maximum(_kth_largest_key(count_ge, top_k, (1, qb)), INT_MIN + 1)
    scale = LANES ** -0.5
    heads = [slice(h * LANES, (h + 1) * LANES) for h in range(att_heads)]

    def attend_chunk(c, carry):
        sel = key_sc[c] >= tau
        out = []
        for sl, (m_old, l_old, acc) in zip(heads, carry):
            s = jnp.where(sel, _dot_nt(k_ref[chunk_rows(c), sl], q_ref[:, sl]) * scale, NEG_BIG)
            m_new = jnp.maximum(m_old, jnp.max(s, axis=0, keepdims=True))
            alpha = jnp.exp(m_old - m_new)
            p = jnp.exp(s - m_new)
            l_new = alpha * l_old + jnp.sum(p, axis=0, keepdims=True)
            out.append((m_new, l_new, alpha * acc + _dot(v_t_ref[c, sl, :], p.astype(BF16))))
        return tuple(out)

    init = tuple((jnp.full((1, qb), NEG_BIG, F32), jnp.zeros((1, qb), F32), jnp.zeros((LANES, qb), F32))
                 for _ in heads)
    for sl, (_, l_fin, acc) in zip(heads, lax.fori_loop(0, n_chunks, attend_chunk, init)):
        o_ref[:, sl] = (acc / l_fin).T.astype(o_ref.dtype)


def dsa_prompt(q_bf, iq_bf, small, k_bf, v_bf, ikd_bf, idx_dim):
    b, t, aw = q_bf.shape
    assert t % KEY_CHUNK == 0 and KEY_CHUNK % QBLOCK == 0
    top_k = min(TOPK_MAX, t // 4)
    n_kc = t // KEY_CHUNK
    small_t = jnp.swapaxes(small, 1, 2)
    v_t = jnp.swapaxes(v_bf.reshape(b, n_kc, KEY_CHUNK, aw), 2, 3)
    qrow = lambda w: pl.BlockSpec((None, QBLOCK, w), lambda bi, n: (bi, n, 0))
    full = lambda w: pl.BlockSpec((None, t, w), lambda bi, n: (bi, 0, 0))
    return pl.pallas_call(
        functools.partial(_dsa_prompt_kernel, top_k=top_k, att_heads=aw // LANES, idx_dim=idx_dim),
        grid=(b, t // QBLOCK),
        in_specs=[qrow(aw), qrow(iq_bf.shape[-1]),
                  pl.BlockSpec((None, LANES, QBLOCK), lambda bi, n: (bi, 0, n)),
                  full(aw),
                  pl.BlockSpec((None, n_kc, aw, KEY_CHUNK), lambda bi, n: (bi, 0, 0, 0)),
                  full(LANES)],
        out_specs=qrow(aw),
        out_shape=jax.ShapeDtypeStruct((b, t, aw), BF16),
        scratch_shapes=[pltpu.VMEM((IDX_HEADS // 2, 2 * QBLOCK, LANES), BF16),
                        pltpu.VMEM((n_kc, KEY_CHUNK, QBLOCK), I32)],
        compiler_params=_cparams(("arbitrary", "arbitrary")),
        name="dsa_prompt",
    )(q_bf, iq_bf, small_t, k_bf, v_t, ikd_bf)


def _idx_lhs_all_heads(iq, lane):
    iq = iq.astype(F32)
    return jnp.concatenate([_idx_head_lhs(iq, h, lane) for h in range(IDX_HEADS)], axis=0).astype(BF16)


def _idx_scores(lhs, small, keys_dup, lane, idx_dim):
    d = _dot_nt(lhs, keys_dup) * (idx_dim ** -0.5)
    score = jnp.zeros((SAMPLE_T_PAD, keys_dup.shape[0]), F32)
    for h in range(IDX_HEADS):
        dh = d[h * SAMPLE_T_PAD:(h + 1) * SAMPLE_T_PAD, :]
        score = score + _idx_weight(small, h, lane, idx_dim) * jnp.maximum(dh, 0.0)
    return score


def _sample_idx_kernel(pt_ref, iq_ref, small_ref, *rest, idx_dim):
    page_refs, o_ref = rest[:-1], rest[-1]
    lane = lax.broadcasted_iota(I32, (SAMPLE_T_PAD, LANES), 1)
    keys = jnp.concatenate([r[...] for r in page_refs], axis=0)
    keys_dup = jnp.concatenate([keys, keys], axis=-1).astype(BF16)
    lhs = _idx_lhs_all_heads(iq_ref[...], lane)
    o_ref[...] = _idx_scores(lhs, small_ref[...], keys_dup, lane, idx_dim)


def sample_idx_scores(page_table, iq_bf, small, cache_idx_k, layer):
    b, n_pages = page_table.shape
    page, idx_dim = cache_idx_k.shape[-2:]
    g = IDX_PAGES_PER_STEP if n_pages % IDX_PAGES_PER_STEP == 0 else 1

    def page_spec(j):
        return pl.BlockSpec((None, None, page, idx_dim),
                            lambda bi, s, pt: (layer, pt[bi * n_pages + s * g + j], 0, 0))

    row = lambda w: pl.BlockSpec((None, SAMPLE_T_PAD, w), lambda bi, s, pt: (bi, 0, 0))
    grid_spec = pltpu.PrefetchScalarGridSpec(
        num_scalar_prefetch=1,
        grid=(b, n_pages // g),
        in_specs=[row(iq_bf.shape[-1]), row(LANES)] + [page_spec(j) for j in range(g)],
        out_specs=pl.BlockSpec((None, SAMPLE_T_PAD, g * page), lambda bi, s, pt: (bi, 0, s)),
    )
    return pl.pallas_call(
        functools.partial(_sample_idx_kernel, idx_dim=idx_dim),
        grid_spec=grid_spec,
        out_shape=jax.ShapeDtypeStruct((b, SAMPLE_T_PAD, n_pages * page), F32),
        compiler_params=_cparams(("arbitrary", "arbitrary")),
        name="sample_idx",
    )(page_table.reshape(-1), iq_bf, small, *([cache_idx_k] * g))


def _sample_attn_kernel(pt_ref, sc_all_ref, sc_ref, q_ref, iq_ref, small_ref, ikn_ref, kn_ref, vn_ref,
                        *rest, top_k, n_new, att_heads, idx_dim, n_groups):
    g = (len(rest) - 7) // 2
    k_pages, v_pages = rest[:g], rest[g:2 * g]
    o_ref, qbd_ref, tau_ref, keyn_ref, m_ref, l_ref, acc_ref = rest[2 * g:]
    s_idx = pl.program_id(1)
    rows = att_heads * SAMPLE_T_PAD
    aw = att_heads * LANES
    scale = LANES ** -0.5

    @pl.when(s_idx == 0)
    def _():
        lane = lax.broadcasted_iota(I32, (SAMPLE_T_PAD, LANES), 1)
        tok = lax.broadcasted_iota(I32, (SAMPLE_T_PAD, LANES), 0)
        lhs = _idx_lhs_all_heads(iq_ref[...], lane)
        sc_new = _idx_scores(lhs, small_ref[...], ikn_ref[...], lane, idx_dim)
        new_ok = (lane <= tok) & (lane < n_new)
        key_new = jnp.where(new_ok, _order_key(sc_new), INT_MIN)
        key_past = _order_key(sc_all_ref[...])

        def count_ge(cand):
            return (jnp.sum((key_past >= cand).astype(I32), axis=-1, keepdims=True)
                    + jnp.sum((key_new >= cand).astype(I32), axis=-1, keepdims=True))

        tau = _kth_largest_key(count_ge, top_k, (SAMPLE_T_PAD, 1))
        tau_ref[...] = jnp.broadcast_to(tau, (SAMPLE_T_PAD, LANES))
        keyn_ref[...] = key_new
        q_rep = jnp.concatenate([q_ref[...].astype(F32)] * att_heads, axis=0)
        r_head = lax.broadcasted_iota(I32, (rows, aw), 0) // SAMPLE_T_PAD
        c_head = lax.broadcasted_iota(I32, (rows, aw), 1) // LANES
        qbd_ref[...] = jnp.where(r_head == c_head, q_rep, 0.0).astype(BF16)
        m_ref[...] = jnp.full(m_ref.shape, NEG_BIG, F32)
        l_ref[...] = jnp.zeros(l_ref.shape, F32)
        acc_ref[...] = jnp.zeros(acc_ref.shape, F32)

    tau = tau_ref[:, 0:1]

    def attend(sel8, key_of_head, val_of_head):
        sel = jnp.concatenate([sel8.astype(F32)] * att_heads, axis=0) > 0.5
        s = _dot_nt(qbd_ref[:, 0:LANES], key_of_head(0))
        for h in range(1, att_heads):
            s = s + _dot_nt(qbd_ref[:, h * LANES:(h + 1) * LANES], key_of_head(h))
        s = jnp.where(sel, s * scale, NEG_BIG)
        m_old = m_ref[:, 0:1]
        m_new = jnp.maximum(m_old, jnp.max(s, axis=-1, keepdims=True))
        alpha = jnp.exp(m_old - m_new)
        p = jnp.where(sel, jnp.exp(s - m_new), 0.0)
        l_ref[...] = jnp.broadcast_to(alpha * l_ref[:, 0:1] + jnp.sum(p, axis=-1, keepdims=True), l_ref.shape)
        p = p.astype(BF16)
        for h in range(att_heads):
            cs = slice(h * LANES, (h + 1) * LANES)
            acc_ref[:, cs] = alpha * acc_ref[:, cs] + _dot(p, val_of_head(h))
        m_ref[...] = jnp.broadcast_to(m_new, m_ref.shape)

    def paged(pages):
        return lambda h: jnp.concatenate([r[:, h, :] for r in pages], axis=0).astype(BF16)

    attend(_order_key(sc_ref[...]) >= tau, paged(k_pages), paged(v_pages))

    @pl.when(s_idx == n_groups - 1)
    def _():
        attend(keyn_ref[...] >= tau,
               lambda h: kn_ref[:, h * LANES:(h + 1) * LANES], lambda h: vn_ref[:, h * LANES:(h + 1) * LANES])
        inv_l = 1.0 / l_ref[:, 0:1]
        for h in range(att_heads):
            rs = slice(h * SAMPLE_T_PAD, (h + 1) * SAMPLE_T_PAD)
            cs = slice(h * LANES, (h + 1) * LANES)
            o_ref[:, cs] = (acc_ref[rs, cs] * inv_l[rs, :]).astype(o_ref.dtype)


def sample_attention(page_table, scores, q_bf, iq_bf, small, ikn_pad, kn_pad, vn_pad,
                     cache_k, cache_v, layer, n_new, idx_dim):
    b, n_pages = page_table.shape
    page = cache_k.shape[2]
    aw = q_bf.shape[-1]
    att_heads = aw // LANES
    past = n_pages * page
    top_k = min(TOPK_MAX, (past + n_new) // 4)
    g = PAGES_PER_STEP if n_pages % PAGES_PER_STEP == 0 else 1
    n_groups = n_pages // g
    rows = att_heads * SAMPLE_T_PAD

    def page_spec(j):
        return pl.BlockSpec((None, None, page, att_heads, LANES),
                            lambda bi, s, pt: (layer, pt[bi * n_pages + s * g + j], 0, 0, 0))

    def per_seq(r, w):
        return pl.BlockSpec((None, r, w), lambda bi, s, pt: (bi, 0, 0))

    grid_spec = pltpu.PrefetchScalarGridSpec(
        num_scalar_prefetch=1,
        grid=(b, n_groups),
        in_specs=[per_seq(SAMPLE_T_PAD, past),
                  pl.BlockSpec((None, SAMPLE_T_PAD, g * page), lambda bi, s, pt: (bi, 0, s)),
                  per_seq(SAMPLE_T_PAD, aw), per_seq(SAMPLE_T_PAD, iq_bf.shape[-1]),
                  per_seq(SAMPLE_T_PAD, LANES),
                  per_seq(LANES, LANES), per_seq(LANES, aw), per_seq(LANES, aw)]
                 + [page_spec(j) for j in range(g)] * 2,
        out_specs=per_seq(SAMPLE_T_PAD, aw),
        scratch_shapes=[
            pltpu.VMEM((rows, aw), BF16),
            pltpu.VMEM((SAMPLE_T_PAD, LANES), I32),
            pltpu.VMEM((SAMPLE_T_PAD, LANES), I32),
            pltpu.VMEM((rows, LANES), F32),
            pltpu.VMEM((rows, LANES), F32),
            pltpu.VMEM((rows, aw), F32),
        ],
    )
    return pl.pallas_call(
        functools.partial(_sample_attn_kernel, top_k=top_k, n_new=n_new, att_heads=att_heads,
                          idx_dim=idx_dim, n_groups=n_groups),
        grid_spec=grid_spec,
        out_shape=jax.ShapeDtypeStruct((b, SAMPLE_T_PAD, aw), BF16),
        compiler_params=_cparams(("arbitrary", "arbitrary")),
        name="sample_attn",
    )(page_table.reshape(-1), scores, scores, q_bf, iq_bf, small, ikn_pad, kn_pad, vn_pad,
      *([cache_k] * g), *([cache_v] * g))


HIST_ROWS = 32
CONV_ROWS = 32


def _conv_kernel(*refs, width, multi_tile):
    if multi_tile:
        u_ref, prev_ref, hist_ref, cw_ref, cb_ref, g_ref, b_ref, o_ref, buf = refs
    else:
        u_ref, hist_ref, cw_ref, cb_ref, g_ref, b_ref, o_ref, buf = refs
    tt = u_ref.shape[0]
    buf[HIST_ROWS:HIST_ROWS + tt, :] = u_ref[...]
    if multi_tile:
        buf[0:HIST_ROWS, :] = jnp.where(pl.program_id(1) == 0, hist_ref[...], prev_ref[...])
    else:
        buf[0:HIST_ROWS, :] = hist_ref[...]
    first = HIST_ROWS - (width - 1)
    rows = min(CONV_ROWS, tt)
    cb, g, b = cb_ref[...], g_ref[...], b_ref[...]
    for r0 in range(0, tt, rows):
        acc = jnp.zeros((rows, u_ref.shape[1]), F32)
        for j in range(width):
            acc = acc + buf[r0 + first + j:r0 + first + j + rows, :] * cw_ref[j:j + 1, :]
        y = acc + cb
        yc = y - jnp.mean(y, axis=-1, keepdims=True)
        var = jnp.mean(yc * yc, axis=-1, keepdims=True)
        y = yc * lax.rsqrt(var + LN_EPS) * g + b
        o_ref[r0:r0 + rows, :] = _silu(y).astype(o_ref.dtype)


def conv_branch(u, hist, cw, cb, ln_g, ln_b):
    b, t, c = u.shape
    width = cw.shape[0]
    tt = _row_tile(t, 256)
    multi_tile = t > tt
    row = pl.BlockSpec((None, tt, c), lambda bi, i: (bi, i, 0))
    vec = pl.BlockSpec((1, c), lambda bi, i: (0, 0))
    specs, args = [row], [u]
    if multi_tile:
        per = tt // HIST_ROWS
        specs.append(pl.BlockSpec((None, HIST_ROWS, c), lambda bi, i: (bi, jnp.maximum(i * per - 1, 0), 0)))
        args.append(u)
    specs += [pl.BlockSpec((None, HIST_ROWS, c), lambda bi, i: (bi, 0, 0)),
              pl.BlockSpec((width, c), lambda bi, i: (0, 0)), vec, vec, vec]
    args += [hist, cw, cb.reshape(1, c), ln_g.reshape(1, c), ln_b.reshape(1, c)]
    return pl.pallas_call(
        functools.partial(_conv_kernel, width=width, multi_tile=multi_tile),
        grid=(b, t // tt),
        in_specs=specs,
        out_specs=row,
        out_shape=jax.ShapeDtypeStruct((b, t, c), BF16),
        scratch_shapes=[pltpu.VMEM((HIST_ROWS + tt, c), F32)],
        compiler_params=_cparams(("arbitrary", "arbitrary")),
        name="conv_branch",
    )(*args)


def _rw_mix_kernel(h_ref, prev_ref, shift_ref, mu_ref, o_ref, buf):
    tt = h_ref.shape[0]
    h = h_ref[...]
    buf[SUBLANES:SUBLANES + tt, :] = h
    buf[SUBLANES - 1:SUBLANES, :] = jnp.where(pl.program_id(1) == 0, shift_ref[...],
                                              prev_ref[SUBLANES - 1:SUBLANES, :])
    xx = buf[SUBLANES - 1:SUBLANES - 1 + tt, :] - h
    for j in range(o_ref.shape[0]):
        o_ref[j] = (h + xx * mu_ref[j:j + 1, :]).astype(o_ref.dtype)


def rw_mix(h, shift_prev, mu):
    b, t, d = h.shape
    n_mix = mu.shape[0]
    tt = _row_tile(t, 256)
    per = tt // SUBLANES
    return pl.pallas_call(
        _rw_mix_kernel,
        grid=(b, t // tt),
        in_specs=[
            pl.BlockSpec((None, tt, d), lambda bi, i: (bi, i, 0)),
            pl.BlockSpec((None, SUBLANES, d), lambda bi, i: (bi, jnp.maximum(i * per - 1, 0), 0)),
            pl.BlockSpec((None, 1, d), lambda bi, i: (bi, 0, 0)),
            pl.BlockSpec((n_mix, d), lambda bi, i: (0, 0)),
        ],
        out_specs=pl.BlockSpec((n_mix, None, tt, d), lambda bi, i: (0, bi, i, 0)),
        out_shape=jax.ShapeDtypeStruct((n_mix, b, t, d), BF16),
        scratch_shapes=[pltpu.VMEM((SUBLANES + tt, d), F32)],
        compiler_params=_cparams(("arbitrary", "arbitrary")),
        name="rw_mix",
    )(h, h, shift_prev.reshape(b, 1, d), mu)


RW_HEAD = 64


def _seg_sum(x, ones_blk):
    outs = []
    for s in range(x.shape[1] // LANES):
        xs = x[:, s * LANES:(s + 1) * LANES]
        hi = xs.astype(BF16)
        lo = (xs - hi.astype(F32)).astype(BF16)
        outs.append(_dot(hi, ones_blk) + _dot(lo, ones_blk))
    return jnp.concatenate(outs, axis=-1)


def _wkv_kernel(r_ref, k_ref, v_ref, wl_ref, al_ref, g_ref,
                w0_ref, a0_ref, kk_ref, ka_ref, rk_ref, lg_ref, lb_ref, s0_ref,
                o_ref, s_ref, dec_sc, kk_sc, b_sc, kh_sc, y_sc, *, n_steps):
    i = pl.program_id(1)
    tc, d = r_ref.shape
    rb = lax.broadcasted_iota(I32, (LANES, LANES), 0) // RW_HEAD
    cb = lax.broadcasted_iota(I32, (LANES, LANES), 1) // RW_HEAD
    ones_blk = jnp.where(rb == cb, 1.0, 0.0).astype(BF16)
    eye2 = (lax.broadcasted_iota(I32, (RW_HEAD, LANES), 0)
            == lax.broadcasted_iota(I32, (RW_HEAD, LANES), 1) % RW_HEAD)

    @pl.when(i == 0)
    def _():
        s_ref[...] = s0_ref[...]

    k = k_ref[...]
    z = -(w0_ref[...] + wl_ref[...])
    softplus = jnp.maximum(z, 0.0) + jnp.log(1.0 + jnp.exp(-jnp.abs(z)))
    dec_sc[...] = jnp.exp(-jnp.exp(-softplus - 0.5))
    a = jax.nn.sigmoid(a0_ref[...] + al_ref[...])
    kk = k * kk_ref[...]
    kk = kk * lax.rsqrt(jnp.maximum(_seg_sum(kk * kk, ones_blk), 1e-24))
    kk_sc[...] = kk
    b_sc[...] = kk * a
    kh_sc[...] = k * (1.0 + (a - 1.0) * ka_ref[...])
    assert -(-n_steps // SUBLANES) * SUBLANES == tc, "the recurrence blocks must cover the whole time tile"

    steps_per_block = min(SUBLANES, n_steps)

    slabs = [slice(s * LANES, (s + 1) * LANES) for s in range(d // LANES)]
    head_sel = (lax.broadcasted_iota(I32, (SUBLANES, LANES), 0)
                == lax.broadcasted_iota(I32, (SUBLANES, LANES), 1) // RW_HEAD).astype(BF16)

    def block(blk, carry):
        rows = pl.ds(pl.multiple_of(blk * SUBLANES, SUBLANES), SUBLANES)
        st = [s_ref[:, sl] for sl in slabs]
        y_rows = [[] for _ in slabs]
        for j in range(steps_per_block):
            sa, vb = [], []
            for s, sl in enumerate(slabs):
                lhs = jnp.concatenate([st[s] * kk_sc[rows, sl][j:j + 1],
                                       jnp.where(eye2, v_ref[rows, sl][j:j + 1], 0.0)], axis=0)
                both = _dot(lhs.astype(BF16), ones_blk)
                sa.append(both[:RW_HEAD])
                vb.append(both[RW_HEAD:])
            for s, sl in enumerate(slabs):
                st[s] = (st[s] * dec_sc[rows, sl][j:j + 1] - sa[s] * b_sc[rows, sl][j:j + 1]
                         + vb[s] * kh_sc[rows, sl][j:j + 1])
            for s, sl in enumerate(slabs):
                yh = _dot_nt(head_sel, (st[s] * r_ref[rows, sl][j:j + 1]).astype(BF16))
                y_rows[s].append(jnp.concatenate([yh[0:1], yh[1:2]], axis=-1))
        for s, sl in enumerate(slabs):
            s_ref[:, sl] = st[s]
            pad = [jnp.zeros((1, LANES), F32)] * (SUBLANES - steps_per_block)
            y_sc[rows, sl] = jnp.concatenate(y_rows[s] + pad, axis=0)
        return carry

    lax.fori_loop(0, -(-n_steps // SUBLANES), block, 0)

    y = y_sc[...]
    inv_n = 1.0 / RW_HEAD
    yc = y - _seg_sum(y, ones_blk) * inv_n
    var = _seg_sum(yc * yc, ones_blk) * inv_n
    y = yc * lax.rsqrt(var + LNX_EPS) * lg_ref[...] + lb_ref[...]
    r = r_ref[...]
    y = y + _seg_sum(r * kh_sc[...] * rk_ref[...], ones_blk) * v_ref[...]
    o_ref[...] = (y * g_ref[...]).astype(o_ref.dtype)


def wkv(r, k, v, wl, al, g, w0, a0, k_k, k_a, r_k, lnx_g, lnx_b, s0, n_steps):
    b, t, d = r.shape
    tc = _row_tile(t, 128)
    steps = tc if t > tc else n_steps
    row = pl.BlockSpec((None, tc, d), lambda bi, i: (bi, i, 0))
    vec = pl.BlockSpec((1, d), lambda bi, i: (0, 0))
    st = pl.BlockSpec((None, RW_HEAD, d), lambda bi, i: (bi, 0, 0))
    vecs = [x.reshape(1, d) for x in (w0, a0, k_k, k_a, r_k, lnx_g, lnx_b)]
    return pl.pallas_call(
        functools.partial(_wkv_kernel, n_steps=steps),
        grid=(b, t // tc),
        in_specs=[row] * 6 + [vec] * 7 + [st],
        out_specs=[row, st],
        out_shape=[jax.ShapeDtypeStruct((b, t, d), BF16), jax.ShapeDtypeStruct((b, RW_HEAD, d), F32)],
        scratch_shapes=[pltpu.VMEM((tc, d), F32)] * 5,
        compiler_params=_cparams(("arbitrary", "arbitrary")),
        name="wkv",
    )(r, k, v, wl, al, g, *vecs, s0)


WKV_CHUNK = 64
WKV_TILE = 1024
WKV_TILE_MIN = 512


def _split3(x):
    h1 = x.astype(BF16)
    r1 = x - h1.astype(F32)
    h2 = r1.astype(BF16)
    return h1, h2, (r1 - h2.astype(F32)).astype(BF16)


def _wkv_prep_kernel(k_ref, wl_ref, al_ref, w0_ref, a0_ref, kkp_ref, ka_ref, lw_ref, kk_ref, b_ref, kh_ref):
    rb = lax.broadcasted_iota(I32, (LANES, LANES), 0) // RW_HEAD
    cb = lax.broadcasted_iota(I32, (LANES, LANES), 1) // RW_HEAD
    ones_blk = jnp.where(rb == cb, 1.0, 0.0).astype(BF16)
    k = k_ref[...]
    z = -(w0_ref[...] + wl_ref[...])
    softplus = jnp.maximum(z, 0.0) + jnp.log(1.0 + jnp.exp(-jnp.abs(z)))
    lw_ref[...] = -jnp.exp(-softplus - 0.5)
    a = jax.nn.sigmoid(a0_ref[...] + al_ref[...])
    kk = k * kkp_ref[...]
    kk = kk * lax.rsqrt(jnp.maximum(_seg_sum(kk * kk, ones_blk), 1e-24))
    kk_ref[...] = kk
    b_ref[...] = kk * a
    kh_ref[...] = k * (1.0 + (a - 1.0) * ka_ref[...])


def wkv_prep(k, wl, al, w0, a0, k_k, k_a):
    b, t, d = k.shape
    tt = _row_tile(t, 256)
    row = pl.BlockSpec((None, tt, d), lambda bi, i: (bi, i, 0))
    vec = pl.BlockSpec((1, d), lambda bi, i: (0, 0))
    return pl.pallas_call(
        _wkv_prep_kernel,
        grid=(b, t // tt),
        in_specs=[row] * 3 + [vec] * 4,
        out_specs=[row] * 4,
        out_shape=[jax.ShapeDtypeStruct((b, t, d), F32)] * 4,
        compiler_params=_cparams(("arbitrary", "arbitrary")),
        name="wkv_prep",
    )(k, wl, al, *[x.reshape(1, d) for x in (w0, a0, k_k, k_a)])


def _wkv_chunk_kernel(r_ref, lw_ref, kh_ref, v_ref, kk_ref, b_ref, lwt_ref, kht_ref, bt_ref, h0_ref,
                      y_ref, h_ref, *, n_chunks):
    c_ = WKV_CHUNK

    @pl.when(pl.program_id(2) == 0)
    def _():
        h_ref[...] = h0_ref[...]

    row = lax.broadcasted_iota(I32, (c_, c_), 0)
    col = lax.broadcasted_iota(I32, (c_, c_), 1)
    tri_incl = col <= row
    tri_strict = col < row
    l_incl = jnp.where(tri_incl, 1.0, 0.0).astype(BF16)
    u_incl = jnp.where(row <= col, 1.0, 0.0).astype(BF16)
    lane_head = lax.broadcasted_iota(I32, (1, LANES), 1) // RW_HEAD
    row_head = lax.broadcasted_iota(I32, (LANES, 1), 0) // RW_HEAD
    n_heads = LANES // RW_HEAD
    chunks = range(n_chunks)
    items = [(c, hh) for c in chunks for hh in range(n_heads)]
    ch = []
    for c in chunks:
        rs = slice(c * c_, (c + 1) * c_)
        lw, r, kh, v, kk, b = lw_ref[rs, :], r_ref[rs, :], kh_ref[rs, :], v_ref[rs, :], kk_ref[rs, :], b_ref[rs, :]
        cum = sum(_dot(l_incl, part) for part in _split3(lw))
        e_neg = jnp.exp(-cum)
        bt = b * e_neg
        kt = kh * e_neg
        cumt = sum(_dot(part, u_incl) for part in _split3(lwt_ref[c]))
        dt = jnp.exp(cumt[:, c_ - 1:c_] - cumt)
        ch.append(dict(
            kkt=kk * jnp.exp(cum - lw), rt=r * jnp.exp(cum),
            y1=jnp.concatenate([bt, kt], axis=0).astype(BF16), v_bf=v.astype(BF16),
            g_col=jnp.exp(cumt[:, c_ - 1:c_]),
            bht=bt_ref[c] * dt, kht=kht_ref[c] * dt))
    it = {}
    for c, hh in items:
        m = lane_head == hh
        kkt_h = jnp.where(m, ch[c]["kkt"], 0.0)
        rt_h = jnp.where(m, ch[c]["rt"], 0.0)
        gmat = _dot_nt(jnp.concatenate([kkt_h, rt_h], axis=0).astype(BF16), ch[c]["y1"])
        it[c, hh] = dict(
            kkt_h=kkt_h, rt_h=rt_h,
            a_b=jnp.where(tri_strict, gmat[:c_, :c_], 0.0).astype(BF16),
            a_k=jnp.where(tri_strict, gmat[:c_, c_:], 0.0).astype(BF16),
            m_b=jnp.where(tri_incl, gmat[c_:, :c_], 0.0).astype(BF16),
            m_k=jnp.where(tri_incl, gmat[c_:, c_:], 0.0).astype(BF16))
    for c, hh in items:
        d_ = it[c, hh]
        d_["x"] = jnp.concatenate([d_["kkt_h"], _dot(d_["a_k"], ch[c]["v_bf"])], axis=1)
        d_["pows"] = [d_["a_b"]]
    n_levels = c_.bit_length() - 1
    for _ in range(n_levels - 1):
        for key in items:
            pows = it[key]["pows"]
            pows.append(_dot(pows[-1], pows[-1]).astype(BF16))
    for level in range(n_levels - 1, 0, -1):
        for key in items:
            d_ = it[key]
            d_["x"] = d_["x"] + _dot(d_["pows"][level], d_["x"].astype(BF16))
    for c, hh in items:
        d_ = it[c, hh]
        m = lane_head == hh
        rmask = row_head == hh
        x = d_["x"] - _dot(d_["a_b"], d_["x"].astype(BF16))
        x_bf = x.astype(BF16)
        mbx = _dot(d_["m_b"], x_bf)
        d_["rw"] = d_["rt_h"] - mbx[:, :LANES]
        d_["y0"] = jnp.where(m, _dot(d_["m_k"], ch[c]["v_bf"]) - mbx[:, LANES:], 0.0)
        bx = _dot(jnp.where(rmask, ch[c]["bht"], 0.0).astype(BF16), x_bf)
        d_["pm"] = -bx[:, :LANES]
        d_["qm"] = jnp.where(m, _dot(jnp.where(rmask, ch[c]["kht"], 0.0).astype(BF16), ch[c]["v_bf"])
                             - bx[:, LANES:], 0.0)
    hst = h_ref[...]
    for c in chunks:
        parts = [it[c, hh] for hh in range(n_heads)]
        h_bf = hst.astype(BF16)
        y_ref[c * c_:(c + 1) * c_, :] = (_dot(sum(p_["rw"] for p_ in parts).astype(BF16), h_bf)
                                         + sum(p_["y0"] for p_ in parts))
        hst = (ch[c]["g_col"] * hst + _dot(sum(p_["pm"] for p_ in parts).astype(BF16), h_bf)
               + sum(p_["qm"] for p_ in parts))
    h_ref[...] = hst


def wkv_chunked(r, lw, kh, v, kk, b_, h0):
    bsz, t, d = r.shape
    n_pairs = d // LANES
    tile = WKV_TILE if t % WKV_TILE == 0 else WKV_TILE_MIN
    per_tile = tile // WKV_CHUNK

    def chunk_major_t(x):
        return jnp.swapaxes(x.reshape(bsz, t // WKV_CHUNK, WKV_CHUNK, d), 2, 3)

    row = pl.BlockSpec((None, tile, LANES), lambda bi, p, i: (bi, i, p))
    row_t = pl.BlockSpec((None, per_tile, LANES, WKV_CHUNK), lambda bi, p, i: (bi, i, p, 0))
    st = pl.BlockSpec((None, None, LANES, LANES), lambda bi, p, i: (bi, p, 0, 0))
    return pl.pallas_call(
        functools.partial(_wkv_chunk_kernel, n_chunks=per_tile),
        grid=(bsz, n_pairs, t // tile),
        in_specs=[row] * 6 + [row_t] * 3 + [st],
        out_specs=[row, st],
        out_shape=[jax.ShapeDtypeStruct((bsz, t, d), F32),
                   jax.ShapeDtypeStruct((bsz, n_pairs, LANES, LANES), F32)],
        compiler_params=_cparams(("arbitrary", "arbitrary", "arbitrary")),
        name="wkv_chunked",
    )(r, lw, kh, v, kk, b_, chunk_major_t(lw), chunk_major_t(kh), chunk_major_t(b_), h0)


def _wkv_post_kernel(y_ref, r_ref, kh_ref, v_ref, g_ref, rk_ref, lg_ref, lb_ref, o_ref):
    rb = lax.broadcasted_iota(I32, (LANES, LANES), 0) // RW_HEAD
    cb = lax.broadcasted_iota(I32, (LANES, LANES), 1) // RW_HEAD
    ones_blk = jnp.where(rb == cb, 1.0, 0.0).astype(BF16)
    y = y_ref[...]
    inv_n = 1.0 / RW_HEAD
    yc = y - _seg_sum(y, ones_blk) * inv_n
    var = _seg_sum(yc * yc, ones_blk) * inv_n
    y = yc * lax.rsqrt(var + LNX_EPS) * lg_ref[...] + lb_ref[...]
    y = y + _seg_sum(r_ref[...] * kh_ref[...] * rk_ref[...], ones_blk) * v_ref[...]
    o_ref[...] = (y * g_ref[...]).astype(o_ref.dtype)


def wkv_post(y, r, kh, v, g, r_k, lnx_g, lnx_b):
    b, t, d = y.shape
    tt = _row_tile(t, 256)
    row = pl.BlockSpec((None, tt, d), lambda bi, i: (bi, i, 0))
    vec = pl.BlockSpec((1, d), lambda bi, i: (0, 0))
    return pl.pallas_call(
        _wkv_post_kernel,
        grid=(b, t // tt),
        in_specs=[row] * 5 + [vec] * 3,
        out_specs=row,
        out_shape=jax.ShapeDtypeStruct((b, t, d), BF16),
        compiler_params=_cparams(("arbitrary", "arbitrary")),
        name="wkv_post",
    )(y, r, kh, v, g, *[x.reshape(1, d) for x in (r_k, lnx_g, lnx_b)])


def _state_to_blockdiag(s):
    st = jnp.swapaxes(s, 2, 3)
    st = st.reshape(s.shape[0], s.shape[1] // 2, 2, RW_HEAD, RW_HEAD)
    z = jnp.zeros_like(st[:, :, 0])
    return jnp.concatenate([jnp.concatenate([st[:, :, 0], z], axis=-1),
                            jnp.concatenate([z, st[:, :, 1]], axis=-1)], axis=-2)


def _state_from_blockdiag(hbd):
    b, p = hbd.shape[:2]
    x = hbd.reshape(b, p, 2, RW_HEAD, 2, RW_HEAD)
    diag = jnp.stack([x[:, :, 0, :, 0, :], x[:, :, 1, :, 1, :]], axis=2)
    return jnp.swapaxes(diag.reshape(b, 2 * p, RW_HEAD, RW_HEAD), 2, 3)


def _state_to_kernel_layout(s):
    b, h, n, _ = s.shape
    return jnp.transpose(s, (0, 2, 1, 3)).reshape(b, n, h * n)


def _state_from_kernel_layout(s, heads):
    b, n, _ = s.shape
    return jnp.transpose(s.reshape(b, n, heads, n), (0, 2, 1, 3))


def _run_group(x, mod_all, t_real, positions, sample_ctx, weights):
    (norm_g, ffn_w1, ffn_w3, ffn_w2, att_w_in, att_w_out, q_norm_g, k_norm_g, idx_k_norm_g,
     conv_w, conv_b, conv_ln_g, conv_ln_b, rw_mu, rw_w0, rw_w1, rw_w2, rw_a0, rw_a1, rw_a2, rw_g1, rw_g2,
     rw_k_k, rw_k_a, rw_r_k, rw_wr, rw_wk, rw_wv, rw_wo, rw_lnx_g, rw_lnx_b) = weights
    b, t, d = x.shape
    depth = norm_g.shape[0]
    aw = att_w_out.shape[1] - conv_w.shape[2]
    att_heads = aw // LANES
    cc = conv_w.shape[2]
    conv_width = conv_w.shape[1]
    idx_dim = idx_k_norm_g.shape[1]
    iq_w = IDX_HEADS * idx_dim
    rw_heads = d // RW_HEAD
    tables = _rope_tables(positions, LANES) + _rope_tables(positions, idx_dim)

    outs = dict(k=[], v=[], ik=[], conv=[], shift=[], wkv=[])
    h = resid_norm(x, mod_all[0], g=norm_g[0, 0], shift_idx=0, scale_idx=1, emit_x=False, h_dtype=BF16)
    for l in range(depth):
        i = l // 2
        mod = mod_all[l]
        even = l % 2 == 0
        y = ffn(h, ffn_w1, ffn_w3, ffn_w2, l, 0)
        x, h = resid_norm(x, mod, y=y, gate_idx=2, coef=HALF_STEP, g=norm_g[l, 1], shift_idx=3, scale_idx=4,
                          h_dtype=BF16 if even else F32)
        if even:
            qkvi = matmul(h, att_w_in, lead=i, n_cols=3 * aw + iq_w)
            w_small = jnp.pad(att_w_in[i][:, 3 * aw + iq_w:3 * aw + iq_w + idx_dim + IDX_HEADS],
                              ((0, 0), (0, LANES - idx_dim - IDX_HEADS)))
            small = matmul(h, w_small)
            u = matmul(h, att_w_in[i][:, 3 * aw + iq_w + idx_dim + IDX_HEADS:])
            q_bf, k_f, k_bf, v_bf, iq_bf, small, ikd_bf, glu = even_post(
                qkvi, small, u, tables, q_norm_g[i], k_norm_g[i], idx_k_norm_g[i], att_heads, idx_dim)
            if sample_ctx is None:
                att = dsa_prompt(q_bf, iq_bf, small, k_bf, v_bf, ikd_bf, idx_dim)
                hist = jnp.zeros((b, HIST_ROWS, cc), F32)
                outs["conv"].append(glu[:, t - (conv_width - 1):])
            else:
                pt = sample_ctx["page_table"]
                scores = sample_idx_scores(pt, iq_bf, small, sample_ctx["cache_idx_k"], i)
                pad_rows = lambda a: jnp.pad(a, ((0, 0), (0, LANES - t), (0, 0)))
                att = sample_attention(pt, scores, q_bf, iq_bf, small, pad_rows(ikd_bf), pad_rows(k_bf),
                                       pad_rows(v_bf), sample_ctx["cache_k"], sample_ctx["cache_v"], i,
                                       t_real, idx_dim)
                state = sample_ctx["state_conv"][i]
                hist = jnp.pad(state, ((0, 0), (HIST_ROWS - (conv_width - 1), 0), (0, 0)))
                outs["conv"].append(jnp.concatenate([state, glu[:, :t_real]], axis=1)[:, -(conv_width - 1):])
            conv_y = conv_branch(glu, hist, conv_w[i], conv_b[i], conv_ln_g[i], conv_ln_b[i])
            mixed = matmul(jnp.concatenate([att, conv_y], axis=-1), att_w_out, lead=i)
            outs["k"].append(k_f[:, :t_real].reshape(b, t_real, att_heads, LANES))
            outs["v"].append(qkvi[:, :t_real, 2 * aw:3 * aw].reshape(b, t_real, att_heads, LANES))
            outs["ik"].append(small[:, :t_real, :idx_dim])
        else:
            if sample_ctx is None:
                shift_prev = jnp.zeros((b, d), F32)
                s_init = jnp.zeros((b, rw_heads, RW_HEAD, RW_HEAD), F32)
            else:
                shift_prev = sample_ctx["state_shift"][i]
                s_init = sample_ctx["state_wkv"][i]
            xs = rw_mix(h, shift_prev, rw_mu[i])
            r = matmul(xs[0], rw_wr, lead=i)
            wl = lora(xs[1], rw_w1, rw_w2, i, "tanh")
            k = matmul(xs[2], rw_wk, lead=i)
            v = matmul(xs[3], rw_wv, lead=i)
            al = lora(xs[4], rw_a1, rw_a2, i, "none")
            g = lora(xs[5], rw_g1, rw_g2, i, "sigmoid")
            if t == t_real and t % WKV_TILE_MIN == 0:
                lw, kk, b_, kh = wkv_prep(k, wl, al, rw_w0[i], rw_a0[i], rw_k_k[i], rw_k_a[i])
                y, h_fin = wkv_chunked(r, lw, kh, v, kk, b_, _state_to_blockdiag(s_init))
                yg = wkv_post(y, r, kh, v, g, rw_r_k[i], rw_lnx_g[i], rw_lnx_b[i])
                s_out = _state_from_blockdiag(h_fin)
            else:
                yg, s_fin = wkv(r, k, v, wl, al, g, rw_w0[i], rw_a0[i], rw_k_k[i], rw_k_a[i], rw_r_k[i],
                                rw_lnx_g[i], rw_lnx_b[i], _state_to_kernel_layout(s_init), t_real)
                s_out = _state_from_kernel_layout(s_fin, rw_heads)
            mixed = matmul(yg, rw_wo, lead=i)
            outs["shift"].append(h[:, t_real - 1])
            outs["wkv"].append(s_out)
        x, h = resid_norm(x, mod, y=mixed, gate_idx=5, coef=1.0, g=norm_g[l, 2], shift_idx=6, scale_idx=7,
                          h_dtype=BF16)
        y = ffn(h, ffn_w1, ffn_w3, ffn_w2, l, 1)
        if l + 1 < depth:
            x, h = resid_norm(x, mod, y=y, gate_idx=8, coef=HALF_STEP, g=norm_g[l + 1, 0],
                              shift_idx=0, scale_idx=1, h_dtype=BF16, mod_norm=mod_all[l + 1])
        else:
            x = resid_norm(x, mod, y=y, gate_idx=8, coef=HALF_STEP)
    return x[:, :t_real], outs


def kernel(x_prompt, x_sample, cache_k, cache_v, cache_idx_k, state_conv, state_shift, state_wkv, page_table, c_prompt, c_sample, norm_g, ada_w, ada_b, ffn_w1, ffn_w3, ffn_w2, att_w_in, att_w_out, q_norm_g, k_norm_g, idx_k_norm_g, conv_w, conv_b, conv_ln_g, conv_ln_b, rw_mu, rw_w0, rw_w1, rw_w2, rw_a0, rw_a1, rw_a2, rw_g1, rw_g2, rw_k_k, rw_k_a, rw_r_k, rw_wr, rw_wk, rw_wv, rw_wo, rw_lnx_g, rw_lnx_b):
    bp, tp, d = x_prompt.shape
    bs, ts, _ = x_sample.shape
    depth = norm_g.shape[0]
    past = page_table.shape[1] * cache_k.shape[2]

    n_c = bp + bs
    c_rows = -(-n_c // 16) * 16
    c_all = jnp.pad(jnp.concatenate([c_prompt, c_sample], axis=0), ((0, c_rows - n_c), (0, 0)))
    mod = adaln_all(c_all, ada_w, ada_b).reshape(depth, c_rows, N_MOD, 1, d)
    mod_p, mod_s = mod[:, :bp], mod[:, bp:n_c]

    weights = (norm_g, ffn_w1, ffn_w3, ffn_w2, att_w_in, att_w_out, q_norm_g, k_norm_g, idx_k_norm_g,
               conv_w, conv_b, conv_ln_g, conv_ln_b, rw_mu, rw_w0, rw_w1, rw_w2, rw_a0, rw_a1, rw_a2,
               rw_g1, rw_g2, rw_k_k, rw_k_a, rw_r_k.reshape(rw_r_k.shape[0], -1), rw_wr, rw_wk, rw_wv, rw_wo,
               rw_lnx_g, rw_lnx_b)

    yp, op = _run_group(x_prompt, mod_p, tp, np.arange(tp), None, weights)
    xs_pad = jnp.pad(x_sample, ((0, 0), (0, SAMPLE_T_PAD - ts), (0, 0)))
    sample_ctx = dict(page_table=page_table, cache_k=cache_k, cache_v=cache_v, cache_idx_k=cache_idx_k,
                      state_conv=state_conv, state_shift=state_shift, state_wkv=state_wkv)
    ys, os_ = _run_group(xs_pad, mod_s, ts, past + np.arange(SAMPLE_T_PAD), sample_ctx, weights)

    st = lambda xs: jnp.stack(xs)
    return (yp, ys,
            st(op["k"]), st(op["v"]), st(op["ik"]), st(op["conv"]), st(op["shift"]), st(op["wkv"]),
            st(os_["k"]), st(os_["v"]), st(os_["ik"]), st(os_["conv"]), st(os_["shift"]), st(os_["wkv"]))
```

```python
import functools
import math

import numpy as np
import jax
import jax.numpy as jnp
from jax import lax
from jax.experimental import pallas as pl
from jax.experimental.pallas import tpu as pltpu

F32 = jnp.float32
BF16 = jnp.bfloat16
I32 = jnp.int32

ROPE_THETA = 10000.0
NORM_EPS = 1e-6
LN_EPS = 1e-5
LNX_EPS = 64e-5
N_MOD = 9
HALF_STEP = 0.5
TOPK_MAX = 256
QBLOCK = 128
IDX_HEADS = 16

LANES = 128
SUBLANES = 8
VMEM_LIMIT_MB = 56
NEG_BIG = -1e30
INT_MIN = -(2 ** 31)

SAMPLE_T_PAD = 8
PAGES_PER_STEP = 4
IDX_PAGES_PER_STEP = 8


def _cparams(sem, vmem_mb=VMEM_LIMIT_MB):
    return pltpu.CompilerParams(dimension_semantics=sem, vmem_limit_bytes=vmem_mb * 1024 * 1024)


def _silu(x):
    return x * jax.nn.sigmoid(x)


def _dot(a, b):
    return jnp.dot(a, b, preferred_element_type=F32)


def _dot_nt(a, b):
    return lax.dot_general(a, b, (((1,), (1,)), ((), ())), preferred_element_type=F32)


def _row_tile(t, target):
    return t if t <= target else target


def _adaln_kernel(c_ref, w_ref, b_ref, o_ref):
    sc = _silu(c_ref[...]).astype(BF16)
    o_ref[...] = _dot(sc, w_ref[...].astype(BF16)) + b_ref[...]


def adaln_all(c_all, ada_w, ada_b):
    depth, d, n = ada_w.shape
    rows = c_all.shape[0]
    tn = 1024
    return pl.pallas_call(
        _adaln_kernel,
        grid=(depth, n // tn),
        in_specs=[
            pl.BlockSpec((rows, d), lambda l, j: (0, 0)),
            pl.BlockSpec((None, d, tn), lambda l, j: (l, 0, j)),
            pl.BlockSpec((None, 1, tn), lambda l, j: (l, 0, j)),
        ],
        out_specs=pl.BlockSpec((None, rows, tn), lambda l, j: (l, 0, j)),
        out_shape=jax.ShapeDtypeStruct((depth, rows, n), F32),
        compiler_params=_cparams(("arbitrary", "arbitrary")),
        name="adaln",
    )(c_all, ada_w, ada_b.reshape(depth, 1, n))


def _resid_norm_kernel(*refs, has_y, coef, emit_x, emit_h):
    refs = list(refs)
    x = refs.pop(0)[...]
    if has_y:
        y = refs.pop(0)[...]
        gate = refs.pop(0)[...]
        x = x + (coef * gate) * y
    if emit_h:
        g = refs.pop(0)[...]
        shift = refs.pop(0)[...]
        scale = refs.pop(0)[...]
    if emit_x:
        refs.pop(0)[...] = x
    if emit_h:
        h_ref = refs.pop(0)
        ms = jnp.mean(x * x, axis=-1, keepdims=True)
        h = x * lax.rsqrt(ms + NORM_EPS) * g
        h_ref[...] = (h * (1.0 + scale) + shift).astype(h_ref.dtype)


def resid_norm(x, mod, *, y=None, gate_idx=None, coef=1.0, g=None, shift_idx=None, scale_idx=None,
               emit_x=True, h_dtype=None, mod_norm=None):
    mod_norm = mod if mod_norm is None else mod_norm
    b, t, d = x.shape
    tt = _row_tile(t, 256)
    has_y = y is not None
    emit_h = h_dtype is not None
    row = pl.BlockSpec((None, tt, d), lambda bi, i: (bi, i, 0))

    def mod_spec(idx):
        return pl.BlockSpec((None, None, 1, d), lambda bi, i: (bi, idx, 0, 0))

    args, specs = [x], [row]
    if has_y:
        args += [y, mod]
        specs += [row, mod_spec(gate_idx)]
    if emit_h:
        args += [g.reshape(1, d), mod_norm, mod_norm]
        specs += [pl.BlockSpec((1, d), lambda bi, i: (0, 0)), mod_spec(shift_idx), mod_spec(scale_idx)]
    out_shape, out_specs = [], []
    if emit_x:
        out_shape.append(jax.ShapeDtypeStruct((b, t, d), F32))
        out_specs.append(row)
    if emit_h:
        out_shape.append(jax.ShapeDtypeStruct((b, t, d), h_dtype))
        out_specs.append(row)
    outs = pl.pallas_call(
        functools.partial(_resid_norm_kernel, has_y=has_y, coef=coef, emit_x=emit_x, emit_h=emit_h),
        grid=(b, t // tt),
        in_specs=specs,
        out_specs=out_specs,
        out_shape=out_shape,
        compiler_params=_cparams(("arbitrary", "arbitrary")),
        name="resid_norm",
    )(*args)
    return outs if len(outs) > 1 else outs[0]


def _ffn_kernel(h_ref, w1_ref, w3_ref, w2_ref, o_ref):
    f = pl.program_id(1)
    h = h_ref[...]
    a = _dot(h, w1_ref[...].astype(BF16))
    b = _dot(h, w3_ref[...].astype(BF16))
    z = (_silu(a) * b).astype(BF16)

    @pl.when(f == 0)
    def _():
        o_ref[...] = jnp.zeros(o_ref.shape, o_ref.dtype)

    o_ref[...] += _dot(z, w2_ref[...].astype(BF16))


def ffn(h, w1, w3, w2, layer, slot):
    b, t, d = h.shape
    m = b * t
    d_ff = w1.shape[-1]
    tm = 1024 if m >= 1024 else m
    tf = 256 if m >= 1024 else 512
    out = pl.pallas_call(
        _ffn_kernel,
        grid=(m // tm, d_ff // tf),
        in_specs=[
            pl.BlockSpec((tm, d), lambda i, f: (i, 0)),
            pl.BlockSpec((None, None, d, tf), lambda i, f: (layer, slot, 0, f)),
            pl.BlockSpec((None, None, d, tf), lambda i, f: (layer, slot, 0, f)),
            pl.BlockSpec((None, None, tf, d), lambda i, f: (layer, slot, f, 0)),
        ],
        out_specs=pl.BlockSpec((tm, d), lambda i, f: (i, 0)),
        out_shape=jax.ShapeDtypeStruct((m, d), F32),
        compiler_params=_cparams(("arbitrary", "arbitrary")),
        name="ffn",
    )(h.reshape(m, d), w1, w3, w2)
    return out.reshape(b, t, d)


def _mm_kernel(x_ref, w_ref, o_ref):
    o_ref[...] = _dot(x_ref[...], w_ref[...].astype(BF16)).astype(o_ref.dtype)


def matmul(x, w, lead=None, n_cols=None, out_dtype=F32):
    b, t, k = x.shape
    m = b * t
    n = w.shape[-1] if n_cols is None else n_cols
    tm = 1024 if m >= 1024 else m
    tn = 512 if n % 512 == 0 else n
    if lead is None:
        w_spec = pl.BlockSpec((k, tn), lambda i, j: (0, j))
    else:
        w_spec = pl.BlockSpec((None, k, tn), lambda i, j: (lead, 0, j))
    out = pl.pallas_call(
        _mm_kernel,
        grid=(m // tm, n // tn),
        in_specs=[pl.BlockSpec((tm, k), lambda i, j: (i, 0)), w_spec],
        out_specs=pl.BlockSpec((tm, tn), lambda i, j: (i, j)),
        out_shape=jax.ShapeDtypeStruct((m, n), out_dtype),
        compiler_params=_cparams(("arbitrary", "arbitrary")),
        name="matmul",
    )(x.reshape(m, k), w)
    return out.reshape(b, t, n)


def _lora_kernel(x_ref, a_ref, b_ref, o_ref, *, act):
    h = _dot(x_ref[...], a_ref[...].astype(BF16))
    if act == "tanh":
        h = jnp.tanh(h)
    elif act == "sigmoid":
        h = jax.nn.sigmoid(h)
    o_ref[...] = _dot(h.astype(BF16), b_ref[...].astype(BF16))


def lora(x, a, bmat, layer, act):
    b, t, d = x.shape
    m = b * t
    r = a.shape[-1]
    n = bmat.shape[-1]
    tm = 512 if m >= 512 else m
    out = pl.pallas_call(
        functools.partial(_lora_kernel, act=act),
        grid=(m // tm,),
        in_specs=[
            pl.BlockSpec((tm, d), lambda i: (i, 0)),
            pl.BlockSpec((None, d, r), lambda i: (layer, 0, 0)),
            pl.BlockSpec((None, r, n), lambda i: (layer, 0, 0)),
        ],
        out_specs=pl.BlockSpec((tm, n), lambda i: (i, 0)),
        out_shape=jax.ShapeDtypeStruct((m, n), F32),
        compiler_params=_cparams(("arbitrary",)),
        name="lora",
    )(x.reshape(m, d), a, bmat)
    return out.reshape(b, t, n)


def _rope_tables(positions, head_dim):
    half = head_dim // 2
    inv = ROPE_THETA ** (-np.arange(half, dtype=np.float64) / half)
    ang = np.asarray(positions, np.float64)[:, None] * inv[None, :]
    cos = np.concatenate([np.cos(ang), np.cos(ang)], axis=-1)
    sin = np.concatenate([-np.sin(ang), np.sin(ang)], axis=-1)
    reps = LANES // head_dim
    return (jnp.asarray(np.tile(cos, (1, reps)), F32), jnp.asarray(np.tile(sin, (1, reps)), F32))


def _rope128(x, cos, sin):
    return x * cos + pltpu.roll(x, 64, 1) * sin


def _rope64(x, cos, sin, lane):
    first_half = (lane % 64) < 32
    partner = jnp.where(first_half, pltpu.roll(x, 96, 1), pltpu.roll(x, 32, 1))
    return x * cos + partner * sin


def _even_post_kernel(q_ref, k_ref, v_ref, iq_ref, small_ref, u_ref,
                      c128_ref, s128_ref, c64_ref, s64_ref, qg_ref, kg_ref, ikg_ref,
                      qo_ref, kf_ref, kb_ref, vb_ref, iqo_ref, smallo_ref, ikd_ref, uo_ref,
                      *, att_heads, idx_dim, idx_heads):
    c128, s128 = c128_ref[...], s128_ref[...]
    c64, s64 = c64_ref[...], s64_ref[...]
    tt = c128.shape[0]
    lane = lax.broadcasted_iota(I32, (tt, LANES), 1)
    qg, kg = qg_ref[...], kg_ref[...]
    for h in range(att_heads):
        sl = slice(h * LANES, (h + 1) * LANES)
        q = q_ref[:, sl]
        q = q * lax.rsqrt(jnp.mean(q * q, axis=-1, keepdims=True) + NORM_EPS) * qg
        qo_ref[:, sl] = _rope128(q, c128, s128).astype(BF16)
        k = k_ref[:, sl]
        k = k * lax.rsqrt(jnp.mean(k * k, axis=-1, keepdims=True) + NORM_EPS) * kg
        k = _rope128(k, c128, s128)
        kf_ref[:, sl] = k
        kb_ref[:, sl] = k.astype(BF16)
        vb_ref[:, sl] = v_ref[:, sl].astype(BF16)
    for p in range(idx_heads * idx_dim // LANES):
        sl = slice(p * LANES, (p + 1) * LANES)
        iqo_ref[:, sl] = _rope64(iq_ref[:, sl], c64, s64, lane).astype(BF16)
    small = small_ref[...]
    is_ik = lane < idx_dim
    ik = jnp.where(is_ik, small, 0.0)
    ms = jnp.sum(ik * ik, axis=-1, keepdims=True) * (1.0 / idx_dim)
    ik = _rope64(ik * lax.rsqrt(ms + NORM_EPS) * ikg_ref[...], c64, s64, lane)
    ik = jnp.where(is_ik, ik, 0.0)
    smallo_ref[...] = jnp.where(is_ik, ik, small * (idx_heads ** -0.5))
    ikd_ref[...] = (ik + pltpu.roll(ik, 64, 1)).astype(BF16)
    cc = u_ref.shape[-1] // 2
    uo_ref[...] = u_ref[:, :cc] * jax.nn.sigmoid(u_ref[:, cc:])


def even_post(qkvi, small, u, tables, q_g, k_g, ik_g, att_heads, idx_dim):
    b, t, _ = qkvi.shape
    aw = att_heads * LANES
    iw = IDX_HEADS * idx_dim
    cc = u.shape[-1] // 2
    tt = _row_tile(t, 256)
    c128, s128, c64, s64 = tables

    def row(width, col=0):
        return pl.BlockSpec((None, tt, width), lambda bi, i: (bi, i, col))

    tab = pl.BlockSpec((tt, LANES), lambda bi, i: (i, 0))
    vec = pl.BlockSpec((1, LANES), lambda bi, i: (0, 0))
    ikg_pad = jnp.zeros((1, LANES), F32).at[0, :idx_dim].set(ik_g)
    assert aw == iw, "q/k/v/indexer-q column groups are addressed as equal-width blocks"
    return pl.pallas_call(
        functools.partial(_even_post_kernel, att_heads=att_heads, idx_dim=idx_dim, idx_heads=IDX_HEADS),
        grid=(b, t // tt),
        in_specs=[row(aw, 0), row(aw, 1), row(aw, 2), row(iw, 3), row(LANES), row(2 * cc),
                  tab, tab, tab, tab, vec, vec, vec],
        out_specs=[row(aw), row(aw), row(aw), row(aw), row(iw), row(LANES), row(LANES), row(cc)],
        out_shape=[
            jax.ShapeDtypeStruct((b, t, aw), BF16),
            jax.ShapeDtypeStruct((b, t, aw), F32),
            jax.ShapeDtypeStruct((b, t, aw), BF16),
            jax.ShapeDtypeStruct((b, t, aw), BF16),
            jax.ShapeDtypeStruct((b, t, iw), BF16),
            jax.ShapeDtypeStruct((b, t, LANES), F32),
            jax.ShapeDtypeStruct((b, t, LANES), BF16),
            jax.ShapeDtypeStruct((b, t, cc), F32),
        ],
        compiler_params=_cparams(("arbitrary", "arbitrary")),
        name="even_post",
    )(qkvi, qkvi, qkvi, qkvi, small, u, c128, s128, c64, s64,
      q_g.reshape(1, LANES), k_g.reshape(1, LANES), ikg_pad)


def _order_key(score):
    bits = pltpu.bitcast(score, I32)
    return jnp.where(bits < 0, bits ^ 0x7FFFFFFF, bits)


def _kth_largest_key(count_ge, top_k, shape):
    tau = jnp.where(count_ge(jnp.zeros(shape, I32)) >= top_k, 0, INT_MIN).astype(I32)

    def body(i, tau):
        cand = tau | jnp.left_shift(jnp.int32(1), 30 - i)
        return jnp.where(count_ge(cand) >= top_k, cand, tau)

    return lax.fori_loop(0, 31, body, tau)


def _idx_head_lhs(iq, h, lane):
    slab = iq[:, (h // 2) * LANES:(h // 2 + 1) * LANES]
    keep = (lane < 64) if h % 2 == 0 else (lane >= 64)
    return jnp.where(keep, slab, jnp.zeros_like(slab))


def _idx_weight(small, h, lane, idx_dim):
    return jnp.sum(jnp.where(lane == idx_dim + h, small, 0.0), axis=-1, keepdims=True)


KEY_CHUNK = 512


def _dsa_prompt_kernel(q_ref, iq_ref, small_t_ref, k_ref, v_t_ref, ikd_ref, o_ref, lhs_sc, key_sc,
                       *, top_k, att_heads, idx_dim):
    n = pl.program_id(1)
    qb = q_ref.shape[0]
    n_chunks = ((n + 1) * qb + KEY_CHUNK - 1) // KEY_CHUNK
    lane = lax.broadcasted_iota(I32, (qb, LANES), 1)
    iq = iq_ref[...]
    for h in range(IDX_HEADS):
        lhs_sc[h // 2, (h % 2) * qb:(h % 2 + 1) * qb, :] = _idx_head_lhs(iq, h, lane)
    small_t = small_t_ref[...]
    weights = [small_t[idx_dim + h:idx_dim + h + 1, :] * (idx_dim ** -0.5) for h in range(IDX_HEADS)]
    qpos = lax.broadcasted_iota(I32, (KEY_CHUNK, qb), 1) + n * qb
    krow = lax.broadcasted_iota(I32, (KEY_CHUNK, qb), 0)

    def chunk_rows(c):
        return pl.ds(pl.multiple_of(c * KEY_CHUNK, KEY_CHUNK), KEY_CHUNK)

    def score_chunk(c, carry):
        ikd = ikd_ref[chunk_rows(c), :]
        score = jnp.zeros((KEY_CHUNK, qb), F32)
        for pair in range(IDX_HEADS // 2):
            d = jnp.maximum(_dot_nt(ikd, lhs_sc[pair]), 0.0)
            score = score + weights[2 * pair] * d[:, :qb] + weights[2 * pair + 1] * d[:, qb:]
        admissible = krow + c * KEY_CHUNK <= qpos
        key_sc[c] = jnp.where(admissible, _order_key(score), INT_MIN)
        return carry

    lax.fori_loop(0, n_chunks, score_chunk, 0)

    def count_ge(cand):
        return lax.fori_loop(
            0, n_chunks,
            lambda c, acc: acc + jnp.sum((key_sc[c] >= cand).astype(I32), axis=0, keepdims=True),
            jnp.zeros((1, qb), I32))

    tau = jnp.maximum(_kth_largest_key(count_ge, top_k, (1, qb)), INT_MIN + 1)
    scale = LANES ** -0.5
    heads = [slice(h * LANES, (h + 1) * LANES) for h in range(att_heads)]

    def attend_chunk(c, carry):
        sel = key_sc[c] >= tau
        out = []
        for sl, (m_old, l_old, acc) in zip(heads, carry):
            s = jnp.where(sel, _dot_nt(k_ref[chunk_rows(c), sl], q_ref[:, sl]) * scale, NEG_BIG)
            m_new = jnp.maximum(m_old, jnp.max(s, axis=0, keepdims=True))
            alpha = jnp.exp(m_old - m_new)
            p = jnp.exp(s - m_new)
            l_new = alpha * l_old + jnp.sum(p, axis=0, keepdims=True)
            out.append((m_new, l_new, alpha * acc + _dot(v_t_ref[c, sl, :], p.astype(BF16))))
        return tuple(out)

    init = tuple((jnp.full((1, qb), NEG_BIG, F32), jnp.zeros((1, qb), F32), jnp.zeros((LANES, qb), F32))
                 for _ in heads)
    for sl, (_, l_fin, acc) in zip(heads, lax.fori_loop(0, n_chunks, attend_chunk, init)):
        o_ref[:, sl] = (acc / l_fin).T.astype(o_ref.dtype)


def dsa_prompt(q_bf, iq_bf, small, k_bf, v_bf, ikd_bf, idx_dim):
    b, t, aw = q_bf.shape
    assert t % KEY_CHUNK == 0 and KEY_CHUNK % QBLOCK == 0
    top_k = min(TOPK_MAX, t // 4)
    n_kc = t // KEY_CHUNK
    small_t = jnp.swapaxes(small, 1, 2)
    v_t = jnp.swapaxes(v_bf.reshape(b, n_kc, KEY_CHUNK, aw), 2, 3)
    qrow = lambda w: pl.BlockSpec((None, QBLOCK, w), lambda bi, n: (bi, n, 0))
    full = lambda w: pl.BlockSpec((None, t, w), lambda bi, n: (bi, 0, 0))
    return pl.pallas_call(
        functools.partial(_dsa_prompt_kernel, top_k=top_k, att_heads=aw // LANES, idx_dim=idx_dim),
        grid=(b, t // QBLOCK),
        in_specs=[qrow(aw), qrow(iq_bf.shape[-1]),
                  pl.BlockSpec((None, LANES, QBLOCK), lambda bi, n: (bi, 0, n)),
                  full(aw),
                  pl.BlockSpec((None, n_kc, aw, KEY_CHUNK), lambda bi, n: (bi, 0, 0, 0)),
                  full(LANES)],
        out_specs=qrow(aw),
        out_shape=jax.ShapeDtypeStruct((b, t, aw), BF16),
        scratch_shapes=[pltpu.VMEM((IDX_HEADS // 2, 2 * QBLOCK, LANES), BF16),
                        pltpu.VMEM((n_kc, KEY_CHUNK, QBLOCK), I32)],
        compiler_params=_cparams(("arbitrary", "arbitrary")),
        name="dsa_prompt",
    )(q_bf, iq_bf, small_t, k_bf, v_t, ikd_bf)


def _idx_lhs_all_heads(iq, lane):
    iq = iq.astype(F32)
    return jnp.concatenate([_idx_head_lhs(iq, h, lane) for h in range(IDX_HEADS)], axis=0).astype(BF16)


def _idx_scores(lhs, small, keys_dup, lane, idx_dim):
    d = _dot_nt(lhs, keys_dup) * (idx_dim ** -0.5)
    score = jnp.zeros((SAMPLE_T_PAD, keys_dup.shape[0]), F32)
    for h in range(IDX_HEADS):
        dh = d[h * SAMPLE_T_PAD:(h + 1) * SAMPLE_T_PAD, :]
        score = score + _idx_weight(small, h, lane, idx_dim) * jnp.maximum(dh, 0.0)
    return score


def _sample_idx_kernel(pt_ref, iq_ref, small_ref, *rest, idx_dim):
    page_refs, o_ref = rest[:-1], rest[-1]
    lane = lax.broadcasted_iota(I32, (SAMPLE_T_PAD, LANES), 1)
    keys = jnp.concatenate([r[...] for r in page_refs], axis=0)
    keys_dup = jnp.concatenate([keys, keys], axis=-1).astype(BF16)
    lhs = _idx_lhs_all_heads(iq_ref[...], lane)
    o_ref[...] = _idx_scores(lhs, small_ref[...], keys_dup, lane, idx_dim)


def sample_idx_scores(page_table, iq_bf, small, cache_idx_k, layer):
    b, n_pages = page_table.shape
    page, idx_dim = cache_idx_k.shape[-2:]
    g = IDX_PAGES_PER_STEP if n_pages % IDX_PAGES_PER_STEP == 0 else 1

    def page_spec(j):
        return pl.BlockSpec((None, None, page, idx_dim),
                            lambda bi, s, pt: (layer, pt[bi * n_pages + s * g + j], 0, 0))

    row = lambda w: pl.BlockSpec((None, SAMPLE_T_PAD, w), lambda bi, s, pt: (bi, 0, 0))
    grid_spec = pltpu.PrefetchScalarGridSpec(
        num_scalar_prefetch=1,
        grid=(b, n_pages // g),
        in_specs=[row(iq_bf.shape[-1]), row(LANES)] + [page_spec(j) for j in range(g)],
        out_specs=pl.BlockSpec((None, SAMPLE_T_PAD, g * page), lambda bi, s, pt: (bi, 0, s)),
    )
    return pl.pallas_call(
        functools.partial(_sample_idx_kernel, idx_dim=idx_dim),
        grid_spec=grid_spec,
        out_shape=jax.ShapeDtypeStruct((b, SAMPLE_T_PAD, n_pages * page), F32),
        compiler_params=_cparams(("arbitrary", "arbitrary")),
        name="sample_idx",
    )(page_table.reshape(-1), iq_bf, small, *([cache_idx_k] * g))


def _sample_attn_kernel(pt_ref, sc_all_ref, sc_ref, q_ref, iq_ref, small_ref, ikn_ref, kn_ref, vn_ref,
                        *rest, top_k, n_new, att_heads, idx_dim, n_groups):
    g = (len(rest) - 8) // 2
    k_pages, v_pages = rest[:g], rest[g:2 * g]
    o_ref, qrows_ref, erow_ref, tau_ref, keyn_ref, m_ref, l_ref, acc_ref = rest[2 * g:]
    s_idx = pl.program_id(1)
    page = k_pages[0].shape[0]
    scale = LANES ** -0.5
    assert att_heads == SUBLANES and att_heads * SAMPLE_T_PAD <= LANES and page <= LANES

    @pl.when(s_idx == 0)
    def _():
        lane = lax.broadcasted_iota(I32, (SAMPLE_T_PAD, LANES), 1)
        tok = lax.broadcasted_iota(I32, (SAMPLE_T_PAD, LANES), 0)
        lhs = _idx_lhs_all_heads(iq_ref[...], lane)
        sc_new = _idx_scores(lhs, small_ref[...], ikn_ref[...], lane, idx_dim)
        new_ok = (lane <= tok) & (lane < n_new)
        key_new = jnp.where(new_ok, _order_key(sc_new), INT_MIN)
        key_past = _order_key(sc_all_ref[...])

        def count_ge(cand):
            return (jnp.sum((key_past >= cand).astype(I32), axis=-1, keepdims=True)
                    + jnp.sum((key_new >= cand).astype(I32), axis=-1, keepdims=True))

        tau = _kth_largest_key(count_ge, top_k, (SAMPLE_T_PAD, 1))
        tau_ref[...] = jnp.broadcast_to(tau, (SAMPLE_T_PAD, LANES))
        keyn_ref[...] = key_new
        q = q_ref[...].astype(F32)
        q_rows = [q[:, h * LANES:(h + 1) * LANES] for h in range(att_heads)]
        q_rows.append(jnp.zeros((LANES - att_heads * SAMPLE_T_PAD, LANES), F32))
        qrows_ref[...] = jnp.concatenate(q_rows, axis=0).astype(BF16)
        r_tok = lax.broadcasted_iota(I32, erow_ref.shape, 0) // att_heads
        c_tok = lax.broadcasted_iota(I32, erow_ref.shape, 1)
        erow_ref[...] = jnp.where(r_tok == c_tok, 1.0, 0.0).astype(BF16)
        m_ref[...] = jnp.full(m_ref.shape, NEG_BIG, F32)
        l_ref[...] = jnp.zeros(l_ref.shape, F32)
        acc_ref[...] = jnp.zeros(acc_ref.shape, F32)

    tau = tau_ref[:, 0:1]
    rep = jnp.where(lax.broadcasted_iota(I32, (LANES, LANES), 0)
                    == lax.broadcasted_iota(I32, (LANES, LANES), 1) % SAMPLE_T_PAD, 1.0, 0.0).astype(BF16)
    head_match8 = (lax.broadcasted_iota(I32, (SUBLANES, LANES), 0)
                   == lax.broadcasted_iota(I32, (SUBLANES, LANES), 1) // SAMPLE_T_PAD)

    def masked_scores(sel8, keys):
        rows = keys.shape[0]
        n_tok = rows // att_heads
        sel_pad = jnp.concatenate([sel8.astype(F32), jnp.zeros((LANES - SAMPLE_T_PAD, LANES), F32)], axis=0)
        sel_rep = _dot(sel_pad.T.astype(BF16), rep)
        sel_rows = _dot(erow_ref[0:rows, :], sel_rep.astype(BF16))
        head_match = jnp.broadcast_to(head_match8[None], (n_tok, SUBLANES, LANES)).reshape(rows, LANES)
        mask = (sel_rows > 0.5) & head_match
        return jnp.where(mask, _dot_nt(keys, qrows_ref[...]) * scale, NEG_BIG)

    def attend(groups):
        scores = [masked_scores(sel8, keys) for sel8, keys, _ in groups]
        m_old = m_ref[0:1, :]
        m_new = m_old
        for s in scores:
            m_new = jnp.maximum(m_new, jnp.max(s, axis=0, keepdims=True))
        alpha = jnp.exp(m_old - m_new)
        l_new = alpha * l_ref[0:1, :]
        acc = alpha * acc_ref[...]
        for s, (_, _, vals) in zip(scores, groups):
            p = jnp.exp(s - m_new)
            l_new = l_new + jnp.sum(p, axis=0, keepdims=True)
            acc = acc + lax.dot_general(vals, p.astype(BF16), (((0,), (0,)), ((), ())),
                                        preferred_element_type=F32)
        l_ref[...] = jnp.broadcast_to(l_new, l_ref.shape)
        acc_ref[...] = acc
        m_ref[...] = jnp.broadcast_to(m_new, m_ref.shape)

    def as_rows(ref):
        return ref[...].reshape(ref.shape[0] * att_heads, LANES).astype(BF16)

    attend([(_order_key(sc_ref[:, j * page:(j + 1) * page]) >= tau, as_rows(k_pages[j]), as_rows(v_pages[j]))
            for j in range(g)])

    @pl.when(s_idx == n_groups - 1)
    def _():
        attend([(keyn_ref[...] >= tau, as_rows(kn_ref), as_rows(vn_ref))])
        out = (acc_ref[...] / l_ref[0:1, :]).T
        for h in range(att_heads):
            o_ref[:, h * LANES:(h + 1) * LANES] = (
                out[h * SAMPLE_T_PAD:(h + 1) * SAMPLE_T_PAD, :].astype(o_ref.dtype))


def sample_attention(page_table, scores, q_bf, iq_bf, small, ikn_pad, k_new, v_new,
                     cache_k, cache_v, layer, n_new, idx_dim):
    b, n_pages = page_table.shape
    page = cache_k.shape[2]
    aw = q_bf.shape[-1]
    att_heads = aw // LANES
    past = n_pages * page
    top_k = min(TOPK_MAX, (past + n_new) // 4)
    g = PAGES_PER_STEP if n_pages % PAGES_PER_STEP == 0 else 1
    n_groups = n_pages // g

    def page_spec(j):
        return pl.BlockSpec((None, None, page, att_heads, LANES),
                            lambda bi, s, pt: (layer, pt[bi * n_pages + s * g + j], 0, 0, 0))

    def per_seq(r, w):
        return pl.BlockSpec((None, r, w), lambda bi, s, pt: (bi, 0, 0))

    new_rows = pl.BlockSpec((None, SAMPLE_T_PAD, att_heads, LANES), lambda bi, s, pt: (bi, 0, 0, 0))
    grid_spec = pltpu.PrefetchScalarGridSpec(
        num_scalar_prefetch=1,
        grid=(b, n_groups),
        in_specs=[per_seq(SAMPLE_T_PAD, past),
                  pl.BlockSpec((None, SAMPLE_T_PAD, g * page), lambda bi, s, pt: (bi, 0, s)),
                  per_seq(SAMPLE_T_PAD, aw), per_seq(SAMPLE_T_PAD, iq_bf.shape[-1]),
                  per_seq(SAMPLE_T_PAD, LANES),
                  per_seq(LANES, LANES), new_rows, new_rows]
                 + [page_spec(j) for j in range(g)] * 2,
        out_specs=per_seq(SAMPLE_T_PAD, aw),
        scratch_shapes=[
            pltpu.VMEM((LANES, LANES), BF16),
            pltpu.VMEM((page * att_heads, LANES), BF16),
            pltpu.VMEM((SAMPLE_T_PAD, LANES), I32),
            pltpu.VMEM((SAMPLE_T_PAD, LANES), I32),
            pltpu.VMEM((SUBLANES, LANES), F32),
            pltpu.VMEM((SUBLANES, LANES), F32),
            pltpu.VMEM((LANES, LANES), F32),
        ],
    )
    return pl.pallas_call(
        functools.partial(_sample_attn_kernel, top_k=top_k, n_new=n_new, att_heads=att_heads,
                          idx_dim=idx_dim, n_groups=n_groups),
        grid_spec=grid_spec,
        out_shape=jax.ShapeDtypeStruct((b, SAMPLE_T_PAD, aw), BF16),
        compiler_params=_cparams(("arbitrary", "arbitrary")),
        name="sample_attn",
    )(page_table.reshape(-1), scores, scores, q_bf, iq_bf, small, ikn_pad,
      k_new.reshape(b, SAMPLE_T_PAD, att_heads, LANES), v_new.reshape(b, SAMPLE_T_PAD, att_heads, LANES),
      *([cache_k] * g), *([cache_v] * g))


HIST_ROWS = 32
CONV_ROWS = 32


def _conv_kernel(*refs, width, multi_tile):
    if multi_tile:
        u_ref, prev_ref, hist_ref, cw_ref, cb_ref, g_ref, b_ref, o_ref, buf = refs
    else:
        u_ref, hist_ref, cw_ref, cb_ref, g_ref, b_ref, o_ref, buf = refs
    tt = u_ref.shape[0]
    buf[HIST_ROWS:HIST_ROWS + tt, :] = u_ref[...]
    if multi_tile:
        buf[0:HIST_ROWS, :] = jnp.where(pl.program_id(1) == 0, hist_ref[...], prev_ref[...])
    else:
        buf[0:HIST_ROWS, :] = hist_ref[...]
    first = HIST_ROWS - (width - 1)
    rows = min(CONV_ROWS, tt)
    cb, g, b = cb_ref[...], g_ref[...], b_ref[...]
    for r0 in range(0, tt, rows):
        acc = jnp.zeros((rows, u_ref.shape[1]), F32)
        for j in range(width):
            acc = acc + buf[r0 + first + j:r0 + first + j + rows, :] * cw_ref[j:j + 1, :]
        y = acc + cb
        yc = y - jnp.mean(y, axis=-1, keepdims=True)
        var = jnp.mean(yc * yc, axis=-1, keepdims=True)
        y = yc * lax.rsqrt(var + LN_EPS) * g + b
        o_ref[r0:r0 + rows, :] = _silu(y).astype(o_ref.dtype)


def conv_branch(u, hist, cw, cb, ln_g, ln_b):
    b, t, c = u.shape
    width = cw.shape[0]
    tt = _row_tile(t, 256)
    multi_tile = t > tt
    row = pl.BlockSpec((None, tt, c), lambda bi, i: (bi, i, 0))
    vec = pl.BlockSpec((1, c), lambda bi, i: (0, 0))
    specs, args = [row], [u]
    if multi_tile:
        per = tt // HIST_ROWS
        specs.append(pl.BlockSpec((None, HIST_ROWS, c), lambda bi, i: (bi, jnp.maximum(i * per - 1, 0), 0)))
        args.append(u)
    specs += [pl.BlockSpec((None, HIST_ROWS, c), lambda bi, i: (bi, 0, 0)),
              pl.BlockSpec((width, c), lambda bi, i: (0, 0)), vec, vec, vec]
    args += [hist, cw, cb.reshape(1, c), ln_g.reshape(1, c), ln_b.reshape(1, c)]
    return pl.pallas_call(
        functools.partial(_conv_kernel, width=width, multi_tile=multi_tile),
        grid=(b, t // tt),
        in_specs=specs,
        out_specs=row,
        out_shape=jax.ShapeDtypeStruct((b, t, c), BF16),
        scratch_shapes=[pltpu.VMEM((HIST_ROWS + tt, c), F32)],
        compiler_params=_cparams(("arbitrary", "arbitrary")),
        name="conv_branch",
    )(*args)


def _rw_mix_kernel(h_ref, prev_ref, shift_ref, mu_ref, o_ref, buf):
    tt = h_ref.shape[0]
    h = h_ref[...]
    buf[SUBLANES:SUBLANES + tt, :] = h
    buf[SUBLANES - 1:SUBLANES, :] = jnp.where(pl.program_id(1) == 0, shift_ref[...],
                                              prev_ref[SUBLANES - 1:SUBLANES, :])
    xx = buf[SUBLANES - 1:SUBLANES - 1 + tt, :] - h
    for j in range(o_ref.shape[0]):
        o_ref[j] = (h + xx * mu_ref[j:j + 1, :]).astype(o_ref.dtype)


def rw_mix(h, shift_prev, mu):
    b, t, d = h.shape
    n_mix = mu.shape[0]
    tt = _row_tile(t, 256)
    per = tt // SUBLANES
    return pl.pallas_call(
        _rw_mix_kernel,
        grid=(b, t // tt),
        in_specs=[
            pl.BlockSpec((None, tt, d), lambda bi, i: (bi, i, 0)),
            pl.BlockSpec((None, SUBLANES, d), lambda bi, i: (bi, jnp.maximum(i * per - 1, 0), 0)),
            pl.BlockSpec((None, 1, d), lambda bi, i: (bi, 0, 0)),
            pl.BlockSpec((n_mix, d), lambda bi, i: (0, 0)),
        ],
        out_specs=pl.BlockSpec((n_mix, None, tt, d), lambda bi, i: (0, bi, i, 0)),
        out_shape=jax.ShapeDtypeStruct((n_mix, b, t, d), BF16),
        scratch_shapes=[pltpu.VMEM((SUBLANES + tt, d), F32)],
        compiler_params=_cparams(("arbitrary", "arbitrary")),
        name="rw_mix",
    )(h, h, shift_prev.reshape(b, 1, d), mu)


RW_HEAD = 64


def _seg_sum(x, ones_blk):
    outs = []
    for s in range(x.shape[1] // LANES):
        xs = x[:, s * LANES:(s + 1) * LANES]
        hi = xs.astype(BF16)
        lo = (xs - hi.astype(F32)).astype(BF16)
        outs.append(_dot(hi, ones_blk) + _dot(lo, ones_blk))
    return jnp.concatenate(outs, axis=-1)


def _wkv_kernel(r_ref, k_ref, v_ref, wl_ref, al_ref, g_ref,
                w0_ref, a0_ref, kk_ref, ka_ref, rk_ref, lg_ref, lb_ref, s0_ref,
                o_ref, s_ref, dec_sc, kk_sc, b_sc, kh_sc, y_sc, *, n_steps):
    i = pl.program_id(1)
    tc, d = r_ref.shape
    rb = lax.broadcasted_iota(I32, (LANES, LANES), 0) // RW_HEAD
    cb = lax.broadcasted_iota(I32, (LANES, LANES), 1) // RW_HEAD
    ones_blk = jnp.where(rb == cb, 1.0, 0.0).astype(BF16)
    eye2 = (lax.broadcasted_iota(I32, (RW_HEAD, LANES), 0)
            == lax.broadcasted_iota(I32, (RW_HEAD, LANES), 1) % RW_HEAD)

    @pl.when(i == 0)
    def _():
        s_ref[...] = s0_ref[...]

    k = k_ref[...]
    z = -(w0_ref[...] + wl_ref[...])
    softplus = jnp.maximum(z, 0.0) + jnp.log(1.0 + jnp.exp(-jnp.abs(z)))
    dec_sc[...] = jnp.exp(-jnp.exp(-softplus - 0.5))
    a = jax.nn.sigmoid(a0_ref[...] + al_ref[...])
    kk = k * kk_ref[...]
    kk = kk * lax.rsqrt(jnp.maximum(_seg_sum(kk * kk, ones_blk), 1e-24))
    kk_sc[...] = kk
    b_sc[...] = kk * a
    kh_sc[...] = k * (1.0 + (a - 1.0) * ka_ref[...])
    assert -(-n_steps // SUBLANES) * SUBLANES == tc, "the recurrence blocks must cover the whole time tile"

    steps_per_block = min(SUBLANES, n_steps)

    slabs = [slice(s * LANES, (s + 1) * LANES) for s in range(d // LANES)]
    head_sel = (lax.broadcasted_iota(I32, (SUBLANES, LANES), 0)
                == lax.broadcasted_iota(I32, (SUBLANES, LANES), 1) // RW_HEAD).astype(BF16)

    def block(blk, carry):
        rows = pl.ds(pl.multiple_of(blk * SUBLANES, SUBLANES), SUBLANES)
        st = [s_ref[:, sl] for sl in slabs]
        y_rows = [[] for _ in slabs]
        for j in range(steps_per_block):
            sa, vb = [], []
            for s, sl in enumerate(slabs):
                lhs = jnp.concatenate([st[s] * kk_sc[rows, sl][j:j + 1],
                                       jnp.where(eye2, v_ref[rows, sl][j:j + 1], 0.0)], axis=0)
                both = _dot(lhs.astype(BF16), ones_blk)
                sa.append(both[:RW_HEAD])
                vb.append(both[RW_HEAD:])
            for s, sl in enumerate(slabs):
                st[s] = (st[s] * dec_sc[rows, sl][j:j + 1] - sa[s] * b_sc[rows, sl][j:j + 1]
                         + vb[s] * kh_sc[rows, sl][j:j + 1])
            for s, sl in enumerate(slabs):
                yh = _dot_nt(head_sel, (st[s] * r_ref[rows, sl][j:j + 1]).astype(BF16))
                y_rows[s].append(jnp.concatenate([yh[0:1], yh[1:2]], axis=-1))
        for s, sl in enumerate(slabs):
            s_ref[:, sl] = st[s]
            pad = [jnp.zeros((1, LANES), F32)] * (SUBLANES - steps_per_block)
            y_sc[rows, sl] = jnp.concatenate(y_rows[s] + pad, axis=0)
        return carry

    lax.fori_loop(0, -(-n_steps // SUBLANES), block, 0)

    y = y_sc[...]
    inv_n = 1.0 / RW_HEAD
    yc = y - _seg_sum(y, ones_blk) * inv_n
    var = _seg_sum(yc * yc, ones_blk) * inv_n
    y = yc * lax.rsqrt(var + LNX_EPS) * lg_ref[...] + lb_ref[...]
    r = r_ref[...]
    y = y + _seg_sum(r * kh_sc[...] * rk_ref[...], ones_blk) * v_ref[...]
    o_ref[...] = (y * g_ref[...]).astype(o_ref.dtype)


def wkv(r, k, v, wl, al, g, w0, a0, k_k, k_a, r_k, lnx_g, lnx_b, s0, n_steps):
    b, t, d = r.shape
    tc = _row_tile(t, 128)
    steps = tc if t > tc else n_steps
    row = pl.BlockSpec((None, tc, d), lambda bi, i: (bi, i, 0))
    vec = pl.BlockSpec((1, d), lambda bi, i: (0, 0))
    st = pl.BlockSpec((None, RW_HEAD, d), lambda bi, i: (bi, 0, 0))
    vecs = [x.reshape(1, d) for x in (w0, a0, k_k, k_a, r_k, lnx_g, lnx_b)]
    return pl.pallas_call(
        functools.partial(_wkv_kernel, n_steps=steps),
        grid=(b, t // tc),
        in_specs=[row] * 6 + [vec] * 7 + [st],
        out_specs=[row, st],
        out_shape=[jax.ShapeDtypeStruct((b, t, d), BF16), jax.ShapeDtypeStruct((b, RW_HEAD, d), F32)],
        scratch_shapes=[pltpu.VMEM((tc, d), F32)] * 5,
        compiler_params=_cparams(("arbitrary", "arbitrary")),
        name="wkv",
    )(r, k, v, wl, al, g, *vecs, s0)


WKV_CHUNK = 64
WKV_TILE = 1024
WKV_TILE_MIN = 512


def _split3(x):
    h1 = x.astype(BF16)
    r1 = x - h1.astype(F32)
    h2 = r1.astype(BF16)
    return h1, h2, (r1 - h2.astype(F32)).astype(BF16)


def _wkv_prep_kernel(k_ref, wl_ref, al_ref, w0_ref, a0_ref, kkp_ref, ka_ref, lw_ref, kk_ref, b_ref, kh_ref):
    rb = lax.broadcasted_iota(I32, (LANES, LANES), 0) // RW_HEAD
    cb = lax.broadcasted_iota(I32, (LANES, LANES), 1) // RW_HEAD
    ones_blk = jnp.where(rb == cb, 1.0, 0.0).astype(BF16)
    k = k_ref[...]
    z = -(w0_ref[...] + wl_ref[...])
    softplus = jnp.maximum(z, 0.0) + jnp.log(1.0 + jnp.exp(-jnp.abs(z)))
    lw_ref[...] = -jnp.exp(-softplus - 0.5)
    a = jax.nn.sigmoid(a0_ref[...] + al_ref[...])
    kk = k * kkp_ref[...]
    kk = kk * lax.rsqrt(jnp.maximum(_seg_sum(kk * kk, ones_blk), 1e-24))
    kk_ref[...] = kk
    b_ref[...] = kk * a
    kh_ref[...] = k * (1.0 + (a - 1.0) * ka_ref[...])


def wkv_prep(k, wl, al, w0, a0, k_k, k_a):
    b, t, d = k.shape
    tt = _row_tile(t, 256)
    row = pl.BlockSpec((None, tt, d), lambda bi, i: (bi, i, 0))
    vec = pl.BlockSpec((1, d), lambda bi, i: (0, 0))
    return pl.pallas_call(
        _wkv_prep_kernel,
        grid=(b, t // tt),
        in_specs=[row] * 3 + [vec] * 4,
        out_specs=[row] * 4,
        out_shape=[jax.ShapeDtypeStruct((b, t, d), F32)] * 4,
        compiler_params=_cparams(("arbitrary", "arbitrary")),
        name="wkv_prep",
    )(k, wl, al, *[x.reshape(1, d) for x in (w0, a0, k_k, k_a)])


def _wkv_chunk_kernel(r_ref, lw_ref, kh_ref, v_ref, kk_ref, b_ref, lwt_ref, kht_ref, bt_ref, h0_ref,
                      y_ref, h_ref, *, n_chunks):
    c_ = WKV_CHUNK

    @pl.when(pl.program_id(2) == 0)
    def _():
        h_ref[...] = h0_ref[...]

    row = lax.broadcasted_iota(I32, (c_, c_), 0)
    col = lax.broadcasted_iota(I32, (c_, c_), 1)
    tri_incl = col <= row
    tri_strict = col < row
    l_incl = jnp.where(tri_incl, 1.0, 0.0).astype(BF16)
    u_incl = jnp.where(row <= col, 1.0, 0.0).astype(BF16)
    lane_head = lax.broadcasted_iota(I32, (1, LANES), 1) // RW_HEAD
    row_head = lax.broadcasted_iota(I32, (LANES, 1), 0) // RW_HEAD
    n_heads = LANES // RW_HEAD
    chunks = range(n_chunks)
    items = [(c, hh) for c in chunks for hh in range(n_heads)]
    ch = []
    for c in chunks:
        rs = slice(c * c_, (c + 1) * c_)
        lw, r, kh, v, kk, b = lw_ref[rs, :], r_ref[rs, :], kh_ref[rs, :], v_ref[rs, :], kk_ref[rs, :], b_ref[rs, :]
        cum = sum(_dot(l_incl, part) for part in _split3(lw))
        e_neg = jnp.exp(-cum)
        bt = b * e_neg
        kt = kh * e_neg
        cumt = sum(_dot(part, u_incl) for part in _split3(lwt_ref[c]))
        dt = jnp.exp(cumt[:, c_ - 1:c_] - cumt)
        ch.append(dict(
            kkt=kk * jnp.exp(cum - lw), rt=r * jnp.exp(cum),
            y1=jnp.concatenate([bt, kt], axis=0).astype(BF16), v_bf=v.astype(BF16),
            g_col=jnp.exp(cumt[:, c_ - 1:c_]),
            bht=bt_ref[c] * dt, kht=kht_ref[c] * dt))
    it = {}
    for c, hh in items:
        m = lane_head == hh
        kkt_h = jnp.where(m, ch[c]["kkt"], 0.0)
        rt_h = jnp.where(m, ch[c]["rt"], 0.0)
        gmat = _dot_nt(jnp.concatenate([kkt_h, rt_h], axis=0).astype(BF16), ch[c]["y1"])
        it[c, hh] = dict(
            kkt_h=kkt_h, rt_h=rt_h,
            a_b=jnp.where(tri_strict, gmat[:c_, :c_], 0.0).astype(BF16),
            a_k=jnp.where(tri_strict, gmat[:c_, c_:], 0.0).astype(BF16),
            m_b=jnp.where(tri_incl, gmat[c_:, :c_], 0.0).astype(BF16),
            m_k=jnp.where(tri_incl, gmat[c_:, c_:], 0.0).astype(BF16))
    for c, hh in items:
        d_ = it[c, hh]
        d_["x"] = jnp.concatenate([d_["kkt_h"], _dot(d_["a_k"], ch[c]["v_bf"])], axis=1)
        d_["pows"] = [d_["a_b"]]
    n_levels = c_.bit_length() - 1
    for _ in range(n_levels - 1):
        for key in items:
            pows = it[key]["pows"]
            pows.append(_dot(pows[-1], pows[-1]).astype(BF16))
    eye_c = jnp.where(row == col, 1.0, 0.0)
    for key in items:
        it[key]["t"] = eye_c - it[key]["a_b"].astype(F32)
    for level in range(1, n_levels):
        for key in items:
            d_ = it[key]
            d_["t"] = d_["t"] + _dot(d_["t"].astype(BF16), d_["pows"][level])
    hst = h_ref[...]
    for c in chunks:
        for hh in range(n_heads):
            d_ = it[c, hh]
            m = lane_head == hh
            rmask = row_head == hh
            x_bf = _dot(d_["t"].astype(BF16), d_["x"].astype(BF16)).astype(BF16)
            mbx = _dot(d_["m_b"], x_bf)
            d_["rw"] = d_["rt_h"] - mbx[:, :LANES]
            d_["y0"] = jnp.where(m, _dot(d_["m_k"], ch[c]["v_bf"]) - mbx[:, LANES:], 0.0)
            bx = _dot(jnp.where(rmask, ch[c]["bht"], 0.0).astype(BF16), x_bf)
            d_["pm"] = -bx[:, :LANES]
            d_["qm"] = jnp.where(m, _dot(jnp.where(rmask, ch[c]["kht"], 0.0).astype(BF16), ch[c]["v_bf"])
                                 - bx[:, LANES:], 0.0)
        parts = [it[c, hh] for hh in range(n_heads)]
        h_bf = hst.astype(BF16)
        y_ref[c * c_:(c + 1) * c_, :] = (_dot(sum(p_["rw"] for p_ in parts).astype(BF16), h_bf)
                                         + sum(p_["y0"] for p_ in parts))
        hst = (ch[c]["g_col"] * hst + _dot(sum(p_["pm"] for p_ in parts).astype(BF16), h_bf)
               + sum(p_["qm"] for p_ in parts))
    h_ref[...] = hst


def wkv_chunked(r, lw, kh, v, kk, b_, h0):
    bsz, t, d = r.shape
    n_pairs = d // LANES
    tile = WKV_TILE if t % WKV_TILE == 0 else WKV_TILE_MIN
    per_tile = tile // WKV_CHUNK

    def chunk_major_t(x):
        return jnp.swapaxes(x.reshape(bsz, t // WKV_CHUNK, WKV_CHUNK, d), 2, 3)

    row = pl.BlockSpec((None, tile, LANES), lambda bi, p, i: (bi, i, p))
    row_t = pl.BlockSpec((None, per_tile, LANES, WKV_CHUNK), lambda bi, p, i: (bi, i, p, 0))
    st = pl.BlockSpec((None, None, LANES, LANES), lambda bi, p, i: (bi, p, 0, 0))
    return pl.pallas_call(
        functools.partial(_wkv_chunk_kernel, n_chunks=per_tile),
        grid=(bsz, n_pairs, t // tile),
        in_specs=[row] * 6 + [row_t] * 3 + [st],
        out_specs=[row, st],
        out_shape=[jax.ShapeDtypeStruct((bsz, t, d), F32),
                   jax.ShapeDtypeStruct((bsz, n_pairs, LANES, LANES), F32)],
        compiler_params=_cparams(("arbitrary", "arbitrary", "arbitrary")),
        name="wkv_chunked",
    )(r, lw, kh, v, kk, b_, chunk_major_t(lw), chunk_major_t(kh), chunk_major_t(b_), h0)


def _wkv_post_kernel(y_ref, r_ref, kh_ref, v_ref, g_ref, rk_ref, lg_ref, lb_ref, o_ref):
    rb = lax.broadcasted_iota(I32, (LANES, LANES), 0) // RW_HEAD
    cb = lax.broadcasted_iota(I32, (LANES, LANES), 1) // RW_HEAD
    ones_blk = jnp.where(rb == cb, 1.0, 0.0).astype(BF16)
    y = y_ref[...]
    inv_n = 1.0 / RW_HEAD
    yc = y - _seg_sum(y, ones_blk) * inv_n
    var = _seg_sum(yc * yc, ones_blk) * inv_n
    y = yc * lax.rsqrt(var + LNX_EPS) * lg_ref[...] + lb_ref[...]
    y = y + _seg_sum(r_ref[...] * kh_ref[...] * rk_ref[...], ones_blk) * v_ref[...]
    o_ref[...] = (y * g_ref[...]).astype(o_ref.dtype)


def wkv_post(y, r, kh, v, g, r_k, lnx_g, lnx_b):
    b, t, d = y.shape
    tt = _row_tile(t, 256)
    row = pl.BlockSpec((None, tt, d), lambda bi, i: (bi, i, 0))
    vec = pl.BlockSpec((1, d), lambda bi, i: (0, 0))
    return pl.pallas_call(
        _wkv_post_kernel,
        grid=(b, t // tt),
        in_specs=[row] * 5 + [vec] * 3,
        out_specs=row,
        out_shape=jax.ShapeDtypeStruct((b, t, d), BF16),
        compiler_params=_cparams(("arbitrary", "arbitrary")),
        name="wkv_post",
    )(y, r, kh, v, g, *[x.reshape(1, d) for x in (r_k, lnx_g, lnx_b)])


def _state_to_blockdiag(s):
    st = jnp.swapaxes(s, 2, 3)
    st = st.reshape(s.shape[0], s.shape[1] // 2, 2, RW_HEAD, RW_HEAD)
    z = jnp.zeros_like(st[:, :, 0])
    return jnp.concatenate([jnp.concatenate([st[:, :, 0], z], axis=-1),
                            jnp.concatenate([z, st[:, :, 1]], axis=-1)], axis=-2)


def _state_from_blockdiag(hbd):
    b, p = hbd.shape[:2]
    x = hbd.reshape(b, p, 2, RW_HEAD, 2, RW_HEAD)
    diag = jnp.stack([x[:, :, 0, :, 0, :], x[:, :, 1, :, 1, :]], axis=2)
    return jnp.swapaxes(diag.reshape(b, 2 * p, RW_HEAD, RW_HEAD), 2, 3)


def _state_to_kernel_layout(s):
    b, h, n, _ = s.shape
    return jnp.transpose(s, (0, 2, 1, 3)).reshape(b, n, h * n)


def _state_from_kernel_layout(s, heads):
    b, n, _ = s.shape
    return jnp.transpose(s.reshape(b, n, heads, n), (0, 2, 1, 3))


def _run_group(x, mod_all, t_real, positions, sample_ctx, weights):
    (norm_g, ffn_w1, ffn_w3, ffn_w2, att_w_in, att_w_out, q_norm_g, k_norm_g, idx_k_norm_g,
     conv_w, conv_b, conv_ln_g, conv_ln_b, rw_mu, rw_w0, rw_w1, rw_w2, rw_a0, rw_a1, rw_a2, rw_g1, rw_g2,
     rw_k_k, rw_k_a, rw_r_k, rw_wr, rw_wk, rw_wv, rw_wo, rw_lnx_g, rw_lnx_b) = weights
    b, t, d = x.shape
    depth = norm_g.shape[0]
    aw = att_w_out.shape[1] - conv_w.shape[2]
    att_heads = aw // LANES
    cc = conv_w.shape[2]
    conv_width = conv_w.shape[1]
    idx_dim = idx_k_norm_g.shape[1]
    iq_w = IDX_HEADS * idx_dim
    rw_heads = d // RW_HEAD
    tables = _rope_tables(positions, LANES) + _rope_tables(positions, idx_dim)

    outs = dict(k=[], v=[], ik=[], conv=[], shift=[], wkv=[])
    h = resid_norm(x, mod_all[0], g=norm_g[0, 0], shift_idx=0, scale_idx=1, emit_x=False, h_dtype=BF16)
    for l in range(depth):
        i = l // 2
        mod = mod_all[l]
        even = l % 2 == 0
        y = ffn(h, ffn_w1, ffn_w3, ffn_w2, l, 0)
        x, h = resid_norm(x, mod, y=y, gate_idx=2, coef=HALF_STEP, g=norm_g[l, 1], shift_idx=3, scale_idx=4,
                          h_dtype=BF16 if even else F32)
        if even:
            qkvi = matmul(h, att_w_in, lead=i, n_cols=3 * aw + iq_w)
            w_small = jnp.pad(att_w_in[i][:, 3 * aw + iq_w:3 * aw + iq_w + idx_dim + IDX_HEADS],
                              ((0, 0), (0, LANES - idx_dim - IDX_HEADS)))
            small = matmul(h, w_small)
            u = matmul(h, att_w_in[i][:, 3 * aw + iq_w + idx_dim + IDX_HEADS:])
            q_bf, k_f, k_bf, v_bf, iq_bf, small, ikd_bf, glu = even_post(
                qkvi, small, u, tables, q_norm_g[i], k_norm_g[i], idx_k_norm_g[i], att_heads, idx_dim)
            if sample_ctx is None:
                att = dsa_prompt(q_bf, iq_bf, small, k_bf, v_bf, ikd_bf, idx_dim)
                hist = jnp.zeros((b, HIST_ROWS, cc), F32)
                outs["conv"].append(glu[:, t - (conv_width - 1):])
            else:
                pt = sample_ctx["page_table"]
                scores = sample_idx_scores(pt, iq_bf, small, sample_ctx["cache_idx_k"], i)
                pad_rows = lambda a: jnp.pad(a, ((0, 0), (0, LANES - t), (0, 0)))
                att = sample_attention(pt, scores, q_bf, iq_bf, small, pad_rows(ikd_bf), k_f,
                                       qkvi[:, :, 2 * aw:3 * aw], sample_ctx["cache_k"], sample_ctx["cache_v"],
                                       i, t_real, idx_dim)
                state = sample_ctx["state_conv"][i]
                hist = jnp.pad(state, ((0, 0), (HIST_ROWS - (conv_width - 1), 0), (0, 0)))
                outs["conv"].append(jnp.concatenate([state, glu[:, :t_real]], axis=1)[:, -(conv_width - 1):])
            conv_y = conv_branch(glu, hist, conv_w[i], conv_b[i], conv_ln_g[i], conv_ln_b[i])
            mixed = matmul(jnp.concatenate([att, conv_y], axis=-1), att_w_out, lead=i)
            outs["k"].append(k_f[:, :t_real].reshape(b, t_real, att_heads, LANES))
            outs["v"].append(qkvi[:, :t_real, 2 * aw:3 * aw].reshape(b, t_real, att_heads, LANES))
            outs["ik"].append(small[:, :t_real, :idx_dim])
        else:
            if sample_ctx is None:
                shift_prev = jnp.zeros((b, d), F32)
                s_init = jnp.zeros((b, rw_heads, RW_HEAD, RW_HEAD), F32)
            else:
                shift_prev = sample_ctx["state_shift"][i]
                s_init = sample_ctx["state_wkv"][i]
            xs = rw_mix(h, shift_prev, rw_mu[i])
            r = matmul(xs[0], rw_wr, lead=i)
            wl = lora(xs[1], rw_w1, rw_w2, i, "tanh")
            k = matmul(xs[2], rw_wk, lead=i)
            v = matmul(xs[3], rw_wv, lead=i)
            al = lora(xs[4], rw_a1, rw_a2, i, "none")
            g = lora(xs[5], rw_g1, rw_g2, i, "sigmoid")
            if t == t_real and t % WKV_TILE_MIN == 0:
                lw, kk, b_, kh = wkv_prep(k, wl, al, rw_w0[i], rw_a0[i], rw_k_k[i], rw_k_a[i])
                y, h_fin = wkv_chunked(r, lw, kh, v, kk, b_, _state_to_blockdiag(s_init))
                yg = wkv_post(y, r, kh, v, g, rw_r_k[i], rw_lnx_g[i], rw_lnx_b[i])
                s_out = _state_from_blockdiag(h_fin)
            else:
                yg, s_fin = wkv(r, k, v, wl, al, g, rw_w0[i], rw_a0[i], rw_k_k[i], rw_k_a[i], rw_r_k[i],
                                rw_lnx_g[i], rw_lnx_b[i], _state_to_kernel_layout(s_init), t_real)
                s_out = _state_from_kernel_layout(s_fin, rw_heads)
            mixed = matmul(yg, rw_wo, lead=i)
            outs["shift"].append(h[:, t_real - 1])
            outs["wkv"].append(s_out)
        x, h = resid_norm(x, mod, y=mixed, gate_idx=5, coef=1.0, g=norm_g[l, 2], shift_idx=6, scale_idx=7,
                          h_dtype=BF16)
        y = ffn(h, ffn_w1, ffn_w3, ffn_w2, l, 1)
        if l + 1 < depth:
            x, h = resid_norm(x, mod, y=y, gate_idx=8, coef=HALF_STEP, g=norm_g[l + 1, 0],
                              shift_idx=0, scale_idx=1, h_dtype=BF16, mod_norm=mod_all[l + 1])
        else:
            x = resid_norm(x, mod, y=y, gate_idx=8, coef=HALF_STEP)
    return x[:, :t_real], outs


def kernel(x_prompt, x_sample, cache_k, cache_v, cache_idx_k, state_conv, state_shift, state_wkv, page_table, c_prompt, c_sample, norm_g, ada_w, ada_b, ffn_w1, ffn_w3, ffn_w2, att_w_in, att_w_out, q_norm_g, k_norm_g, idx_k_norm_g, conv_w, conv_b, conv_ln_g, conv_ln_b, rw_mu, rw_w0, rw_w1, rw_w2, rw_a0, rw_a1, rw_a2, rw_g1, rw_g2, rw_k_k, rw_k_a, rw_r_k, rw_wr, rw_wk, rw_wv, rw_wo, rw_lnx_g, rw_lnx_b):
    bp, tp, d = x_prompt.shape
    bs, ts, _ = x_sample.shape
    depth = norm_g.shape[0]
    past = page_table.shape[1] * cache_k.shape[2]

    n_c = bp + bs
    c_rows = -(-n_c // 16) * 16
    c_all = jnp.pad(jnp.concatenate([c_prompt, c_sample], axis=0), ((0, c_rows - n_c), (0, 0)))
    mod = adaln_all(c_all, ada_w, ada_b).reshape(depth, c_rows, N_MOD, 1, d)
    mod_p, mod_s = mod[:, :bp], mod[:, bp:n_c]

    weights = (norm_g, ffn_w1, ffn_w3, ffn_w2, att_w_in, att_w_out, q_norm_g, k_norm_g, idx_k_norm_g,
               conv_w, conv_b, conv_ln_g, conv_ln_b, rw_mu, rw_w0, rw_w1, rw_w2, rw_a0, rw_a1, rw_a2,
               rw_g1, rw_g2, rw_k_k, rw_k_a, rw_r_k.reshape(rw_r_k.shape[0], -1), rw_wr, rw_wk, rw_wv, rw_wo,
               rw_lnx_g, rw_lnx_b)

    yp, op = _run_group(x_prompt, mod_p, tp, np.arange(tp), None, weights)
    xs_pad = jnp.pad(x_sample, ((0, 0), (0, SAMPLE_T_PAD - ts), (0, 0)))
    sample_ctx = dict(page_table=page_table, cache_k=cache_k, cache_v=cache_v, cache_idx_k=cache_idx_k,
                      state_conv=state_conv, state_shift=state_shift, state_wkv=state_wkv)
    ys, os_ = _run_group(xs_pad, mod_s, ts, past + np.arange(SAMPLE_T_PAD), sample_ctx, weights)

    st = lambda xs: jnp.stack(xs)
    return (yp, ys,
            st(op["k"]), st(op["v"]), st(op["ik"]), st(op["conv"]), st(op["shift"]), st(op["wkv"]),
            st(os_["k"]), st(os_["v"]), st(os_["ik"]), st(os_["conv"]), st(os_["shift"]), st(os_["wkv"]))
```

```python
import functools

import numpy as np
import jax
import jax.numpy as jnp
from jax import lax
from jax.experimental import pallas as pl
from jax.experimental.pallas import tpu as pltpu

F32 = jnp.float32
BF16 = jnp.bfloat16
I32 = jnp.int32

ROPE_THETA = 10000.0
NORM_EPS = 1e-6
LN_EPS = 1e-5
LNX_EPS = 64e-5
N_MOD = 9
HALF_STEP = 0.5
TOPK_MAX = 256
QBLOCK = 128
IDX_HEADS = 16

LANES = 128
SUBLANES = 8
VMEM_LIMIT_MB = 56
NEG_BIG = -1e30
INT_MIN = -(2 ** 31)

SAMPLE_T_PAD = 8
PAGES_PER_STEP = 8
IDX_PAGES_PER_STEP = 8


def _cparams(sem, vmem_mb=VMEM_LIMIT_MB):
    return pltpu.CompilerParams(dimension_semantics=sem, vmem_limit_bytes=vmem_mb * 1024 * 1024)


def _silu(x):
    return x * jax.nn.sigmoid(x)


def _dot(a, b):
    return jnp.dot(a, b, preferred_element_type=F32)


def _dot_nt(a, b):
    return lax.dot_general(a, b, (((1,), (1,)), ((), ())), preferred_element_type=F32)


def _row_tile(t, target):
    return t if t <= target else target


def _adaln_kernel(c_ref, w_ref, b_ref, o_ref):
    sc = _silu(c_ref[...]).astype(BF16)
    o_ref[...] = _dot(sc, w_ref[...].astype(BF16)) + b_ref[...]


def adaln_all(c_all, ada_w, ada_b):
    depth, d, n = ada_w.shape
    rows = c_all.shape[0]
    tn = 1024
    return pl.pallas_call(
        _adaln_kernel,
        grid=(depth, n // tn),
        in_specs=[
            pl.BlockSpec((rows, d), lambda l, j: (0, 0)),
            pl.BlockSpec((None, d, tn), lambda l, j: (l, 0, j)),
            pl.BlockSpec((None, 1, tn), lambda l, j: (l, 0, j)),
        ],
        out_specs=pl.BlockSpec((None, rows, tn), lambda l, j: (l, 0, j)),
        out_shape=jax.ShapeDtypeStruct((depth, rows, n), F32),
        compiler_params=_cparams(("arbitrary", "arbitrary")),
        name="adaln",
    )(c_all, ada_w, ada_b.reshape(depth, 1, n))


def _resid_norm_kernel(*refs, has_y, coef, emit_x, emit_h):
    refs = list(refs)
    x = refs.pop(0)[...]
    if has_y:
        y = refs.pop(0)[...]
        gate = refs.pop(0)[...]
        x = x + (coef * gate) * y
    if emit_h:
        g = refs.pop(0)[...]
        shift = refs.pop(0)[...]
        scale = refs.pop(0)[...]
    if emit_x:
        refs.pop(0)[...] = x
    if emit_h:
        h_ref = refs.pop(0)
        ms = jnp.mean(x * x, axis=-1, keepdims=True)
        h = x * lax.rsqrt(ms + NORM_EPS) * g
        h_ref[...] = (h * (1.0 + scale) + shift).astype(h_ref.dtype)


def resid_norm(x, mod, *, y=None, gate_idx=None, coef=1.0, g=None, shift_idx=None, scale_idx=None,
               emit_x=True, h_dtype=None, mod_norm=None):
    mod_norm = mod if mod_norm is None else mod_norm
    b, t, d = x.shape
    tt = _row_tile(t, 256)
    has_y = y is not None
    emit_h = h_dtype is not None
    row = pl.BlockSpec((None, tt, d), lambda bi, i: (bi, i, 0))

    def mod_spec(idx):
        return pl.BlockSpec((None, None, 1, d), lambda bi, i: (bi, idx, 0, 0))

    args, specs = [x], [row]
    if has_y:
        args += [y, mod]
        specs += [row, mod_spec(gate_idx)]
    if emit_h:
        args += [g.reshape(1, d), mod_norm, mod_norm]
        specs += [pl.BlockSpec((1, d), lambda bi, i: (0, 0)), mod_spec(shift_idx), mod_spec(scale_idx)]
    out_shape, out_specs = [], []
    if emit_x:
        out_shape.append(jax.ShapeDtypeStruct((b, t, d), F32))
        out_specs.append(row)
    if emit_h:
        out_shape.append(jax.ShapeDtypeStruct((b, t, d), h_dtype))
        out_specs.append(row)
    outs = pl.pallas_call(
        functools.partial(_resid_norm_kernel, has_y=has_y, coef=coef, emit_x=emit_x, emit_h=emit_h),
        grid=(b, t // tt),
        in_specs=specs,
        out_specs=out_specs,
        out_shape=out_shape,
        compiler_params=_cparams(("arbitrary", "arbitrary")),
        name="resid_norm",
    )(*args)
    return outs if len(outs) > 1 else outs[0]


def _ffn_kernel(h_ref, w1_ref, w3_ref, w2_ref, o_ref):
    f = pl.program_id(1)
    h = h_ref[...]
    a = _dot(h, w1_ref[...].astype(BF16))
    b = _dot(h, w3_ref[...].astype(BF16))
    z = (_silu(a) * b).astype(BF16)

    @pl.when(f == 0)
    def _():
        o_ref[...] = jnp.zeros(o_ref.shape, o_ref.dtype)

    o_ref[...] += _dot(z, w2_ref[...].astype(BF16))


def ffn(h, w1, w3, w2, layer, slot):
    b, t, d = h.shape
    m = b * t
    d_ff = w1.shape[-1]
    tm = 1024 if m >= 1024 else m
    tf = 256 if m >= 1024 else 512
    out = pl.pallas_call(
        _ffn_kernel,
        grid=(m // tm, d_ff // tf),
        in_specs=[
            pl.BlockSpec((tm, d), lambda i, f: (i, 0)),
            pl.BlockSpec((None, None, d, tf), lambda i, f: (layer, slot, 0, f)),
            pl.BlockSpec((None, None, d, tf), lambda i, f: (layer, slot, 0, f)),
            pl.BlockSpec((None, None, tf, d), lambda i, f: (layer, slot, f, 0)),
        ],
        out_specs=pl.BlockSpec((tm, d), lambda i, f: (i, 0)),
        out_shape=jax.ShapeDtypeStruct((m, d), F32),
        compiler_params=_cparams(("arbitrary", "arbitrary")),
        name="ffn",
    )(h.reshape(m, d), w1, w3, w2)
    return out.reshape(b, t, d)


def _mm_kernel(x_ref, w_ref, o_ref):
    o_ref[...] = _dot(x_ref[...], w_ref[...].astype(BF16)).astype(o_ref.dtype)


def matmul(x, w, lead=None, n_cols=None, out_dtype=F32):
    b, t, k = x.shape
    m = b * t
    n = w.shape[-1] if n_cols is None else n_cols
    tm = 1024 if m >= 1024 else m
    tn = 1024 if n % 1024 == 0 else (512 if n % 512 == 0 else n)
    if lead is None:
        w_spec = pl.BlockSpec((k, tn), lambda i, j: (0, j))
    else:
        w_spec = pl.BlockSpec((None, k, tn), lambda i, j: (lead, 0, j))
    out = pl.pallas_call(
        _mm_kernel,
        grid=(m // tm, n // tn),
        in_specs=[pl.BlockSpec((tm, k), lambda i, j: (i, 0)), w_spec],
        out_specs=pl.BlockSpec((tm, tn), lambda i, j: (i, j)),
        out_shape=jax.ShapeDtypeStruct((m, n), out_dtype),
        compiler_params=_cparams(("arbitrary", "arbitrary")),
        name="matmul",
    )(x.reshape(m, k), w)
    return out.reshape(b, t, n)


def _lora_kernel(x_ref, a_ref, b_ref, o_ref, *, act):
    h = _dot(x_ref[...], a_ref[...].astype(BF16))
    if act == "tanh":
        h = jnp.tanh(h)
    elif act == "sigmoid":
        h = jax.nn.sigmoid(h)
    o_ref[...] = _dot(h.astype(BF16), b_ref[...].astype(BF16))


def lora(x, a, bmat, layer, act):
    b, t, d = x.shape
    m = b * t
    r = a.shape[-1]
    n = bmat.shape[-1]
    tm = 512 if m >= 512 else m
    out = pl.pallas_call(
        functools.partial(_lora_kernel, act=act),
        grid=(m // tm,),
        in_specs=[
            pl.BlockSpec((tm, d), lambda i: (i, 0)),
            pl.BlockSpec((None, d, r), lambda i: (layer, 0, 0)),
            pl.BlockSpec((None, r, n), lambda i: (layer, 0, 0)),
        ],
        out_specs=pl.BlockSpec((tm, n), lambda i: (i, 0)),
        out_shape=jax.ShapeDtypeStruct((m, n), F32),
        compiler_params=_cparams(("arbitrary",)),
        name="lora",
    )(x.reshape(m, d), a, bmat)
    return out.reshape(b, t, n)


def _rope_tables(positions, head_dim):
    half = head_dim // 2
    inv = ROPE_THETA ** (-np.arange(half, dtype=np.float64) / half)
    ang = np.asarray(positions, np.float64)[:, None] * inv[None, :]
    cos = np.concatenate([np.cos(ang), np.cos(ang)], axis=-1)
    sin = np.concatenate([-np.sin(ang), np.sin(ang)], axis=-1)
    reps = LANES // head_dim
    return (jnp.asarray(np.tile(cos, (1, reps)), F32), jnp.asarray(np.tile(sin, (1, reps)), F32))


def _rope128(x, cos, sin):
    return x * cos + pltpu.roll(x, 64, 1) * sin


def _rope64(x, cos, sin, lane):
    first_half = (lane % 64) < 32
    partner = jnp.where(first_half, pltpu.roll(x, 96, 1), pltpu.roll(x, 32, 1))
    return x * cos + partner * sin


def _even_post_kernel(q_ref, k_ref, v_ref, iq_ref, small_ref, u_ref,
                      c128_ref, s128_ref, c64_ref, s64_ref, qg_ref, kg_ref, ikg_ref,
                      qo_ref, kf_ref, kb_ref, vb_ref, iqo_ref, smallo_ref, ikd_ref, uo_ref,
                      *, att_heads, idx_dim, idx_heads):
    c128, s128 = c128_ref[...], s128_ref[...]
    c64, s64 = c64_ref[...], s64_ref[...]
    tt = c128.shape[0]
    lane = lax.broadcasted_iota(I32, (tt, LANES), 1)
    qg, kg = qg_ref[...], kg_ref[...]
    for h in range(att_heads):
        sl = slice(h * LANES, (h + 1) * LANES)
        q = q_ref[:, sl]
        q = q * lax.rsqrt(jnp.mean(q * q, axis=-1, keepdims=True) + NORM_EPS) * qg
        qo_ref[:, sl] = _rope128(q, c128, s128).astype(BF16)
        k = k_ref[:, sl]
        k = k * lax.rsqrt(jnp.mean(k * k, axis=-1, keepdims=True) + NORM_EPS) * kg
        k = _rope128(k, c128, s128)
        kf_ref[:, sl] = k
        kb_ref[:, sl] = k.astype(BF16)
        vb_ref[:, sl] = v_ref[:, sl].astype(BF16)
    for p in range(idx_heads * idx_dim // LANES):
        sl = slice(p * LANES, (p + 1) * LANES)
        iqo_ref[:, sl] = _rope64(iq_ref[:, sl], c64, s64, lane).astype(BF16)
    small = small_ref[...]
    is_ik = lane < idx_dim
    ik = jnp.where(is_ik, small, 0.0)
    ms = jnp.sum(ik * ik, axis=-1, keepdims=True) * (1.0 / idx_dim)
    ik = _rope64(ik * lax.rsqrt(ms + NORM_EPS) * ikg_ref[...], c64, s64, lane)
    ik = jnp.where(is_ik, ik, 0.0)
    smallo_ref[...] = jnp.where(is_ik, ik, small * (idx_heads ** -0.5))
    ikd_ref[...] = (ik + pltpu.roll(ik, 64, 1)).astype(BF16)
    cc = u_ref.shape[-1] // 2
    uo_ref[...] = u_ref[:, :cc] * jax.nn.sigmoid(u_ref[:, cc:])


def even_post(qkvi, small, u, tables, q_g, k_g, ik_g, att_heads, idx_dim):
    b, t, _ = qkvi.shape
    aw = att_heads * LANES
    iw = IDX_HEADS * idx_dim
    cc = u.shape[-1] // 2
    tt = _row_tile(t, 256)
    c128, s128, c64, s64 = tables

    def row(width, col=0):
        return pl.BlockSpec((None, tt, width), lambda bi, i: (bi, i, col))

    tab = pl.BlockSpec((tt, LANES), lambda bi, i: (i, 0))
    vec = pl.BlockSpec((1, LANES), lambda bi, i: (0, 0))
    ikg_pad = jnp.zeros((1, LANES), F32).at[0, :idx_dim].set(ik_g)
    assert aw == iw, "q/k/v/indexer-q column groups are addressed as equal-width blocks"
    return pl.pallas_call(
        functools.partial(_even_post_kernel, att_heads=att_heads, idx_dim=idx_dim, idx_heads=IDX_HEADS),
        grid=(b, t // tt),
        in_specs=[row(aw, 0), row(aw, 1), row(aw, 2), row(iw, 3), row(LANES), row(2 * cc),
                  tab, tab, tab, tab, vec, vec, vec],
        out_specs=[row(aw), row(aw), row(aw), row(aw), row(iw), row(LANES), row(LANES), row(cc)],
        out_shape=[
            jax.ShapeDtypeStruct((b, t, aw), BF16),
            jax.ShapeDtypeStruct((b, t, aw), F32),
            jax.ShapeDtypeStruct((b, t, aw), BF16),
            jax.ShapeDtypeStruct((b, t, aw), BF16),
            jax.ShapeDtypeStruct((b, t, iw), BF16),
            jax.ShapeDtypeStruct((b, t, LANES), F32),
            jax.ShapeDtypeStruct((b, t, LANES), BF16),
            jax.ShapeDtypeStruct((b, t, cc), F32),
        ],
        compiler_params=_cparams(("arbitrary", "arbitrary")),
        name="even_post",
    )(qkvi, qkvi, qkvi, qkvi, small, u, c128, s128, c64, s64,
      q_g.reshape(1, LANES), k_g.reshape(1, LANES), ikg_pad)


def _order_key(score):
    bits = pltpu.bitcast(score, I32)
    return jnp.where(bits < 0, bits ^ 0x7FFFFFFF, bits)


def _kth_largest_key(count_ge, top_k, shape):
    tau = jnp.where(count_ge(jnp.zeros(shape, I32)) >= top_k, 0, INT_MIN).astype(I32)

    def body(i, tau):
        cand = tau | jnp.left_shift(jnp.int32(1), 30 - i)
        return jnp.where(count_ge(cand) >= top_k, cand, tau)

    return lax.fori_loop(0, 31, body, tau)


def _idx_head_lhs(iq, h, lane):
    slab = iq[:, (h // 2) * LANES:(h // 2 + 1) * LANES]
    keep = (lane < 64) if h % 2 == 0 else (lane >= 64)
    return jnp.where(keep, slab, jnp.zeros_like(slab))


def _idx_weight(small, h, lane, idx_dim):
    return jnp.sum(jnp.where(lane == idx_dim + h, small, 0.0), axis=-1, keepdims=True)


KEY_CHUNK = 512


def _dsa_prompt_kernel(q_ref, iq_ref, small_t_ref, k_ref, v_t_ref, ikd_ref, o_ref, lhs_sc, key_sc,
                       *, top_k, att_heads, idx_dim):
    n = pl.program_id(1)
    qb = q_ref.shape[0]
    n_chunks = ((n + 1) * qb + KEY_CHUNK - 1) // KEY_CHUNK
    lane = lax.broadcasted_iota(I32, (qb, LANES), 1)
    iq = iq_ref[...]
    for h in range(IDX_HEADS):
        lhs_sc[h // 2, (h % 2) * qb:(h % 2 + 1) * qb, :] = _idx_head_lhs(iq, h, lane)
    small_t = small_t_ref[...]
    weights = [small_t[idx_dim + h:idx_dim + h + 1, :] * (idx_dim ** -0.5) for h in range(IDX_HEADS)]
    qpos = lax.broadcasted_iota(I32, (KEY_CHUNK, qb), 1) + n * qb
    krow = lax.broadcasted_iota(I32, (KEY_CHUNK, qb), 0)

    def chunk_rows(c):
        return pl.ds(pl.multiple_of(c * KEY_CHUNK, KEY_CHUNK), KEY_CHUNK)

    def score_chunk(c, carry):
        ikd = ikd_ref[chunk_rows(c), :]
        score = jnp.zeros((KEY_CHUNK, qb), F32)
        for pair in range(IDX_HEADS // 2):
            d = jnp.maximum(_dot_nt(ikd, lhs_sc[pair]), 0.0)
            score = score + weights[2 * pair] * d[:, :qb] + weights[2 * pair + 1] * d[:, qb:]
        admissible = krow + c * KEY_CHUNK <= qpos
        key_sc[c] = jnp.where(admissible, _order_key(score), INT_MIN)
        return carry

    lax.fori_loop(0, n_chunks, score_chunk, 0)

    def count_ge(cand):
        return lax.fori_loop(
            0, n_chunks,
            lambda c, acc: acc + jnp.sum((key_sc[c] >= cand).astype(I32), axis=0, keepdims=True),
            jnp.zeros((1, qb), I32))

    tau = jnp.maximum(_kth_largest_key(count_ge, top_k, (1, qb)), INT_MIN + 1)
    scale = LANES ** -0.5
    heads = [slice(h * LANES, (h + 1) * LANES) for h in range(att_heads)]

    def attend_chunk(c, carry):
        sel = key_sc[c] >= tau
        out = []
        for sl, (m_old, l_old, acc) in zip(heads, carry):
            s = jnp.where(sel, _dot_nt(k_ref[chunk_rows(c), sl], q_ref[:, sl]) * scale, NEG_BIG)
            m_new = jnp.maximum(m_old, jnp.max(s, axis=0, keepdims=True))
            alpha = jnp.exp(m_old - m_new)
            p = jnp.exp(s - m_new)
            l_new = alpha * l_old + jnp.sum(p, axis=0, keepdims=True)
            out.append((m_new, l_new, alpha * acc + _dot(v_t_ref[c, sl, :], p.astype(BF16))))
        return tuple(out)

    init = tuple((jnp.full((1, qb), NEG_BIG, F32), jnp.zeros((1, qb), F32), jnp.zeros((LANES, qb), F32))
                 for _ in heads)
    for sl, (_, l_fin, acc) in zip(heads, lax.fori_loop(0, n_chunks, attend_chunk, init)):
        o_ref[:, sl] = (acc / l_fin).T.astype(o_ref.dtype)


def dsa_prompt(q_bf, iq_bf, small, k_bf, v_bf, ikd_bf, idx_dim):
    b, t, aw = q_bf.shape
    assert t % KEY_CHUNK == 0 and KEY_CHUNK % QBLOCK == 0
    top_k = min(TOPK_MAX, t // 4)
    n_kc = t // KEY_CHUNK
    small_t = jnp.swapaxes(small, 1, 2)
    v_t = jnp.swapaxes(v_bf.reshape(b, n_kc, KEY_CHUNK, aw), 2, 3)
    qrow = lambda w: pl.BlockSpec((None, QBLOCK, w), lambda bi, n: (bi, n, 0))
    full = lambda w: pl.BlockSpec((None, t, w), lambda bi, n: (bi, 0, 0))
    return pl.pallas_call(
        functools.partial(_dsa_prompt_kernel, top_k=top_k, att_heads=aw // LANES, idx_dim=idx_dim),
        grid=(b, t // QBLOCK),
        in_specs=[qrow(aw), qrow(iq_bf.shape[-1]),
                  pl.BlockSpec((None, LANES, QBLOCK), lambda bi, n: (bi, 0, n)),
                  full(aw),
                  pl.BlockSpec((None, n_kc, aw, KEY_CHUNK), lambda bi, n: (bi, 0, 0, 0)),
                  full(LANES)],
        out_specs=qrow(aw),
        out_shape=jax.ShapeDtypeStruct((b, t, aw), BF16),
        scratch_shapes=[pltpu.VMEM((IDX_HEADS // 2, 2 * QBLOCK, LANES), BF16),
                        pltpu.VMEM((n_kc, KEY_CHUNK, QBLOCK), I32)],
        compiler_params=_cparams(("arbitrary", "arbitrary")),
        name="dsa_prompt",
    )(q_bf, iq_bf, small_t, k_bf, v_t, ikd_bf)


def _idx_lhs_all_heads(iq, lane):
    iq = iq.astype(F32)
    return jnp.concatenate([_idx_head_lhs(iq, h, lane) for h in range(IDX_HEADS)], axis=0).astype(BF16)


def _idx_scores(lhs, small, keys_dup, lane, idx_dim):
    d = _dot_nt(lhs, keys_dup) * (idx_dim ** -0.5)
    score = jnp.zeros((SAMPLE_T_PAD, keys_dup.shape[0]), F32)
    for h in range(IDX_HEADS):
        dh = d[h * SAMPLE_T_PAD:(h + 1) * SAMPLE_T_PAD, :]
        score = score + _idx_weight(small, h, lane, idx_dim) * jnp.maximum(dh, 0.0)
    return score


def _sample_idx_kernel(pt_ref, iq_ref, small_ref, *rest, idx_dim):
    page_refs, o_ref = rest[:-1], rest[-1]
    lane = lax.broadcasted_iota(I32, (SAMPLE_T_PAD, LANES), 1)
    keys = jnp.concatenate([r[...] for r in page_refs], axis=0)
    keys_dup = jnp.concatenate([keys, keys], axis=-1).astype(BF16)
    lhs = _idx_lhs_all_heads(iq_ref[...], lane)
    o_ref[...] = _idx_scores(lhs, small_ref[...], keys_dup, lane, idx_dim)


def sample_idx_scores(page_table, iq_bf, small, cache_idx_k, layer):
    b, n_pages = page_table.shape
    page, idx_dim = cache_idx_k.shape[-2:]
    g = IDX_PAGES_PER_STEP if n_pages % IDX_PAGES_PER_STEP == 0 else 1

    def page_spec(j):
        return pl.BlockSpec((None, None, page, idx_dim),
                            lambda bi, s, pt: (layer, pt[bi * n_pages + s * g + j], 0, 0))

    row = lambda w: pl.BlockSpec((None, SAMPLE_T_PAD, w), lambda bi, s, pt: (bi, 0, 0))
    grid_spec = pltpu.PrefetchScalarGridSpec(
        num_scalar_prefetch=1,
        grid=(b, n_pages // g),
        in_specs=[row(iq_bf.shape[-1]), row(LANES)] + [page_spec(j) for j in range(g)],
        out_specs=pl.BlockSpec((None, SAMPLE_T_PAD, g * page), lambda bi, s, pt: (bi, 0, s)),
    )
    return pl.pallas_call(
        functools.partial(_sample_idx_kernel, idx_dim=idx_dim),
        grid_spec=grid_spec,
        out_shape=jax.ShapeDtypeStruct((b, SAMPLE_T_PAD, n_pages * page), F32),
        compiler_params=_cparams(("arbitrary", "arbitrary")),
        name="sample_idx",
    )(page_table.reshape(-1), iq_bf, small, *([cache_idx_k] * g))


def _sample_attn_kernel(pt_ref, sc_all_ref, sc_ref, q_ref, iq_ref, small_ref, ikn_ref, kn_ref, vn_ref,
                        *rest, top_k, n_new, att_heads, idx_dim, n_groups):
    g = (len(rest) - 8) // 2
    k_pages, v_pages = rest[:g], rest[g:2 * g]
    o_ref, qrows_ref, erow_ref, tau_ref, keyn_ref, m_ref, l_ref, acc_ref = rest[2 * g:]
    s_idx = pl.program_id(1)
    page = k_pages[0].shape[0]
    scale = LANES ** -0.5
    assert att_heads == SUBLANES and att_heads * SAMPLE_T_PAD <= LANES and page <= LANES

    @pl.when(s_idx == 0)
    def _():
        lane = lax.broadcasted_iota(I32, (SAMPLE_T_PAD, LANES), 1)
        tok = lax.broadcasted_iota(I32, (SAMPLE_T_PAD, LANES), 0)
        lhs = _idx_lhs_all_heads(iq_ref[...], lane)
        sc_new = _idx_scores(lhs, small_ref[...], ikn_ref[...], lane, idx_dim)
        new_ok = (lane <= tok) & (lane < n_new)
        key_new = jnp.where(new_ok, _order_key(sc_new), INT_MIN)
        key_past = _order_key(sc_all_ref[...])

        def count_ge(cand):
            return (jnp.sum((key_past >= cand).astype(I32), axis=-1, keepdims=True)
                    + jnp.sum((key_new >= cand).astype(I32), axis=-1, keepdims=True))

        tau = _kth_largest_key(count_ge, top_k, (SAMPLE_T_PAD, 1))
        tau_ref[...] = jnp.broadcast_to(tau, (SAMPLE_T_PAD, LANES))
        keyn_ref[...] = key_new
        q = q_ref[...].astype(F32)
        q_rows = [q[:, h * LANES:(h + 1) * LANES] for h in range(att_heads)]
        q_rows.append(jnp.zeros((LANES - att_heads * SAMPLE_T_PAD, LANES), F32))
        qrows_ref[...] = jnp.concatenate(q_rows, axis=0).astype(BF16)
        r_tok = lax.broadcasted_iota(I32, erow_ref.shape, 0) // att_heads
        c_tok = lax.broadcasted_iota(I32, erow_ref.shape, 1)
        erow_ref[...] = jnp.where(r_tok == c_tok, 1.0, 0.0).astype(BF16)
        m_ref[...] = jnp.full(m_ref.shape, NEG_BIG, F32)
        l_ref[...] = jnp.zeros(l_ref.shape, F32)
        acc_ref[...] = jnp.zeros(acc_ref.shape, F32)

    tau = tau_ref[:, 0:1]
    rep = jnp.where(lax.broadcasted_iota(I32, (LANES, LANES), 0)
                    == lax.broadcasted_iota(I32, (LANES, LANES), 1) % SAMPLE_T_PAD, 1.0, 0.0).astype(BF16)
    head_match8 = (lax.broadcasted_iota(I32, (SUBLANES, LANES), 0)
                   == lax.broadcasted_iota(I32, (SUBLANES, LANES), 1) // SAMPLE_T_PAD)

    def masked_scores(sel8, keys):
        rows = keys.shape[0]
        n_tok = rows // att_heads
        sel_pad = jnp.concatenate([sel8.astype(F32), jnp.zeros((LANES - SAMPLE_T_PAD, LANES), F32)], axis=0)
        sel_rep = _dot(sel_pad.T.astype(BF16), rep)
        sel_rows = _dot(erow_ref[0:rows, :], sel_rep.astype(BF16))
        head_match = jnp.broadcast_to(head_match8[None], (n_tok, SUBLANES, LANES)).reshape(rows, LANES)
        mask = (sel_rows > 0.5) & head_match
        return jnp.where(mask, _dot_nt(keys, qrows_ref[...]) * scale, NEG_BIG)

    def attend(groups):
        scores = [masked_scores(sel8, keys) for sel8, keys, _ in groups]
        m_old = m_ref[0:1, :]
        m_new = m_old
        for s in scores:
            m_new = jnp.maximum(m_new, jnp.max(s, axis=0, keepdims=True))
        alpha = jnp.exp(m_old - m_new)
        l_new = alpha * l_ref[0:1, :]
        acc = alpha * acc_ref[...]
        for s, (_, _, vals) in zip(scores, groups):
            p = jnp.exp(s - m_new)
            l_new = l_new + jnp.sum(p, axis=0, keepdims=True)
            acc = acc + lax.dot_general(vals, p.astype(BF16), (((0,), (0,)), ((), ())),
                                        preferred_element_type=F32)
        l_ref[...] = jnp.broadcast_to(l_new, l_ref.shape)
        acc_ref[...] = acc
        m_ref[...] = jnp.broadcast_to(m_new, m_ref.shape)

    def as_rows(ref):
        return ref[...].reshape(ref.shape[0] * att_heads, LANES).astype(BF16)

    attend([(_order_key(sc_ref[:, j * page:(j + 1) * page]) >= tau, as_rows(k_pages[j]), as_rows(v_pages[j]))
            for j in range(g)])

    @pl.when(s_idx == n_groups - 1)
    def _():
        attend([(keyn_ref[...] >= tau, as_rows(kn_ref), as_rows(vn_ref))])
        out = (acc_ref[...] / l_ref[0:1, :]).T
        for h in range(att_heads):
            o_ref[:, h * LANES:(h + 1) * LANES] = (
                out[h * SAMPLE_T_PAD:(h + 1) * SAMPLE_T_PAD, :].astype(o_ref.dtype))


def sample_attention(page_table, scores, q_bf, iq_bf, small, ikn_pad, k_new, v_new,
                     cache_k, cache_v, layer, n_new, idx_dim):
    b, n_pages = page_table.shape
    page = cache_k.shape[2]
    aw = q_bf.shape[-1]
    att_heads = aw // LANES
    past = n_pages * page
    top_k = min(TOPK_MAX, (past + n_new) // 4)
    g = PAGES_PER_STEP if n_pages % PAGES_PER_STEP == 0 else 1
    n_groups = n_pages // g

    def page_spec(j):
        return pl.BlockSpec((None, None, page, att_heads, LANES),
                            lambda bi, s, pt: (layer, pt[bi * n_pages + s * g + j], 0, 0, 0))

    def per_seq(r, w):
        return pl.BlockSpec((None, r, w), lambda bi, s, pt: (bi, 0, 0))

    new_rows = pl.BlockSpec((None, SAMPLE_T_PAD, att_heads, LANES), lambda bi, s, pt: (bi, 0, 0, 0))
    grid_spec = pltpu.PrefetchScalarGridSpec(
        num_scalar_prefetch=1,
        grid=(b, n_groups),
        in_specs=[per_seq(SAMPLE_T_PAD, past),
                  pl.BlockSpec((None, SAMPLE_T_PAD, g * page), lambda bi, s, pt: (bi, 0, s)),
                  per_seq(SAMPLE_T_PAD, aw), per_seq(SAMPLE_T_PAD, iq_bf.shape[-1]),
                  per_seq(SAMPLE_T_PAD, LANES),
                  per_seq(LANES, LANES), new_rows, new_rows]
                 + [page_spec(j) for j in range(g)] * 2,
        out_specs=per_seq(SAMPLE_T_PAD, aw),
        scratch_shapes=[
            pltpu.VMEM((LANES, LANES), BF16),
            pltpu.VMEM((page * att_heads, LANES), BF16),
            pltpu.VMEM((SAMPLE_T_PAD, LANES), I32),
            pltpu.VMEM((SAMPLE_T_PAD, LANES), I32),
            pltpu.VMEM((SUBLANES, LANES), F32),
            pltpu.VMEM((SUBLANES, LANES), F32),
            pltpu.VMEM((LANES, LANES), F32),
        ],
    )
    return pl.pallas_call(
        functools.partial(_sample_attn_kernel, top_k=top_k, n_new=n_new, att_heads=att_heads,
                          idx_dim=idx_dim, n_groups=n_groups),
        grid_spec=grid_spec,
        out_shape=jax.ShapeDtypeStruct((b, SAMPLE_T_PAD, aw), BF16),
        compiler_params=_cparams(("arbitrary", "arbitrary")),
        name="sample_attn",
    )(page_table.reshape(-1), scores, scores, q_bf, iq_bf, small, ikn_pad,
      k_new.reshape(b, SAMPLE_T_PAD, att_heads, LANES), v_new.reshape(b, SAMPLE_T_PAD, att_heads, LANES),
      *([cache_k] * g), *([cache_v] * g))


HIST_ROWS = 32
CONV_ROWS = 32


def _conv_kernel(*refs, width, multi_tile):
    if multi_tile:
        u_ref, prev_ref, hist_ref, cw_ref, cb_ref, g_ref, b_ref, o_ref, buf = refs
    else:
        u_ref, hist_ref, cw_ref, cb_ref, g_ref, b_ref, o_ref, buf = refs
    tt = u_ref.shape[0]
    buf[HIST_ROWS:HIST_ROWS + tt, :] = u_ref[...]
    if multi_tile:
        buf[0:HIST_ROWS, :] = jnp.where(pl.program_id(1) == 0, hist_ref[...], prev_ref[...])
    else:
        buf[0:HIST_ROWS, :] = hist_ref[...]
    first = HIST_ROWS - (width - 1)
    rows = min(CONV_ROWS, tt)
    cb, g, b = cb_ref[...], g_ref[...], b_ref[...]
    for r0 in range(0, tt, rows):
        acc = jnp.zeros((rows, u_ref.shape[1]), F32)
        for j in range(width):
            acc = acc + buf[r0 + first + j:r0 + first + j + rows, :] * cw_ref[j:j + 1, :]
        y = acc + cb
        yc = y - jnp.mean(y, axis=-1, keepdims=True)
        var = jnp.mean(yc * yc, axis=-1, keepdims=True)
        y = yc * lax.rsqrt(var + LN_EPS) * g + b
        o_ref[r0:r0 + rows, :] = _silu(y).astype(o_ref.dtype)


def conv_branch(u, hist, cw, cb, ln_g, ln_b):
    b, t, c = u.shape
    width = cw.shape[0]
    tt = _row_tile(t, 256)
    multi_tile = t > tt
    row = pl.BlockSpec((None, tt, c), lambda bi, i: (bi, i, 0))
    vec = pl.BlockSpec((1, c), lambda bi, i: (0, 0))
    specs, args = [row], [u]
    if multi_tile:
        per = tt // HIST_ROWS
        specs.append(pl.BlockSpec((None, HIST_ROWS, c), lambda bi, i: (bi, jnp.maximum(i * per - 1, 0), 0)))
        args.append(u)
    specs += [pl.BlockSpec((None, HIST_ROWS, c), lambda bi, i: (bi, 0, 0)),
              pl.BlockSpec((width, c), lambda bi, i: (0, 0)), vec, vec, vec]
    args += [hist, cw, cb.reshape(1, c), ln_g.reshape(1, c), ln_b.reshape(1, c)]
    return pl.pallas_call(
        functools.partial(_conv_kernel, width=width, multi_tile=multi_tile),
        grid=(b, t // tt),
        in_specs=specs,
        out_specs=row,
        out_shape=jax.ShapeDtypeStruct((b, t, c), BF16),
        scratch_shapes=[pltpu.VMEM((HIST_ROWS + tt, c), F32)],
        compiler_params=_cparams(("arbitrary", "arbitrary")),
        name="conv_branch",
    )(*args)


def _rw_mix_kernel(h_ref, prev_ref, shift_ref, mu_ref, o_ref, buf):
    tt = h_ref.shape[0]
    h = h_ref[...]
    buf[SUBLANES:SUBLANES + tt, :] = h
    buf[SUBLANES - 1:SUBLANES, :] = jnp.where(pl.program_id(1) == 0, shift_ref[...],
                                              prev_ref[SUBLANES - 1:SUBLANES, :])
    xx = buf[SUBLANES - 1:SUBLANES - 1 + tt, :] - h
    for j in range(o_ref.shape[0]):
        o_ref[j] = (h + xx * mu_ref[j:j + 1, :]).astype(o_ref.dtype)


def rw_mix(h, shift_prev, mu):
    b, t, d = h.shape
    n_mix = mu.shape[0]
    tt = _row_tile(t, 256)
    per = tt // SUBLANES
    return pl.pallas_call(
        _rw_mix_kernel,
        grid=(b, t // tt),
        in_specs=[
            pl.BlockSpec((None, tt, d), lambda bi, i: (bi, i, 0)),
            pl.BlockSpec((None, SUBLANES, d), lambda bi, i: (bi, jnp.maximum(i * per - 1, 0), 0)),
            pl.BlockSpec((None, 1, d), lambda bi, i: (bi, 0, 0)),
            pl.BlockSpec((n_mix, d), lambda bi, i: (0, 0)),
        ],
        out_specs=pl.BlockSpec((n_mix, None, tt, d), lambda bi, i: (0, bi, i, 0)),
        out_shape=jax.ShapeDtypeStruct((n_mix, b, t, d), BF16),
        scratch_shapes=[pltpu.VMEM((SUBLANES + tt, d), F32)],
        compiler_params=_cparams(("arbitrary", "arbitrary")),
        name="rw_mix",
    )(h, h, shift_prev.reshape(b, 1, d), mu)


RW_HEAD = 64


def _seg_sum(x, ones_blk):
    outs = []
    for s in range(x.shape[1] // LANES):
        xs = x[:, s * LANES:(s + 1) * LANES]
        hi = xs.astype(BF16)
        lo = (xs - hi.astype(F32)).astype(BF16)
        outs.append(_dot(hi, ones_blk) + _dot(lo, ones_blk))
    return jnp.concatenate(outs, axis=-1)


def _wkv_kernel(r_ref, k_ref, v_ref, wl_ref, al_ref, g_ref,
                w0_ref, a0_ref, kk_ref, ka_ref, rk_ref, lg_ref, lb_ref, s0_ref,
                o_ref, s_ref, dec_sc, kk_sc, b_sc, kh_sc, y_sc, *, n_steps):
    i = pl.program_id(1)
    tc, d = r_ref.shape
    rb = lax.broadcasted_iota(I32, (LANES, LANES), 0) // RW_HEAD
    cb = lax.broadcasted_iota(I32, (LANES, LANES), 1) // RW_HEAD
    ones_blk = jnp.where(rb == cb, 1.0, 0.0).astype(BF16)
    eye2 = (lax.broadcasted_iota(I32, (RW_HEAD, LANES), 0)
            == lax.broadcasted_iota(I32, (RW_HEAD, LANES), 1) % RW_HEAD)

    @pl.when(i == 0)
    def _():
        s_ref[...] = s0_ref[...]

    k = k_ref[...]
    z = -(w0_ref[...] + wl_ref[...])
    softplus = jnp.maximum(z, 0.0) + jnp.log(1.0 + jnp.exp(-jnp.abs(z)))
    dec_sc[...] = jnp.exp(-jnp.exp(-softplus - 0.5))
    a = jax.nn.sigmoid(a0_ref[...] + al_ref[...])
    kk = k * kk_ref[...]
    kk = kk * lax.rsqrt(jnp.maximum(_seg_sum(kk * kk, ones_blk), 1e-24))
    kk_sc[...] = kk
    b_sc[...] = kk * a
    kh_sc[...] = k * (1.0 + (a - 1.0) * ka_ref[...])
    assert -(-n_steps // SUBLANES) * SUBLANES == tc, "the recurrence blocks must cover the whole time tile"

    steps_per_block = min(SUBLANES, n_steps)

    slabs = [slice(s * LANES, (s + 1) * LANES) for s in range(d // LANES)]
    head_sel = (lax.broadcasted_iota(I32, (SUBLANES, LANES), 0)
                == lax.broadcasted_iota(I32, (SUBLANES, LANES), 1) // RW_HEAD).astype(BF16)

    def block(blk, carry):
        rows = pl.ds(pl.multiple_of(blk * SUBLANES, SUBLANES), SUBLANES)
        st = [s_ref[:, sl] for sl in slabs]
        y_rows = [[] for _ in slabs]
        for j in range(steps_per_block):
            sa, vb = [], []
            for s, sl in enumerate(slabs):
                lhs = jnp.concatenate([st[s] * kk_sc[rows, sl][j:j + 1],
                                       jnp.where(eye2, v_ref[rows, sl][j:j + 1], 0.0)], axis=0)
                both = _dot(lhs.astype(BF16), ones_blk)
                sa.append(both[:RW_HEAD])
                vb.append(both[RW_HEAD:])
            for s, sl in enumerate(slabs):
                st[s] = (st[s] * dec_sc[rows, sl][j:j + 1] - sa[s] * b_sc[rows, sl][j:j + 1]
                         + vb[s] * kh_sc[rows, sl][j:j + 1])
            for s, sl in enumerate(slabs):
                yh = _dot_nt(head_sel, (st[s] * r_ref[rows, sl][j:j + 1]).astype(BF16))
                y_rows[s].append(jnp.concatenate([yh[0:1], yh[1:2]], axis=-1))
        for s, sl in enumerate(slabs):
            s_ref[:, sl] = st[s]
            pad = [jnp.zeros((1, LANES), F32)] * (SUBLANES - steps_per_block)
            y_sc[rows, sl] = jnp.concatenate(y_rows[s] + pad, axis=0)
        return carry

    lax.fori_loop(0, -(-n_steps // SUBLANES), block, 0)

    y = y_sc[...]
    inv_n = 1.0 / RW_HEAD
    yc = y - _seg_sum(y, ones_blk) * inv_n
    var = _seg_sum(yc * yc, ones_blk) * inv_n
    y = yc * lax.rsqrt(var + LNX_EPS) * lg_ref[...] + lb_ref[...]
    r = r_ref[...]
    y = y + _seg_sum(r * kh_sc[...] * rk_ref[...], ones_blk) * v_ref[...]
    o_ref[...] = (y * g_ref[...]).astype(o_ref.dtype)


def wkv(r, k, v, wl, al, g, w0, a0, k_k, k_a, r_k, lnx_g, lnx_b, s0, n_steps):
    b, t, d = r.shape
    tc = _row_tile(t, 128)
    steps = tc if t > tc else n_steps
    row = pl.BlockSpec((None, tc, d), lambda bi, i: (bi, i, 0))
    vec = pl.BlockSpec((1, d), lambda bi, i: (0, 0))
    st = pl.BlockSpec((None, RW_HEAD, d), lambda bi, i: (bi, 0, 0))
    vecs = [x.reshape(1, d) for x in (w0, a0, k_k, k_a, r_k, lnx_g, lnx_b)]
    return pl.pallas_call(
        functools.partial(_wkv_kernel, n_steps=steps),
        grid=(b, t // tc),
        in_specs=[row] * 6 + [vec] * 7 + [st],
        out_specs=[row, st],
        out_shape=[jax.ShapeDtypeStruct((b, t, d), BF16), jax.ShapeDtypeStruct((b, RW_HEAD, d), F32)],
        scratch_shapes=[pltpu.VMEM((tc, d), F32)] * 5,
        compiler_params=_cparams(("arbitrary", "arbitrary")),
        name="wkv",
    )(r, k, v, wl, al, g, *vecs, s0)


WKV_CHUNK = 64
WKV_TILE = 1024
WKV_TILE_MIN = 512


def _split3(x):
    h1 = x.astype(BF16)
    r1 = x - h1.astype(F32)
    h2 = r1.astype(BF16)
    return h1, h2, (r1 - h2.astype(F32)).astype(BF16)


def _wkv_prep_kernel(k_ref, wl_ref, al_ref, w0_ref, a0_ref, kkp_ref, ka_ref, lw_ref, kk_ref, b_ref, kh_ref):
    rb = lax.broadcasted_iota(I32, (LANES, LANES), 0) // RW_HEAD
    cb = lax.broadcasted_iota(I32, (LANES, LANES), 1) // RW_HEAD
    ones_blk = jnp.where(rb == cb, 1.0, 0.0).astype(BF16)
    k = k_ref[...]
    z = -(w0_ref[...] + wl_ref[...])
    softplus = jnp.maximum(z, 0.0) + jnp.log(1.0 + jnp.exp(-jnp.abs(z)))
    lw_ref[...] = -jnp.exp(-softplus - 0.5)
    a = jax.nn.sigmoid(a0_ref[...] + al_ref[...])
    kk = k * kkp_ref[...]
    kk = kk * lax.rsqrt(jnp.maximum(_seg_sum(kk * kk, ones_blk), 1e-24))
    kk_ref[...] = kk
    b_ref[...] = kk * a
    kh_ref[...] = k * (1.0 + (a - 1.0) * ka_ref[...])


def wkv_prep(k, wl, al, w0, a0, k_k, k_a):
    b, t, d = k.shape
    tt = _row_tile(t, 256)
    row = pl.BlockSpec((None, tt, d), lambda bi, i: (bi, i, 0))
    vec = pl.BlockSpec((1, d), lambda bi, i: (0, 0))
    return pl.pallas_call(
        _wkv_prep_kernel,
        grid=(b, t // tt),
        in_specs=[row] * 3 + [vec] * 4,
        out_specs=[row] * 4,
        out_shape=[jax.ShapeDtypeStruct((b, t, d), F32)] * 4,
        compiler_params=_cparams(("arbitrary", "arbitrary")),
        name="wkv_prep",
    )(k, wl, al, *[x.reshape(1, d) for x in (w0, a0, k_k, k_a)])


def _wkv_chunk_kernel(r_ref, lw_ref, kh_ref, v_ref, kk_ref, b_ref, lwt_ref, kht_ref, bt_ref, h0_ref,
                      y_ref, h_ref, *, n_chunks):
    c_ = WKV_CHUNK

    @pl.when(pl.program_id(2) == 0)
    def _():
        h_ref[...] = h0_ref[...]

    row = lax.broadcasted_iota(I32, (c_, c_), 0)
    col = lax.broadcasted_iota(I32, (c_, c_), 1)
    tri_incl = col <= row
    tri_strict = col < row
    l_incl = jnp.where(tri_incl, 1.0, 0.0).astype(BF16)
    u_incl = jnp.where(row <= col, 1.0, 0.0).astype(BF16)
    lane_head = lax.broadcasted_iota(I32, (1, LANES), 1) // RW_HEAD
    row_head = lax.broadcasted_iota(I32, (LANES, 1), 0) // RW_HEAD
    n_heads = LANES // RW_HEAD
    chunks = range(n_chunks)
    items = [(c, hh) for c in chunks for hh in range(n_heads)]
    ch = []
    for c in chunks:
        rs = slice(c * c_, (c + 1) * c_)
        lw, r, kh, v, kk, b = lw_ref[rs, :], r_ref[rs, :], kh_ref[rs, :], v_ref[rs, :], kk_ref[rs, :], b_ref[rs, :]
        cum = sum(_dot(l_incl, part) for part in _split3(lw))
        e_neg = jnp.exp(-cum)
        bt = b * e_neg
        kt = kh * e_neg
        cumt = sum(_dot(part, u_incl) for part in _split3(lwt_ref[c]))
        dt = jnp.exp(cumt[:, c_ - 1:c_] - cumt)
        ch.append(dict(
            kkt=kk * jnp.exp(cum - lw), rt=r * jnp.exp(cum),
            y1=jnp.concatenate([bt, kt], axis=0).astype(BF16), v_bf=v.astype(BF16),
            g_col=jnp.exp(cumt[:, c_ - 1:c_]),
            bht=bt_ref[c] * dt, kht=kht_ref[c] * dt))
    it = {}
    for c, hh in items:
        m = lane_head == hh
        kkt_h = jnp.where(m, ch[c]["kkt"], 0.0)
        rt_h = jnp.where(m, ch[c]["rt"], 0.0)
        gmat = _dot_nt(jnp.concatenate([kkt_h, rt_h], axis=0).astype(BF16), ch[c]["y1"])
        it[c, hh] = dict(
            kkt_h=kkt_h, rt_h=rt_h,
            a_b=jnp.where(tri_strict, gmat[:c_, :c_], 0.0).astype(BF16),
            a_k=jnp.where(tri_strict, gmat[:c_, c_:], 0.0).astype(BF16),
            m_b=jnp.where(tri_incl, gmat[c_:, :c_], 0.0).astype(BF16),
            m_k=jnp.where(tri_incl, gmat[c_:, c_:], 0.0).astype(BF16))
    for c, hh in items:
        d_ = it[c, hh]
        d_["x"] = jnp.concatenate([d_["kkt_h"], _dot(d_["a_k"], ch[c]["v_bf"])], axis=1)
        d_["pows"] = [d_["a_b"]]
    n_levels = c_.bit_length() - 1
    for _ in range(n_levels - 1):
        for key in items:
            pows = it[key]["pows"]
            pows.append(_dot(pows[-1], pows[-1]).astype(BF16))
    eye_c = jnp.where(row == col, 1.0, 0.0)
    for key in items:
        it[key]["t"] = eye_c - it[key]["a_b"].astype(F32)
    for level in range(1, n_levels):
        for key in items:
            d_ = it[key]
            d_["t"] = d_["t"] + _dot(d_["t"].astype(BF16), d_["pows"][level])
    hst = h_ref[...]
    for c in chunks:
        for hh in range(n_heads):
            d_ = it[c, hh]
            m = lane_head == hh
            rmask = row_head == hh
            x_bf = _dot(d_["t"].astype(BF16), d_["x"].astype(BF16)).astype(BF16)
            mbx = _dot(d_["m_b"], x_bf)
            d_["rw"] = d_["rt_h"] - mbx[:, :LANES]
            d_["y0"] = jnp.where(m, _dot(d_["m_k"], ch[c]["v_bf"]) - mbx[:, LANES:], 0.0)
            bx = _dot(jnp.where(rmask, ch[c]["bht"], 0.0).astype(BF16), x_bf)
            d_["pm"] = -bx[:, :LANES]
            d_["qm"] = jnp.where(m, _dot(jnp.where(rmask, ch[c]["kht"], 0.0).astype(BF16), ch[c]["v_bf"])
                                 - bx[:, LANES:], 0.0)
        parts = [it[c, hh] for hh in range(n_heads)]
        h_bf = hst.astype(BF16)
        y_ref[c * c_:(c + 1) * c_, :] = (_dot(sum(p_["rw"] for p_ in parts).astype(BF16), h_bf)
                                         + sum(p_["y0"] for p_ in parts))
        hst = (ch[c]["g_col"] * hst + _dot(sum(p_["pm"] for p_ in parts).astype(BF16), h_bf)
               + sum(p_["qm"] for p_ in parts))
    h_ref[...] = hst


def wkv_chunked(r, lw, kh, v, kk, b_, h0):
    bsz, t, d = r.shape
    n_pairs = d // LANES
    tile = WKV_TILE if t % WKV_TILE == 0 else WKV_TILE_MIN
    per_tile = tile // WKV_CHUNK

    def chunk_major_t(x):
        return jnp.swapaxes(x.reshape(bsz, t // WKV_CHUNK, WKV_CHUNK, d), 2, 3)

    row = pl.BlockSpec((None, tile, LANES), lambda bi, p, i: (bi, i, p))
    row_t = pl.BlockSpec((None, per_tile, LANES, WKV_CHUNK), lambda bi, p, i: (bi, i, p, 0))
    st = pl.BlockSpec((None, None, LANES, LANES), lambda bi, p, i: (bi, p, 0, 0))
    return pl.pallas_call(
        functools.partial(_wkv_chunk_kernel, n_chunks=per_tile),
        grid=(bsz, n_pairs, t // tile),
        in_specs=[row] * 6 + [row_t] * 3 + [st],
        out_specs=[row, st],
        out_shape=[jax.ShapeDtypeStruct((bsz, t, d), F32),
                   jax.ShapeDtypeStruct((bsz, n_pairs, LANES, LANES), F32)],
        compiler_params=_cparams(("arbitrary", "arbitrary", "arbitrary")),
        name="wkv_chunked",
    )(r, lw, kh, v, kk, b_, chunk_major_t(lw), chunk_major_t(kh), chunk_major_t(b_), h0)


def _wkv_post_kernel(y_ref, r_ref, kh_ref, v_ref, g_ref, rk_ref, lg_ref, lb_ref, o_ref):
    rb = lax.broadcasted_iota(I32, (LANES, LANES), 0) // RW_HEAD
    cb = lax.broadcasted_iota(I32, (LANES, LANES), 1) // RW_HEAD
    ones_blk = jnp.where(rb == cb, 1.0, 0.0).astype(BF16)
    y = y_ref[...]
    inv_n = 1.0 / RW_HEAD
    yc = y - _seg_sum(y, ones_blk) * inv_n
    var = _seg_sum(yc * yc, ones_blk) * inv_n
    y = yc * lax.rsqrt(var + LNX_EPS) * lg_ref[...] + lb_ref[...]
    y = y + _seg_sum(r_ref[...] * kh_ref[...] * rk_ref[...], ones_blk) * v_ref[...]
    o_ref[...] = (y * g_ref[...]).astype(o_ref.dtype)


def wkv_post(y, r, kh, v, g, r_k, lnx_g, lnx_b):
    b, t, d = y.shape
    tt = _row_tile(t, 256)
    row = pl.BlockSpec((None, tt, d), lambda bi, i: (bi, i, 0))
    vec = pl.BlockSpec((1, d), lambda bi, i: (0, 0))
    return pl.pallas_call(
        _wkv_post_kernel,
        grid=(b, t // tt),
        in_specs=[row] * 5 + [vec] * 3,
        out_specs=row,
        out_shape=jax.ShapeDtypeStruct((b, t, d), BF16),
        compiler_params=_cparams(("arbitrary", "arbitrary")),
        name="wkv_post",
    )(y, r, kh, v, g, *[x.reshape(1, d) for x in (r_k, lnx_g, lnx_b)])


def _state_to_blockdiag(s):
    st = jnp.swapaxes(s, 2, 3)
    st = st.reshape(s.shape[0], s.shape[1] // 2, 2, RW_HEAD, RW_HEAD)
    z = jnp.zeros_like(st[:, :, 0])
    return jnp.concatenate([jnp.concatenate([st[:, :, 0], z], axis=-1),
                            jnp.concatenate([z, st[:, :, 1]], axis=-1)], axis=-2)


def _state_from_blockdiag(hbd):
    b, p = hbd.shape[:2]
    x = hbd.reshape(b, p, 2, RW_HEAD, 2, RW_HEAD)
    diag = jnp.stack([x[:, :, 0, :, 0, :], x[:, :, 1, :, 1, :]], axis=2)
    return jnp.swapaxes(diag.reshape(b, 2 * p, RW_HEAD, RW_HEAD), 2, 3)


def _state_to_kernel_layout(s):
    b, h, n, _ = s.shape
    return jnp.transpose(s, (0, 2, 1, 3)).reshape(b, n, h * n)


def _state_from_kernel_layout(s, heads):
    b, n, _ = s.shape
    return jnp.transpose(s.reshape(b, n, heads, n), (0, 2, 1, 3))


def _run_group(x, mod_all, t_real, positions, sample_ctx, weights):
    (norm_g, ffn_w1, ffn_w3, ffn_w2, att_w_in, att_w_out, q_norm_g, k_norm_g, idx_k_norm_g,
     conv_w, conv_b, conv_ln_g, conv_ln_b, rw_mu, rw_w0, rw_w1, rw_w2, rw_a0, rw_a1, rw_a2, rw_g1, rw_g2,
     rw_k_k, rw_k_a, rw_r_k, rw_wr, rw_wk, rw_wv, rw_wo, rw_lnx_g, rw_lnx_b) = weights
    b, t, d = x.shape
    depth = norm_g.shape[0]
    aw = att_w_out.shape[1] - conv_w.shape[2]
    att_heads = aw // LANES
    cc = conv_w.shape[2]
    conv_width = conv_w.shape[1]
    idx_dim = idx_k_norm_g.shape[1]
    iq_w = IDX_HEADS * idx_dim
    rw_heads = d // RW_HEAD
    tables = _rope_tables(positions, LANES) + _rope_tables(positions, idx_dim)

    outs = dict(k=[], v=[], ik=[], conv=[], shift=[], wkv=[])
    h = resid_norm(x, mod_all[0], g=norm_g[0, 0], shift_idx=0, scale_idx=1, emit_x=False, h_dtype=BF16)
    for l in range(depth):
        i = l // 2
        mod = mod_all[l]
        even = l % 2 == 0
        y = ffn(h, ffn_w1, ffn_w3, ffn_w2, l, 0)
        x, h = resid_norm(x, mod, y=y, gate_idx=2, coef=HALF_STEP, g=norm_g[l, 1], shift_idx=3, scale_idx=4,
                          h_dtype=BF16 if even else F32)
        if even:
            qkvi = matmul(h, att_w_in, lead=i, n_cols=3 * aw + iq_w)
            w_small = jnp.pad(att_w_in[i][:, 3 * aw + iq_w:3 * aw + iq_w + idx_dim + IDX_HEADS],
                              ((0, 0), (0, LANES - idx_dim - IDX_HEADS)))
            small = matmul(h, w_small)
            u = matmul(h, att_w_in[i][:, 3 * aw + iq_w + idx_dim + IDX_HEADS:])
            q_bf, k_f, k_bf, v_bf, iq_bf, small, ikd_bf, glu = even_post(
                qkvi, small, u, tables, q_norm_g[i], k_norm_g[i], idx_k_norm_g[i], att_heads, idx_dim)
            if sample_ctx is None:
                att = dsa_prompt(q_bf, iq_bf, small, k_bf, v_bf, ikd_bf, idx_dim)
                hist = jnp.zeros((b, HIST_ROWS, cc), F32)
                outs["conv"].append(glu[:, t - (conv_width - 1):])
            else:
                pt = sample_ctx["page_table"]
                scores = sample_idx_scores(pt, iq_bf, small, sample_ctx["cache_idx_k"], i)
                pad_rows = lambda a: jnp.pad(a, ((0, 0), (0, LANES - t), (0, 0)))
                att = sample_attention(pt, scores, q_bf, iq_bf, small, pad_rows(ikd_bf), k_f,
                                       qkvi[:, :, 2 * aw:3 * aw], sample_ctx["cache_k"], sample_ctx["cache_v"],
                                       i, t_real, idx_dim)
                state = sample_ctx["state_conv"][i]
                hist = jnp.pad(state, ((0, 0), (HIST_ROWS - (conv_width - 1), 0), (0, 0)))
                outs["conv"].append(jnp.concatenate([state, glu[:, :t_real]], axis=1)[:, -(conv_width - 1):])
            conv_y = conv_branch(glu, hist, conv_w[i], conv_b[i], conv_ln_g[i], conv_ln_b[i])
            mixed = matmul(jnp.concatenate([att, conv_y], axis=-1), att_w_out, lead=i)
            outs["k"].append(k_f[:, :t_real].reshape(b, t_real, att_heads, LANES))
            outs["v"].append(qkvi[:, :t_real, 2 * aw:3 * aw].reshape(b, t_real, att_heads, LANES))
            outs["ik"].append(small[:, :t_real, :idx_dim])
        else:
            if sample_ctx is None:
                shift_prev = jnp.zeros((b, d), F32)
                s_init = jnp.zeros((b, rw_heads, RW_HEAD, RW_HEAD), F32)
            else:
                shift_prev = sample_ctx["state_shift"][i]
                s_init = sample_ctx["state_wkv"][i]
            xs = rw_mix(h, shift_prev, rw_mu[i])
            r = matmul(xs[0], rw_wr, lead=i)
            wl = lora(xs[1], rw_w1, rw_w2, i, "tanh")
            k = matmul(xs[2], rw_wk, lead=i)
            v = matmul(xs[3], rw_wv, lead=i)
            al = lora(xs[4], rw_a1, rw_a2, i, "none")
            g = lora(xs[5], rw_g1, rw_g2, i, "sigmoid")
            if t == t_real and t % WKV_TILE_MIN == 0:
                lw, kk, b_, kh = wkv_prep(k, wl, al, rw_w0[i], rw_a0[i], rw_k_k[i], rw_k_a[i])
                y, h_fin = wkv_chunked(r, lw, kh, v, kk, b_, _state_to_blockdiag(s_init))
                yg = wkv_post(y, r, kh, v, g, rw_r_k[i], rw_lnx_g[i], rw_lnx_b[i])
                s_out = _state_from_blockdiag(h_fin)
            else:
                yg, s_fin = wkv(r, k, v, wl, al, g, rw_w0[i], rw_a0[i], rw_k_k[i], rw_k_a[i], rw_r_k[i],
                                rw_lnx_g[i], rw_lnx_b[i], _state_to_kernel_layout(s_init), t_real)
                s_out = _state_from_kernel_layout(s_fin, rw_heads)
            mixed = matmul(yg, rw_wo, lead=i)
            outs["shift"].append(h[:, t_real - 1])
            outs["wkv"].append(s_out)
        x, h = resid_norm(x, mod, y=mixed, gate_idx=5, coef=1.0, g=norm_g[l, 2], shift_idx=6, scale_idx=7,
                          h_dtype=BF16)
        y = ffn(h, ffn_w1, ffn_w3, ffn_w2, l, 1)
        if l + 1 < depth:
            x, h = resid_norm(x, mod, y=y, gate_idx=8, coef=HALF_STEP, g=norm_g[l + 1, 0],
                              shift_idx=0, scale_idx=1, h_dtype=BF16, mod_norm=mod_all[l + 1])
        else:
            x = resid_norm(x, mod, y=y, gate_idx=8, coef=HALF_STEP)
    return x[:, :t_real], outs


def kernel(x_prompt, x_sample, cache_k, cache_v, cache_idx_k, state_conv, state_shift, state_wkv, page_table, c_prompt, c_sample, norm_g, ada_w, ada_b, ffn_w1, ffn_w3, ffn_w2, att_w_in, att_w_out, q_norm_g, k_norm_g, idx_k_norm_g, conv_w, conv_b, conv_ln_g, conv_ln_b, rw_mu, rw_w0, rw_w1, rw_w2, rw_a0, rw_a1, rw_a2, rw_g1, rw_g2, rw_k_k, rw_k_a, rw_r_k, rw_wr, rw_wk, rw_wv, rw_wo, rw_lnx_g, rw_lnx_b):
    bp, tp, d = x_prompt.shape
    bs, ts, _ = x_sample.shape
    depth = norm_g.shape[0]
    past = page_table.shape[1] * cache_k.shape[2]

    n_c = bp + bs
    c_rows = -(-n_c // 16) * 16
    c_all = jnp.pad(jnp.concatenate([c_prompt, c_sample], axis=0), ((0, c_rows - n_c), (0, 0)))
    mod = adaln_all(c_all, ada_w, ada_b).reshape(depth, c_rows, N_MOD, 1, d)
    mod_p, mod_s = mod[:, :bp], mod[:, bp:n_c]

    weights = (norm_g, ffn_w1, ffn_w3, ffn_w2, att_w_in, att_w_out, q_norm_g, k_norm_g, idx_k_norm_g,
               conv_w, conv_b, conv_ln_g, conv_ln_b, rw_mu, rw_w0, rw_w1, rw_w2, rw_a0, rw_a1, rw_a2,
               rw_g1, rw_g2, rw_k_k, rw_k_a, rw_r_k.reshape(rw_r_k.shape[0], -1), rw_wr, rw_wk, rw_wv, rw_wo,
               rw_lnx_g, rw_lnx_b)

    yp, op = _run_group(x_prompt, mod_p, tp, np.arange(tp), None, weights)
    xs_pad = jnp.pad(x_sample, ((0, 0), (0, SAMPLE_T_PAD - ts), (0, 0)))
    sample_ctx = dict(page_table=page_table, cache_k=cache_k, cache_v=cache_v, cache_idx_k=cache_idx_k,
                      state_conv=state_conv, state_shift=state_shift, state_wkv=state_wkv)
    ys, os_ = _run_group(xs_pad, mod_s, ts, past + np.arange(SAMPLE_T_PAD), sample_ctx, weights)

    st = lambda xs: jnp.stack(xs)
    return (yp, ys,
            st(op["k"]), st(op["v"]), st(op["ik"]), st(op["conv"]), st(op["shift"]), st(op["wkv"]),
            st(os_["k"]), st(os_["v"]), st(os_["ik"]), st(os_["conv"]), st(os_["shift"]), st(os_["wkv"]))
```

```python
import functools

import numpy as np
import jax
import jax.numpy as jnp
from jax import lax
from jax.experimental import pallas as pl
from jax.experimental.pallas import tpu as pltpu

F32 = jnp.float32
BF16 = jnp.bfloat16
I32 = jnp.int32

ROPE_THETA = 10000.0
NORM_EPS = 1e-6
LN_EPS = 1e-5
LNX_EPS = 64e-5
N_MOD = 9
HALF_STEP = 0.5
TOPK_MAX = 256
QBLOCK = 128
IDX_HEADS = 16

LANES = 128
SUBLANES = 8
VMEM_LIMIT_MB = 56
NEG_BIG = -1e30
INT_MIN = -(2 ** 31)

SAMPLE_T_PAD = 8
PAGES_PER_STEP = 8
IDX_PAGES_PER_STEP = 8


def _cparams(sem, vmem_mb=VMEM_LIMIT_MB):
    return pltpu.CompilerParams(dimension_semantics=sem, vmem_limit_bytes=vmem_mb * 1024 * 1024)


def _silu(x):
    return x * jax.nn.sigmoid(x)


def _dot(a, b):
    return jnp.dot(a, b, preferred_element_type=F32)


def _dot_nt(a, b):
    return lax.dot_general(a, b, (((1,), (1,)), ((), ())), preferred_element_type=F32)


def _row_tile(t, target):
    return t if t <= target else target


def _adaln_kernel(c_ref, w_ref, b_ref, o_ref):
    sc = _silu(c_ref[...]).astype(BF16)
    o_ref[...] = _dot(sc, w_ref[...].astype(BF16)) + b_ref[...]


def adaln_all(c_all, ada_w, ada_b):
    depth, d, n = ada_w.shape
    rows = c_all.shape[0]
    tn = 1024
    return pl.pallas_call(
        _adaln_kernel,
        grid=(depth, n // tn),
        in_specs=[
            pl.BlockSpec((rows, d), lambda l, j: (0, 0)),
            pl.BlockSpec((None, d, tn), lambda l, j: (l, 0, j)),
            pl.BlockSpec((None, 1, tn), lambda l, j: (l, 0, j)),
        ],
        out_specs=pl.BlockSpec((None, rows, tn), lambda l, j: (l, 0, j)),
        out_shape=jax.ShapeDtypeStruct((depth, rows, n), F32),
        compiler_params=_cparams(("arbitrary", "arbitrary")),
        name="adaln",
    )(c_all, ada_w, ada_b.reshape(depth, 1, n))


def _resid_norm_kernel(*refs, has_y, coef, emit_x, emit_h):
    refs = list(refs)
    x = refs.pop(0)[...]
    if has_y:
        y = refs.pop(0)[...]
        gate = refs.pop(0)[...]
        x = x + (coef * gate) * y
    if emit_h:
        g = refs.pop(0)[...]
        shift = refs.pop(0)[...]
        scale = refs.pop(0)[...]
    if emit_x:
        refs.pop(0)[...] = x
    if emit_h:
        h_ref = refs.pop(0)
        ms = jnp.mean(x * x, axis=-1, keepdims=True)
        h = x * lax.rsqrt(ms + NORM_EPS) * g
        h_ref[...] = (h * (1.0 + scale) + shift).astype(h_ref.dtype)


def resid_norm(x, mod, *, y=None, gate_idx=None, coef=1.0, g=None, shift_idx=None, scale_idx=None,
               emit_x=True, h_dtype=None, mod_norm=None):
    mod_norm = mod if mod_norm is None else mod_norm
    b, t, d = x.shape
    tt = _row_tile(t, 256)
    has_y = y is not None
    emit_h = h_dtype is not None
    row = pl.BlockSpec((None, tt, d), lambda bi, i: (bi, i, 0))

    def mod_spec(idx):
        return pl.BlockSpec((None, None, 1, d), lambda bi, i: (bi, idx, 0, 0))

    args, specs = [x], [row]
    if has_y:
        args += [y, mod]
        specs += [row, mod_spec(gate_idx)]
    if emit_h:
        args += [g.reshape(1, d), mod_norm, mod_norm]
        specs += [pl.BlockSpec((1, d), lambda bi, i: (0, 0)), mod_spec(shift_idx), mod_spec(scale_idx)]
    out_shape, out_specs = [], []
    if emit_x:
        out_shape.append(jax.ShapeDtypeStruct((b, t, d), F32))
        out_specs.append(row)
    if emit_h:
        out_shape.append(jax.ShapeDtypeStruct((b, t, d), h_dtype))
        out_specs.append(row)
    outs = pl.pallas_call(
        functools.partial(_resid_norm_kernel, has_y=has_y, coef=coef, emit_x=emit_x, emit_h=emit_h),
        grid=(b, t // tt),
        in_specs=specs,
        out_specs=out_specs,
        out_shape=out_shape,
        compiler_params=_cparams(("arbitrary", "arbitrary")),
        name="resid_norm",
    )(*args)
    return outs if len(outs) > 1 else outs[0]


def _ffn_kernel(h_ref, w1_ref, w3_ref, w2_ref, o_ref):
    f = pl.program_id(1)
    h = h_ref[...]
    a = _dot(h, w1_ref[...].astype(BF16))
    b = _dot(h, w3_ref[...].astype(BF16))
    z = (_silu(a) * b).astype(BF16)

    @pl.when(f == 0)
    def _():
        o_ref[...] = jnp.zeros(o_ref.shape, o_ref.dtype)

    o_ref[...] += _dot(z, w2_ref[...].astype(BF16))


def ffn(h, w1, w3, w2, layer, slot):
    b, t, d = h.shape
    m = b * t
    d_ff = w1.shape[-1]
    tm = 1024 if m >= 1024 else m
    tf = 256 if m >= 1024 else 512
    out = pl.pallas_call(
        _ffn_kernel,
        grid=(m // tm, d_ff // tf),
        in_specs=[
            pl.BlockSpec((tm, d), lambda i, f: (i, 0)),
            pl.BlockSpec((None, None, d, tf), lambda i, f: (layer, slot, 0, f)),
            pl.BlockSpec((None, None, d, tf), lambda i, f: (layer, slot, 0, f)),
            pl.BlockSpec((None, None, tf, d), lambda i, f: (layer, slot, f, 0)),
        ],
        out_specs=pl.BlockSpec((tm, d), lambda i, f: (i, 0)),
        out_shape=jax.ShapeDtypeStruct((m, d), F32),
        compiler_params=_cparams(("arbitrary", "arbitrary")),
        name="ffn",
    )(h.reshape(m, d), w1, w3, w2)
    return out.reshape(b, t, d)


def _mm_kernel(x_ref, w_ref, o_ref):
    o_ref[...] = _dot(x_ref[...], w_ref[...].astype(BF16)).astype(o_ref.dtype)


def matmul(x, w, lead=None, n_cols=None, out_dtype=F32):
    b, t, k = x.shape
    m = b * t
    n = w.shape[-1] if n_cols is None else n_cols
    tm = 1024 if m >= 1024 else m
    tn = 1024 if n % 1024 == 0 else (512 if n % 512 == 0 else n)
    if lead is None:
        w_spec = pl.BlockSpec((k, tn), lambda i, j: (0, j))
    else:
        w_spec = pl.BlockSpec((None, k, tn), lambda i, j: (lead, 0, j))
    out = pl.pallas_call(
        _mm_kernel,
        grid=(m // tm, n // tn),
        in_specs=[pl.BlockSpec((tm, k), lambda i, j: (i, 0)), w_spec],
        out_specs=pl.BlockSpec((tm, tn), lambda i, j: (i, j)),
        out_shape=jax.ShapeDtypeStruct((m, n), out_dtype),
        compiler_params=_cparams(("arbitrary", "arbitrary")),
        name="matmul",
    )(x.reshape(m, k), w)
    return out.reshape(b, t, n)


def _lora_kernel(x_ref, a_ref, b_ref, o_ref, *, act):
    h = _dot(x_ref[...], a_ref[...].astype(BF16))
    if act == "tanh":
        h = jnp.tanh(h)
    elif act == "sigmoid":
        h = jax.nn.sigmoid(h)
    o_ref[...] = _dot(h.astype(BF16), b_ref[...].astype(BF16))


def lora(x, a, bmat, layer, act):
    b, t, d = x.shape
    m = b * t
    r = a.shape[-1]
    n = bmat.shape[-1]
    tm = 512 if m >= 512 else m
    out = pl.pallas_call(
        functools.partial(_lora_kernel, act=act),
        grid=(m // tm,),
        in_specs=[
            pl.BlockSpec((tm, d), lambda i: (i, 0)),
            pl.BlockSpec((None, d, r), lambda i: (layer, 0, 0)),
            pl.BlockSpec((None, r, n), lambda i: (layer, 0, 0)),
        ],
        out_specs=pl.BlockSpec((tm, n), lambda i: (i, 0)),
        out_shape=jax.ShapeDtypeStruct((m, n), F32),
        compiler_params=_cparams(("arbitrary",)),
        name="lora",
    )(x.reshape(m, d), a, bmat)
    return out.reshape(b, t, n)


def _rope_tables(positions, head_dim):
    half = head_dim // 2
    inv = ROPE_THETA ** (-np.arange(half, dtype=np.float64) / half)
    ang = np.asarray(positions, np.float64)[:, None] * inv[None, :]
    cos = np.concatenate([np.cos(ang), np.cos(ang)], axis=-1)
    sin = np.concatenate([-np.sin(ang), np.sin(ang)], axis=-1)
    reps = LANES // head_dim
    return (jnp.asarray(np.tile(cos, (1, reps)), F32), jnp.asarray(np.tile(sin, (1, reps)), F32))


def _rope128(x, cos, sin):
    return x * cos + pltpu.roll(x, 64, 1) * sin


def _rope64(x, cos, sin, lane):
    first_half = (lane % 64) < 32
    partner = jnp.where(first_half, pltpu.roll(x, 96, 1), pltpu.roll(x, 32, 1))
    return x * cos + partner * sin


def _even_post_kernel(q_ref, k_ref, v_ref, iq_ref, small_ref, u_ref,
                      c128_ref, s128_ref, c64_ref, s64_ref, qg_ref, kg_ref, ikg_ref,
                      qo_ref, kf_ref, kb_ref, vb_ref, iqo_ref, smallo_ref, ikd_ref, uo_ref,
                      *, att_heads, idx_dim, idx_heads):
    c128, s128 = c128_ref[...], s128_ref[...]
    c64, s64 = c64_ref[...], s64_ref[...]
    tt = c128.shape[0]
    lane = lax.broadcasted_iota(I32, (tt, LANES), 1)
    qg, kg = qg_ref[...], kg_ref[...]
    for h in range(att_heads):
        sl = slice(h * LANES, (h + 1) * LANES)
        q = q_ref[:, sl]
        q = q * lax.rsqrt(jnp.mean(q * q, axis=-1, keepdims=True) + NORM_EPS) * qg
        qo_ref[:, sl] = _rope128(q, c128, s128).astype(BF16)
        k = k_ref[:, sl]
        k = k * lax.rsqrt(jnp.mean(k * k, axis=-1, keepdims=True) + NORM_EPS) * kg
        k = _rope128(k, c128, s128)
        kf_ref[:, sl] = k
        kb_ref[:, sl] = k.astype(BF16)
        vb_ref[:, sl] = v_ref[:, sl].astype(BF16)
    for p in range(idx_heads * idx_dim // LANES):
        sl = slice(p * LANES, (p + 1) * LANES)
        iqo_ref[:, sl] = _rope64(iq_ref[:, sl], c64, s64, lane).astype(BF16)
    small = small_ref[...]
    is_ik = lane < idx_dim
    ik = jnp.where(is_ik, small, 0.0)
    ms = jnp.sum(ik * ik, axis=-1, keepdims=True) * (1.0 / idx_dim)
    ik = _rope64(ik * lax.rsqrt(ms + NORM_EPS) * ikg_ref[...], c64, s64, lane)
    ik = jnp.where(is_ik, ik, 0.0)
    smallo_ref[...] = jnp.where(is_ik, ik, small * (idx_heads ** -0.5))
    ikd_ref[...] = (ik + pltpu.roll(ik, 64, 1)).astype(BF16)
    cc = u_ref.shape[-1] // 2
    uo_ref[...] = u_ref[:, :cc] * jax.nn.sigmoid(u_ref[:, cc:])


def even_post(qkvi, small, u, tables, q_g, k_g, ik_g, att_heads, idx_dim):
    b, t, _ = qkvi.shape
    aw = att_heads * LANES
    iw = IDX_HEADS * idx_dim
    cc = u.shape[-1] // 2
    tt = _row_tile(t, 256)
    c128, s128, c64, s64 = tables

    def row(width, col=0):
        return pl.BlockSpec((None, tt, width), lambda bi, i: (bi, i, col))

    tab = pl.BlockSpec((tt, LANES), lambda bi, i: (i, 0))
    vec = pl.BlockSpec((1, LANES), lambda bi, i: (0, 0))
    ikg_pad = jnp.zeros((1, LANES), F32).at[0, :idx_dim].set(ik_g)
    assert aw == iw, "q/k/v/indexer-q column groups are addressed as equal-width blocks"
    return pl.pallas_call(
        functools.partial(_even_post_kernel, att_heads=att_heads, idx_dim=idx_dim, idx_heads=IDX_HEADS),
        grid=(b, t // tt),
        in_specs=[row(aw, 0), row(aw, 1), row(aw, 2), row(iw, 3), row(LANES), row(2 * cc),
                  tab, tab, tab, tab, vec, vec, vec],
        out_specs=[row(aw), row(aw), row(aw), row(aw), row(iw), row(LANES), row(LANES), row(cc)],
        out_shape=[
            jax.ShapeDtypeStruct((b, t, aw), BF16),
            jax.ShapeDtypeStruct((b, t, aw), F32),
            jax.ShapeDtypeStruct((b, t, aw), BF16),
            jax.ShapeDtypeStruct((b, t, aw), BF16),
            jax.ShapeDtypeStruct((b, t, iw), BF16),
            jax.ShapeDtypeStruct((b, t, LANES), F32),
            jax.ShapeDtypeStruct((b, t, LANES), BF16),
            jax.ShapeDtypeStruct((b, t, cc), F32),
        ],
        compiler_params=_cparams(("arbitrary", "arbitrary")),
        name="even_post",
    )(qkvi, qkvi, qkvi, qkvi, small, u, c128, s128, c64, s64,
      q_g.reshape(1, LANES), k_g.reshape(1, LANES), ikg_pad)


def _order_key(score):
    bits = pltpu.bitcast(score, I32)
    return jnp.where(bits < 0, bits ^ 0x7FFFFFFF, bits)


def _kth_largest_key(count_ge, top_k, shape):
    tau = jnp.where(count_ge(jnp.zeros(shape, I32)) >= top_k, 0, INT_MIN).astype(I32)

    def body(i, tau):
        cand = tau | jnp.left_shift(jnp.int32(1), 30 - i)
        return jnp.where(count_ge(cand) >= top_k, cand, tau)

    return lax.fori_loop(0, 31, body, tau)


def _idx_head_lhs(iq, h, lane):
    slab = iq[:, (h // 2) * LANES:(h // 2 + 1) * LANES]
    keep = (lane < 64) if h % 2 == 0 else (lane >= 64)
    return jnp.where(keep, slab, jnp.zeros_like(slab))


def _idx_weight(small, h, lane, idx_dim):
    return jnp.sum(jnp.where(lane == idx_dim + h, small, 0.0), axis=-1, keepdims=True)


KEY_CHUNK = 512


def _dsa_prompt_kernel(q_ref, iq_ref, small_t_ref, k_ref, v_t_ref, ikd_ref, o_ref, lhs_sc, key_sc,
                       *, top_k, att_heads, idx_dim):
    n = pl.program_id(1)
    qb = q_ref.shape[0]
    n_chunks = ((n + 1) * qb + KEY_CHUNK - 1) // KEY_CHUNK
    lane = lax.broadcasted_iota(I32, (qb, LANES), 1)
    iq = iq_ref[...]
    for h in range(IDX_HEADS):
        lhs_sc[h // 2, (h % 2) * qb:(h % 2 + 1) * qb, :] = _idx_head_lhs(iq, h, lane)
    small_t = small_t_ref[...]
    weights = [small_t[idx_dim + h:idx_dim + h + 1, :] * (idx_dim ** -0.5) for h in range(IDX_HEADS)]
    qpos = lax.broadcasted_iota(I32, (KEY_CHUNK, qb), 1) + n * qb
    krow = lax.broadcasted_iota(I32, (KEY_CHUNK, qb), 0)

    def chunk_rows(c):
        return pl.ds(pl.multiple_of(c * KEY_CHUNK, KEY_CHUNK), KEY_CHUNK)

    def score_chunk(c, carry):
        ikd = ikd_ref[chunk_rows(c), :]
        score = jnp.zeros((KEY_CHUNK, qb), F32)
        for pair in range(IDX_HEADS // 2):
            d = jnp.maximum(_dot_nt(ikd, lhs_sc[pair]), 0.0)
            score = score + weights[2 * pair] * d[:, :qb] + weights[2 * pair + 1] * d[:, qb:]
        admissible = krow + c * KEY_CHUNK <= qpos
        key_sc[c] = jnp.where(admissible, _order_key(score), INT_MIN)
        return carry

    lax.fori_loop(0, n_chunks, score_chunk, 0)

    def count_ge(cand):
        def add_chunk(c, acc):
            hit = (key_sc[c] >= cand).astype(I32)
            return acc + jnp.sum(hit.reshape(KEY_CHUNK // SUBLANES, SUBLANES, qb), axis=0)

        part = lax.fori_loop(0, n_chunks, add_chunk, jnp.zeros((SUBLANES, qb), I32))
        return jnp.sum(part, axis=0, keepdims=True)

    tau = jnp.maximum(_kth_largest_key(count_ge, top_k, (1, qb)), INT_MIN + 1)
    scale_log2e = LANES ** -0.5 * float(np.log2(np.e))
    heads = [slice(h * LANES, (h + 1) * LANES) for h in range(att_heads)]

    def attend_chunk(c, carry):
        sel = key_sc[c] >= tau
        out = []
        for sl, (m_old, l_old, acc) in zip(heads, carry):
            s = jnp.where(sel, _dot_nt(k_ref[chunk_rows(c), sl], q_ref[:, sl]), NEG_BIG)
            m_new = jnp.maximum(m_old, jnp.max(s, axis=0, keepdims=True))
            alpha = jnp.exp2((m_old - m_new) * scale_log2e)
            p = jnp.exp2((s - m_new) * scale_log2e)
            l_new = alpha * l_old + jnp.sum(p, axis=0, keepdims=True)
            out.append((m_new, l_new, alpha * acc + _dot(v_t_ref[c, sl, :], p.astype(BF16))))
        return tuple(out)

    init = tuple((jnp.full((1, qb), NEG_BIG, F32), jnp.zeros((1, qb), F32), jnp.zeros((LANES, qb), F32))
                 for _ in heads)
    for sl, (_, l_fin, acc) in zip(heads, lax.fori_loop(0, n_chunks, attend_chunk, init)):
        o_ref[:, sl] = (acc / l_fin).T.astype(o_ref.dtype)


def dsa_prompt(q_bf, iq_bf, small, k_bf, v_bf, ikd_bf, idx_dim):
    b, t, aw = q_bf.shape
    assert t % KEY_CHUNK == 0 and KEY_CHUNK % QBLOCK == 0
    top_k = min(TOPK_MAX, t // 4)
    n_kc = t // KEY_CHUNK
    small_t = jnp.swapaxes(small, 1, 2)
    v_t = jnp.swapaxes(v_bf.reshape(b, n_kc, KEY_CHUNK, aw), 2, 3)
    qrow = lambda w: pl.BlockSpec((None, QBLOCK, w), lambda bi, n: (bi, n, 0))
    full = lambda w: pl.BlockSpec((None, t, w), lambda bi, n: (bi, 0, 0))
    return pl.pallas_call(
        functools.partial(_dsa_prompt_kernel, top_k=top_k, att_heads=aw // LANES, idx_dim=idx_dim),
        grid=(b, t // QBLOCK),
        in_specs=[qrow(aw), qrow(iq_bf.shape[-1]),
                  pl.BlockSpec((None, LANES, QBLOCK), lambda bi, n: (bi, 0, n)),
                  full(aw),
                  pl.BlockSpec((None, n_kc, aw, KEY_CHUNK), lambda bi, n: (bi, 0, 0, 0)),
                  full(LANES)],
        out_specs=qrow(aw),
        out_shape=jax.ShapeDtypeStruct((b, t, aw), BF16),
        scratch_shapes=[pltpu.VMEM((IDX_HEADS // 2, 2 * QBLOCK, LANES), BF16),
                        pltpu.VMEM((n_kc, KEY_CHUNK, QBLOCK), I32)],
        compiler_params=_cparams(("arbitrary", "arbitrary")),
        name="dsa_prompt",
    )(q_bf, iq_bf, small_t, k_bf, v_t, ikd_bf)


def _idx_lhs_all_heads(iq, lane):
    iq = iq.astype(F32)
    return jnp.concatenate([_idx_head_lhs(iq, h, lane) for h in range(IDX_HEADS)], axis=0).astype(BF16)


def _idx_scores(lhs, small, keys_dup, lane, idx_dim):
    d = _dot_nt(lhs, keys_dup) * (idx_dim ** -0.5)
    score = jnp.zeros((SAMPLE_T_PAD, keys_dup.shape[0]), F32)
    for h in range(IDX_HEADS):
        dh = d[h * SAMPLE_T_PAD:(h + 1) * SAMPLE_T_PAD, :]
        score = score + _idx_weight(small, h, lane, idx_dim) * jnp.maximum(dh, 0.0)
    return score


def _sample_idx_kernel(pt_ref, iq_ref, small_ref, *rest, idx_dim):
    page_refs, o_ref = rest[:-1], rest[-1]
    lane = lax.broadcasted_iota(I32, (SAMPLE_T_PAD, LANES), 1)
    keys = jnp.concatenate([r[...] for r in page_refs], axis=0)
    keys_dup = jnp.concatenate([keys, keys], axis=-1).astype(BF16)
    lhs = _idx_lhs_all_heads(iq_ref[...], lane)
    o_ref[...] = _idx_scores(lhs, small_ref[...], keys_dup, lane, idx_dim)


def sample_idx_scores(page_table, iq_bf, small, cache_idx_k, layer):
    b, n_pages = page_table.shape
    page, idx_dim = cache_idx_k.shape[-2:]
    g = IDX_PAGES_PER_STEP if n_pages % IDX_PAGES_PER_STEP == 0 else 1

    def page_spec(j):
        return pl.BlockSpec((None, None, page, idx_dim),
                            lambda bi, s, pt: (layer, pt[bi * n_pages + s * g + j], 0, 0))

    row = lambda w: pl.BlockSpec((None, SAMPLE_T_PAD, w), lambda bi, s, pt: (bi, 0, 0))
    grid_spec = pltpu.PrefetchScalarGridSpec(
        num_scalar_prefetch=1,
        grid=(b, n_pages // g),
        in_specs=[row(iq_bf.shape[-1]), row(LANES)] + [page_spec(j) for j in range(g)],
        out_specs=pl.BlockSpec((None, SAMPLE_T_PAD, g * page), lambda bi, s, pt: (bi, 0, s)),
    )
    return pl.pallas_call(
        functools.partial(_sample_idx_kernel, idx_dim=idx_dim),
        grid_spec=grid_spec,
        out_shape=jax.ShapeDtypeStruct((b, SAMPLE_T_PAD, n_pages * page), F32),
        compiler_params=_cparams(("arbitrary", "arbitrary")),
        name="sample_idx",
    )(page_table.reshape(-1), iq_bf, small, *([cache_idx_k] * g))


def _sample_attn_kernel(pt_ref, sc_all_ref, sc_ref, q_ref, iq_ref, small_ref, ikn_ref, kn_ref, vn_ref,
                        *rest, top_k, n_new, att_heads, idx_dim, n_groups):
    g = (len(rest) - 8) // 2
    k_pages, v_pages = rest[:g], rest[g:2 * g]
    o_ref, qrows_ref, erow_ref, tau_ref, keyn_ref, m_ref, l_ref, acc_ref = rest[2 * g:]
    s_idx = pl.program_id(1)
    page = k_pages[0].shape[0]
    scale = LANES ** -0.5
    assert att_heads == SUBLANES and att_heads * SAMPLE_T_PAD <= LANES and page <= LANES

    @pl.when(s_idx == 0)
    def _():
        lane = lax.broadcasted_iota(I32, (SAMPLE_T_PAD, LANES), 1)
        tok = lax.broadcasted_iota(I32, (SAMPLE_T_PAD, LANES), 0)
        lhs = _idx_lhs_all_heads(iq_ref[...], lane)
        sc_new = _idx_scores(lhs, small_ref[...], ikn_ref[...], lane, idx_dim)
        new_ok = (lane <= tok) & (lane < n_new)
        key_new = jnp.where(new_ok, _order_key(sc_new), INT_MIN)
        key_past = _order_key(sc_all_ref[...])

        def count_ge(cand):
            return (jnp.sum((key_past >= cand).astype(I32), axis=-1, keepdims=True)
                    + jnp.sum((key_new >= cand).astype(I32), axis=-1, keepdims=True))

        tau = _kth_largest_key(count_ge, top_k, (SAMPLE_T_PAD, 1))
        tau_ref[...] = jnp.broadcast_to(tau, (SAMPLE_T_PAD, LANES))
        keyn_ref[...] = key_new
        q = q_ref[...].astype(F32)
        q_rows = [q[:, h * LANES:(h + 1) * LANES] for h in range(att_heads)]
        q_rows.append(jnp.zeros((LANES - att_heads * SAMPLE_T_PAD, LANES), F32))
        qrows_ref[...] = jnp.concatenate(q_rows, axis=0).astype(BF16)
        r_tok = lax.broadcasted_iota(I32, erow_ref.shape, 0) // att_heads
        c_tok = lax.broadcasted_iota(I32, erow_ref.shape, 1)
        erow_ref[...] = jnp.where(r_tok == c_tok, 1.0, 0.0).astype(BF16)
        m_ref[...] = jnp.full(m_ref.shape, NEG_BIG, F32)
        l_ref[...] = jnp.zeros(l_ref.shape, F32)
        acc_ref[...] = jnp.zeros(acc_ref.shape, F32)

    tau = tau_ref[:, 0:1]
    rep = jnp.where(lax.broadcasted_iota(I32, (LANES, LANES), 0)
                    == lax.broadcasted_iota(I32, (LANES, LANES), 1) % SAMPLE_T_PAD, 1.0, 0.0).astype(BF16)
    head_match8 = (lax.broadcasted_iota(I32, (SUBLANES, LANES), 0)
                   == lax.broadcasted_iota(I32, (SUBLANES, LANES), 1) // SAMPLE_T_PAD)

    def masked_scores(sel8, keys):
        rows = keys.shape[0]
        n_tok = rows // att_heads
        sel_pad = jnp.concatenate([sel8.astype(F32), jnp.zeros((LANES - SAMPLE_T_PAD, LANES), F32)], axis=0)
        sel_rep = _dot(sel_pad.T.astype(BF16), rep)
        sel_rows = _dot(erow_ref[0:rows, :], sel_rep.astype(BF16))
        head_match = jnp.broadcast_to(head_match8[None], (n_tok, SUBLANES, LANES)).reshape(rows, LANES)
        mask = (sel_rows > 0.5) & head_match
        return jnp.where(mask, _dot_nt(keys, qrows_ref[...]) * scale, NEG_BIG)

    def attend(groups):
        scores = [masked_scores(sel8, keys) for sel8, keys, _ in groups]
        m_old = m_ref[0:1, :]
        m_new = m_old
        for s in scores:
            m_new = jnp.maximum(m_new, jnp.max(s, axis=0, keepdims=True))
        alpha = jnp.exp(m_old - m_new)
        l_new = alpha * l_ref[0:1, :]
        acc = alpha * acc_ref[...]
        for s, (_, _, vals) in zip(scores, groups):
            p = jnp.exp(s - m_new)
            l_new = l_new + jnp.sum(p, axis=0, keepdims=True)
            acc = acc + lax.dot_general(vals, p.astype(BF16), (((0,), (0,)), ((), ())),
                                        preferred_element_type=F32)
        l_ref[...] = jnp.broadcast_to(l_new, l_ref.shape)
        acc_ref[...] = acc
        m_ref[...] = jnp.broadcast_to(m_new, m_ref.shape)

    def as_rows(ref):
        return ref[...].reshape(ref.shape[0] * att_heads, LANES).astype(BF16)

    attend([(_order_key(sc_ref[:, j * page:(j + 1) * page]) >= tau, as_rows(k_pages[j]), as_rows(v_pages[j]))
            for j in range(g)])

    @pl.when(s_idx == n_groups - 1)
    def _():
        attend([(keyn_ref[...] >= tau, as_rows(kn_ref), as_rows(vn_ref))])
        out = (acc_ref[...] / l_ref[0:1, :]).T
        for h in range(att_heads):
            o_ref[:, h * LANES:(h + 1) * LANES] = (
                out[h * SAMPLE_T_PAD:(h + 1) * SAMPLE_T_PAD, :].astype(o_ref.dtype))


def sample_attention(page_table, scores, q_bf, iq_bf, small, ikn_pad, k_new, v_new,
                     cache_k, cache_v, layer, n_new, idx_dim):
    b, n_pages = page_table.shape
    page = cache_k.shape[2]
    aw = q_bf.shape[-1]
    att_heads = aw // LANES
    past = n_pages * page
    top_k = min(TOPK_MAX, (past + n_new) // 4)
    g = PAGES_PER_STEP if n_pages % PAGES_PER_STEP == 0 else 1
    n_groups = n_pages // g

    def page_spec(j):
        return pl.BlockSpec((None, None, page, att_heads, LANES),
                            lambda bi, s, pt: (layer, pt[bi * n_pages + s * g + j], 0, 0, 0))

    def per_seq(r, w):
        return pl.BlockSpec((None, r, w), lambda bi, s, pt: (bi, 0, 0))

    new_rows = pl.BlockSpec((None, SAMPLE_T_PAD, att_heads, LANES), lambda bi, s, pt: (bi, 0, 0, 0))
    grid_spec = pltpu.PrefetchScalarGridSpec(
        num_scalar_prefetch=1,
        grid=(b, n_groups),
        in_specs=[per_seq(SAMPLE_T_PAD, past),
                  pl.BlockSpec((None, SAMPLE_T_PAD, g * page), lambda bi, s, pt: (bi, 0, s)),
                  per_seq(SAMPLE_T_PAD, aw), per_seq(SAMPLE_T_PAD, iq_bf.shape[-1]),
                  per_seq(SAMPLE_T_PAD, LANES),
                  per_seq(LANES, LANES), new_rows, new_rows]
                 + [page_spec(j) for j in range(g)] * 2,
        out_specs=per_seq(SAMPLE_T_PAD, aw),
        scratch_shapes=[
            pltpu.VMEM((LANES, LANES), BF16),
            pltpu.VMEM((page * att_heads, LANES), BF16),
            pltpu.VMEM((SAMPLE_T_PAD, LANES), I32),
            pltpu.VMEM((SAMPLE_T_PAD, LANES), I32),
            pltpu.VMEM((SUBLANES, LANES), F32),
            pltpu.VMEM((SUBLANES, LANES), F32),
            pltpu.VMEM((LANES, LANES), F32),
        ],
    )
    return pl.pallas_call(
        functools.partial(_sample_attn_kernel, top_k=top_k, n_new=n_new, att_heads=att_heads,
                          idx_dim=idx_dim, n_groups=n_groups),
        grid_spec=grid_spec,
        out_shape=jax.ShapeDtypeStruct((b, SAMPLE_T_PAD, aw), BF16),
        compiler_params=_cparams(("arbitrary", "arbitrary")),
        name="sample_attn",
    )(page_table.reshape(-1), scores, scores, q_bf, iq_bf, small, ikn_pad,
      k_new.reshape(b, SAMPLE_T_PAD, att_heads, LANES), v_new.reshape(b, SAMPLE_T_PAD, att_heads, LANES),
      *([cache_k] * g), *([cache_v] * g))


HIST_ROWS = 32
CONV_ROWS = 32


def _conv_kernel(*refs, width, multi_tile):
    if multi_tile:
        u_ref, prev_ref, hist_ref, cw_ref, cb_ref, g_ref, b_ref, o_ref, buf = refs
    else:
        u_ref, hist_ref, cw_ref, cb_ref, g_ref, b_ref, o_ref, buf = refs
    tt = u_ref.shape[0]
    buf[HIST_ROWS:HIST_ROWS + tt, :] = u_ref[...]
    if multi_tile:
        buf[0:HIST_ROWS, :] = jnp.where(pl.program_id(1) == 0, hist_ref[...], prev_ref[...])
    else:
        buf[0:HIST_ROWS, :] = hist_ref[...]
    first = HIST_ROWS - (width - 1)
    rows = min(CONV_ROWS, tt)
    cb, g, b = cb_ref[...], g_ref[...], b_ref[...]
    for r0 in range(0, tt, rows):
        acc = jnp.zeros((rows, u_ref.shape[1]), F32)
        for j in range(width):
            acc = acc + buf[r0 + first + j:r0 + first + j + rows, :] * cw_ref[j:j + 1, :]
        y = acc + cb
        yc = y - jnp.mean(y, axis=-1, keepdims=True)
        var = jnp.mean(yc * yc, axis=-1, keepdims=True)
        y = yc * lax.rsqrt(var + LN_EPS) * g + b
        o_ref[r0:r0 + rows, :] = _silu(y).astype(o_ref.dtype)


def conv_branch(u, hist, cw, cb, ln_g, ln_b):
    b, t, c = u.shape
    width = cw.shape[0]
    tt = _row_tile(t, 256)
    multi_tile = t > tt
    row = pl.BlockSpec((None, tt, c), lambda bi, i: (bi, i, 0))
    vec = pl.BlockSpec((1, c), lambda bi, i: (0, 0))
    specs, args = [row], [u]
    if multi_tile:
        per = tt // HIST_ROWS
        specs.append(pl.BlockSpec((None, HIST_ROWS, c), lambda bi, i: (bi, jnp.maximum(i * per - 1, 0), 0)))
        args.append(u)
    specs += [pl.BlockSpec((None, HIST_ROWS, c), lambda bi, i: (bi, 0, 0)),
              pl.BlockSpec((width, c), lambda bi, i: (0, 0)), vec, vec, vec]
    args += [hist, cw, cb.reshape(1, c), ln_g.reshape(1, c), ln_b.reshape(1, c)]
    return pl.pallas_call(
        functools.partial(_conv_kernel, width=width, multi_tile=multi_tile),
        grid=(b, t // tt),
        in_specs=specs,
        out_specs=row,
        out_shape=jax.ShapeDtypeStruct((b, t, c), BF16),
        scratch_shapes=[pltpu.VMEM((HIST_ROWS + tt, c), F32)],
        compiler_params=_cparams(("arbitrary", "arbitrary")),
        name="conv_branch",
    )(*args)


def _rw_mix_kernel(h_ref, prev_ref, shift_ref, mu_ref, o_ref, buf):
    tt = h_ref.shape[0]
    h = h_ref[...]
    buf[SUBLANES:SUBLANES + tt, :] = h
    buf[SUBLANES - 1:SUBLANES, :] = jnp.where(pl.program_id(1) == 0, shift_ref[...],
                                              prev_ref[SUBLANES - 1:SUBLANES, :])
    xx = buf[SUBLANES - 1:SUBLANES - 1 + tt, :] - h
    for j in range(o_ref.shape[0]):
        o_ref[j] = (h + xx * mu_ref[j:j + 1, :]).astype(o_ref.dtype)


def rw_mix(h, shift_prev, mu):
    b, t, d = h.shape
    n_mix = mu.shape[0]
    tt = _row_tile(t, 256)
    per = tt // SUBLANES
    return pl.pallas_call(
        _rw_mix_kernel,
        grid=(b, t // tt),
        in_specs=[
            pl.BlockSpec((None, tt, d), lambda bi, i: (bi, i, 0)),
            pl.BlockSpec((None, SUBLANES, d), lambda bi, i: (bi, jnp.maximum(i * per - 1, 0), 0)),
            pl.BlockSpec((None, 1, d), lambda bi, i: (bi, 0, 0)),
            pl.BlockSpec((n_mix, d), lambda bi, i: (0, 0)),
        ],
        out_specs=pl.BlockSpec((n_mix, None, tt, d), lambda bi, i: (0, bi, i, 0)),
        out_shape=jax.ShapeDtypeStruct((n_mix, b, t, d), BF16),
        scratch_shapes=[pltpu.VMEM((SUBLANES + tt, d), F32)],
        compiler_params=_cparams(("arbitrary", "arbitrary")),
        name="rw_mix",
    )(h, h, shift_prev.reshape(b, 1, d), mu)


RW_HEAD = 64


def _seg_sum(x, ones_blk):
    outs = []
    for s in range(x.shape[1] // LANES):
        xs = x[:, s * LANES:(s + 1) * LANES]
        hi = xs.astype(BF16)
        lo = (xs - hi.astype(F32)).astype(BF16)
        outs.append(_dot(hi, ones_blk) + _dot(lo, ones_blk))
    return jnp.concatenate(outs, axis=-1)


def _wkv_kernel(r_ref, k_ref, v_ref, wl_ref, al_ref, g_ref,
                w0_ref, a0_ref, kk_ref, ka_ref, rk_ref, lg_ref, lb_ref, s0_ref,
                o_ref, s_ref, dec_sc, kk_sc, b_sc, kh_sc, y_sc, *, n_steps):
    i = pl.program_id(1)
    tc, d = r_ref.shape
    rb = lax.broadcasted_iota(I32, (LANES, LANES), 0) // RW_HEAD
    cb = lax.broadcasted_iota(I32, (LANES, LANES), 1) // RW_HEAD
    ones_blk = jnp.where(rb == cb, 1.0, 0.0).astype(BF16)
    eye2 = (lax.broadcasted_iota(I32, (RW_HEAD, LANES), 0)
            == lax.broadcasted_iota(I32, (RW_HEAD, LANES), 1) % RW_HEAD)

    @pl.when(i == 0)
    def _():
        s_ref[...] = s0_ref[...]

    k = k_ref[...]
    z = -(w0_ref[...] + wl_ref[...])
    softplus = jnp.maximum(z, 0.0) + jnp.log(1.0 + jnp.exp(-jnp.abs(z)))
    dec_sc[...] = jnp.exp(-jnp.exp(-softplus - 0.5))
    a = jax.nn.sigmoid(a0_ref[...] + al_ref[...])
    kk = k * kk_ref[...]
    kk = kk * lax.rsqrt(jnp.maximum(_seg_sum(kk * kk, ones_blk), 1e-24))
    kk_sc[...] = kk
    b_sc[...] = kk * a
    kh_sc[...] = k * (1.0 + (a - 1.0) * ka_ref[...])
    assert -(-n_steps // SUBLANES) * SUBLANES == tc, "the recurrence blocks must cover the whole time tile"

    steps_per_block = min(SUBLANES, n_steps)

    slabs = [slice(s * LANES, (s + 1) * LANES) for s in range(d // LANES)]
    head_sel = (lax.broadcasted_iota(I32, (SUBLANES, LANES), 0)
                == lax.broadcasted_iota(I32, (SUBLANES, LANES), 1) // RW_HEAD).astype(BF16)

    def block(blk, carry):
        rows = pl.ds(pl.multiple_of(blk * SUBLANES, SUBLANES), SUBLANES)
        st = [s_ref[:, sl] for sl in slabs]
        y_rows = [[] for _ in slabs]
        for j in range(steps_per_block):
            sa, vb = [], []
            for s, sl in enumerate(slabs):
                lhs = jnp.concatenate([st[s] * kk_sc[rows, sl][j:j + 1],
                                       jnp.where(eye2, v_ref[rows, sl][j:j + 1], 0.0)], axis=0)
                both = _dot(lhs.astype(BF16), ones_blk)
                sa.append(both[:RW_HEAD])
                vb.append(both[RW_HEAD:])
            for s, sl in enumerate(slabs):
                st[s] = (st[s] * dec_sc[rows, sl][j:j + 1] - sa[s] * b_sc[rows, sl][j:j + 1]
                         + vb[s] * kh_sc[rows, sl][j:j + 1])
            for s, sl in enumerate(slabs):
                yh = _dot_nt(head_sel, (st[s] * r_ref[rows, sl][j:j + 1]).astype(BF16))
                y_rows[s].append(jnp.concatenate([yh[0:1], yh[1:2]], axis=-1))
        for s, sl in enumerate(slabs):
            s_ref[:, sl] = st[s]
            pad = [jnp.zeros((1, LANES), F32)] * (SUBLANES - steps_per_block)
            y_sc[rows, sl] = jnp.concatenate(y_rows[s] + pad, axis=0)
        return carry

    lax.fori_loop(0, -(-n_steps // SUBLANES), block, 0)

    y = y_sc[...]
    inv_n = 1.0 / RW_HEAD
    yc = y - _seg_sum(y, ones_blk) * inv_n
    var = _seg_sum(yc * yc, ones_blk) * inv_n
    y = yc * lax.rsqrt(var + LNX_EPS) * lg_ref[...] + lb_ref[...]
    r = r_ref[...]
    y = y + _seg_sum(r * kh_sc[...] * rk_ref[...], ones_blk) * v_ref[...]
    o_ref[...] = (y * g_ref[...]).astype(o_ref.dtype)


def wkv(r, k, v, wl, al, g, w0, a0, k_k, k_a, r_k, lnx_g, lnx_b, s0, n_steps):
    b, t, d = r.shape
    tc = _row_tile(t, 128)
    steps = tc if t > tc else n_steps
    row = pl.BlockSpec((None, tc, d), lambda bi, i: (bi, i, 0))
    vec = pl.BlockSpec((1, d), lambda bi, i: (0, 0))
    st = pl.BlockSpec((None, RW_HEAD, d), lambda bi, i: (bi, 0, 0))
    vecs = [x.reshape(1, d) for x in (w0, a0, k_k, k_a, r_k, lnx_g, lnx_b)]
    return pl.pallas_call(
        functools.partial(_wkv_kernel, n_steps=steps),
        grid=(b, t // tc),
        in_specs=[row] * 6 + [vec] * 7 + [st],
        out_specs=[row, st],
        out_shape=[jax.ShapeDtypeStruct((b, t, d), BF16), jax.ShapeDtypeStruct((b, RW_HEAD, d), F32)],
        scratch_shapes=[pltpu.VMEM((tc, d), F32)] * 5,
        compiler_params=_cparams(("arbitrary", "arbitrary")),
        name="wkv",
    )(r, k, v, wl, al, g, *vecs, s0)


WKV_CHUNK = 64
WKV_TILE = 1024
WKV_TILE_MIN = 512


def _split3(x):
    h1 = x.astype(BF16)
    r1 = x - h1.astype(F32)
    h2 = r1.astype(BF16)
    return h1, h2, (r1 - h2.astype(F32)).astype(BF16)


def _wkv_prep_kernel(k_ref, wl_ref, al_ref, w0_ref, a0_ref, kkp_ref, ka_ref, lw_ref, kk_ref, b_ref, kh_ref):
    rb = lax.broadcasted_iota(I32, (LANES, LANES), 0) // RW_HEAD
    cb = lax.broadcasted_iota(I32, (LANES, LANES), 1) // RW_HEAD
    ones_blk = jnp.where(rb == cb, 1.0, 0.0).astype(BF16)
    k = k_ref[...]
    z = -(w0_ref[...] + wl_ref[...])
    softplus = jnp.maximum(z, 0.0) + jnp.log(1.0 + jnp.exp(-jnp.abs(z)))
    lw_ref[...] = -jnp.exp(-softplus - 0.5)
    a = jax.nn.sigmoid(a0_ref[...] + al_ref[...])
    kk = k * kkp_ref[...]
    kk = kk * lax.rsqrt(jnp.maximum(_seg_sum(kk * kk, ones_blk), 1e-24))
    kk_ref[...] = kk
    b_ref[...] = kk * a
    kh_ref[...] = k * (1.0 + (a - 1.0) * ka_ref[...])


def wkv_prep(k, wl, al, w0, a0, k_k, k_a):
    b, t, d = k.shape
    tt = _row_tile(t, 256)
    row = pl.BlockSpec((None, tt, d), lambda bi, i: (bi, i, 0))
    vec = pl.BlockSpec((1, d), lambda bi, i: (0, 0))
    return pl.pallas_call(
        _wkv_prep_kernel,
        grid=(b, t // tt),
        in_specs=[row] * 3 + [vec] * 4,
        out_specs=[row] * 4,
        out_shape=[jax.ShapeDtypeStruct((b, t, d), F32)] * 4,
        compiler_params=_cparams(("arbitrary", "arbitrary")),
        name="wkv_prep",
    )(k, wl, al, *[x.reshape(1, d) for x in (w0, a0, k_k, k_a)])


def _wkv_chunk_kernel(r_ref, lw_ref, kh_ref, v_ref, kk_ref, b_ref, lwt_ref, kht_ref, bt_ref, h0_ref,
                      y_ref, h_ref, *, n_chunks):
    c_ = WKV_CHUNK

    @pl.when(pl.program_id(2) == 0)
    def _():
        h_ref[...] = h0_ref[...]

    row = lax.broadcasted_iota(I32, (c_, c_), 0)
    col = lax.broadcasted_iota(I32, (c_, c_), 1)
    tri_incl = col <= row
    tri_strict = col < row
    l_incl = jnp.where(tri_incl, 1.0, 0.0).astype(BF16)
    u_incl = jnp.where(row <= col, 1.0, 0.0).astype(BF16)
    lane_head = lax.broadcasted_iota(I32, (1, LANES), 1) // RW_HEAD
    row_head = lax.broadcasted_iota(I32, (LANES, 1), 0) // RW_HEAD
    n_heads = LANES // RW_HEAD
    chunks = range(n_chunks)
    items = [(c, hh) for c in chunks for hh in range(n_heads)]
    ch = []
    for c in chunks:
        rs = slice(c * c_, (c + 1) * c_)
        lw, r, kh, v, kk, b = lw_ref[rs, :], r_ref[rs, :], kh_ref[rs, :], v_ref[rs, :], kk_ref[rs, :], b_ref[rs, :]
        cum = sum(_dot(l_incl, part) for part in _split3(lw))
        e_neg = jnp.exp(-cum)
        bt = b * e_neg
        kt = kh * e_neg
        cumt = sum(_dot(part, u_incl) for part in _split3(lwt_ref[c]))
        dt = jnp.exp(cumt[:, c_ - 1:c_] - cumt)
        ch.append(dict(
            kkt=kk * jnp.exp(cum - lw), rt=r * jnp.exp(cum),
            y1=jnp.concatenate([bt, kt], axis=0).astype(BF16), v_bf=v.astype(BF16),
            g_col=jnp.exp(cumt[:, c_ - 1:c_]),
            bht=bt_ref[c] * dt, kht=kht_ref[c] * dt))
    it = {}
    for c, hh in items:
        m = lane_head == hh
        kkt_h = jnp.where(m, ch[c]["kkt"], 0.0)
        rt_h = jnp.where(m, ch[c]["rt"], 0.0)
        gmat = _dot_nt(jnp.concatenate([kkt_h, rt_h], axis=0).astype(BF16), ch[c]["y1"])
        it[c, hh] = dict(
            kkt_h=kkt_h, rt_h=rt_h,
            a_b=jnp.where(tri_strict, gmat[:c_, :c_], 0.0).astype(BF16),
            a_k=jnp.where(tri_strict, gmat[:c_, c_:], 0.0).astype(BF16),
            m_b=jnp.where(tri_incl, gmat[c_:, :c_], 0.0).astype(BF16),
            m_k=jnp.where(tri_incl, gmat[c_:, c_:], 0.0).astype(BF16))
    for c, hh in items:
        d_ = it[c, hh]
        d_["x"] = jnp.concatenate([d_["kkt_h"], _dot(d_["a_k"], ch[c]["v_bf"])], axis=1)
        d_["pows"] = [d_["a_b"]]
    n_levels = c_.bit_length() - 1
    for _ in range(n_levels - 1):
        for key in items:
            pows = it[key]["pows"]
            pows.append(_dot(pows[-1], pows[-1]).astype(BF16))
    eye_c = jnp.where(row == col, 1.0, 0.0)
    for key in items:
        it[key]["t"] = eye_c - it[key]["a_b"].astype(F32)
    for level in range(1, n_levels):
        for key in items:
            d_ = it[key]
            d_["t"] = d_["t"] + _dot(d_["t"].astype(BF16), d_["pows"][level])
    hst = h_ref[...]
    for c in chunks:
        for hh in range(n_heads):
            d_ = it[c, hh]
            m = lane_head == hh
            rmask = row_head == hh
            x_bf = _dot(d_["t"].astype(BF16), d_["x"].astype(BF16)).astype(BF16)
            mbx = _dot(d_["m_b"], x_bf)
            d_["rw"] = d_["rt_h"] - mbx[:, :LANES]
            d_["y0"] = jnp.where(m, _dot(d_["m_k"], ch[c]["v_bf"]) - mbx[:, LANES:], 0.0)
            bx = _dot(jnp.where(rmask, ch[c]["bht"], 0.0).astype(BF16), x_bf)
            d_["pm"] = -bx[:, :LANES]
            d_["qm"] = jnp.where(m, _dot(jnp.where(rmask, ch[c]["kht"], 0.0).astype(BF16), ch[c]["v_bf"])
                                 - bx[:, LANES:], 0.0)
        parts = [it[c, hh] for hh in range(n_heads)]
        h_bf = hst.astype(BF16)
        y_ref[c * c_:(c + 1) * c_, :] = (_dot(sum(p_["rw"] for p_ in parts).astype(BF16), h_bf)
                                         + sum(p_["y0"] for p_ in parts))
        hst = (ch[c]["g_col"] * hst + _dot(sum(p_["pm"] for p_ in parts).astype(BF16), h_bf)
               + sum(p_["qm"] for p_ in parts))
    h_ref[...] = hst


def wkv_chunked(r, lw, kh, v, kk, b_, h0):
    bsz, t, d = r.shape
    n_pairs = d // LANES
    tile = WKV_TILE if t % WKV_TILE == 0 else WKV_TILE_MIN
    per_tile = tile // WKV_CHUNK

    def chunk_major_t(x):
        return jnp.swapaxes(x.reshape(bsz, t // WKV_CHUNK, WKV_CHUNK, d), 2, 3)

    row = pl.BlockSpec((None, tile, LANES), lambda bi, p, i: (bi, i, p))
    row_t = pl.BlockSpec((None, per_tile, LANES, WKV_CHUNK), lambda bi, p, i: (bi, i, p, 0))
    st = pl.BlockSpec((None, None, LANES, LANES), lambda bi, p, i: (bi, p, 0, 0))
    return pl.pallas_call(
        functools.partial(_wkv_chunk_kernel, n_chunks=per_tile),
        grid=(bsz, n_pairs, t // tile),
        in_specs=[row] * 6 + [row_t] * 3 + [st],
        out_specs=[row, st],
        out_shape=[jax.ShapeDtypeStruct((bsz, t, d), F32),
                   jax.ShapeDtypeStruct((bsz, n_pairs, LANES, LANES), F32)],
        compiler_params=_cparams(("arbitrary", "arbitrary", "arbitrary")),
        name="wkv_chunked",
    )(r, lw, kh, v, kk, b_, chunk_major_t(lw), chunk_major_t(kh), chunk_major_t(b_), h0)


def _wkv_post_kernel(y_ref, r_ref, kh_ref, v_ref, g_ref, rk_ref, lg_ref, lb_ref, o_ref):
    rb = lax.broadcasted_iota(I32, (LANES, LANES), 0) // RW_HEAD
    cb = lax.broadcasted_iota(I32, (LANES, LANES), 1) // RW_HEAD
    ones_blk = jnp.where(rb == cb, 1.0, 0.0).astype(BF16)
    y = y_ref[...]
    inv_n = 1.0 / RW_HEAD
    yc = y - _seg_sum(y, ones_blk) * inv_n
    var = _seg_sum(yc * yc, ones_blk) * inv_n
    y = yc * lax.rsqrt(var + LNX_EPS) * lg_ref[...] + lb_ref[...]
    y = y + _seg_sum(r_ref[...] * kh_ref[...] * rk_ref[...], ones_blk) * v_ref[...]
    o_ref[...] = (y * g_ref[...]).astype(o_ref.dtype)


def wkv_post(y, r, kh, v, g, r_k, lnx_g, lnx_b):
    b, t, d = y.shape
    tt = _row_tile(t, 256)
    row = pl.BlockSpec((None, tt, d), lambda bi, i: (bi, i, 0))
    vec = pl.BlockSpec((1, d), lambda bi, i: (0, 0))
    return pl.pallas_call(
        _wkv_post_kernel,
        grid=(b, t // tt),
        in_specs=[row] * 5 + [vec] * 3,
        out_specs=row,
        out_shape=jax.ShapeDtypeStruct((b, t, d), BF16),
        compiler_params=_cparams(("arbitrary", "arbitrary")),
        name="wkv_post",
    )(y, r, kh, v, g, *[x.reshape(1, d) for x in (r_k, lnx_g, lnx_b)])


def _state_to_blockdiag(s):
    st = jnp.swapaxes(s, 2, 3)
    st = st.reshape(s.shape[0], s.shape[1] // 2, 2, RW_HEAD, RW_HEAD)
    z = jnp.zeros_like(st[:, :, 0])
    return jnp.concatenate([jnp.concatenate([st[:, :, 0], z], axis=-1),
                            jnp.concatenate([z, st[:, :, 1]], axis=-1)], axis=-2)


def _state_from_blockdiag(hbd):
    b, p = hbd.shape[:2]
    x = hbd.reshape(b, p, 2, RW_HEAD, 2, RW_HEAD)
    diag = jnp.stack([x[:, :, 0, :, 0, :], x[:, :, 1, :, 1, :]], axis=2)
    return jnp.swapaxes(diag.reshape(b, 2 * p, RW_HEAD, RW_HEAD), 2, 3)


def _state_to_kernel_layout(s):
    b, h, n, _ = s.shape
    return jnp.transpose(s, (0, 2, 1, 3)).reshape(b, n, h * n)


def _state_from_kernel_layout(s, heads):
    b, n, _ = s.shape
    return jnp.transpose(s.reshape(b, n, heads, n), (0, 2, 1, 3))


def _run_group(x, mod_all, t_real, positions, sample_ctx, weights):
    (norm_g, ffn_w1, ffn_w3, ffn_w2, att_w_in, att_w_out, q_norm_g, k_norm_g, idx_k_norm_g,
     conv_w, conv_b, conv_ln_g, conv_ln_b, rw_mu, rw_w0, rw_w1, rw_w2, rw_a0, rw_a1, rw_a2, rw_g1, rw_g2,
     rw_k_k, rw_k_a, rw_r_k, rw_wr, rw_wk, rw_wv, rw_wo, rw_lnx_g, rw_lnx_b) = weights
    b, t, d = x.shape
    depth = norm_g.shape[0]
    aw = att_w_out.shape[1] - conv_w.shape[2]
    att_heads = aw // LANES
    cc = conv_w.shape[2]
    conv_width = conv_w.shape[1]
    idx_dim = idx_k_norm_g.shape[1]
    iq_w = IDX_HEADS * idx_dim
    rw_heads = d // RW_HEAD
    tables = _rope_tables(positions, LANES) + _rope_tables(positions, idx_dim)

    outs = dict(k=[], v=[], ik=[], conv=[], shift=[], wkv=[])
    h = resid_norm(x, mod_all[0], g=norm_g[0, 0], shift_idx=0, scale_idx=1, emit_x=False, h_dtype=BF16)
    for l in range(depth):
        i = l // 2
        mod = mod_all[l]
        even = l % 2 == 0
        y = ffn(h, ffn_w1, ffn_w3, ffn_w2, l, 0)
        x, h = resid_norm(x, mod, y=y, gate_idx=2, coef=HALF_STEP, g=norm_g[l, 1], shift_idx=3, scale_idx=4,
                          h_dtype=BF16 if even else F32)
        if even:
            qkvi = matmul(h, att_w_in, lead=i, n_cols=3 * aw + iq_w)
            w_small = jnp.pad(att_w_in[i][:, 3 * aw + iq_w:3 * aw + iq_w + idx_dim + IDX_HEADS],
                              ((0, 0), (0, LANES - idx_dim - IDX_HEADS)))
            small = matmul(h, w_small)
            u = matmul(h, att_w_in[i][:, 3 * aw + iq_w + idx_dim + IDX_HEADS:])
            q_bf, k_f, k_bf, v_bf, iq_bf, small, ikd_bf, glu = even_post(
                qkvi, small, u, tables, q_norm_g[i], k_norm_g[i], idx_k_norm_g[i], att_heads, idx_dim)
            if sample_ctx is None:
                att = dsa_prompt(q_bf, iq_bf, small, k_bf, v_bf, ikd_bf, idx_dim)
                hist = jnp.zeros((b, HIST_ROWS, cc), F32)
                outs["conv"].append(glu[:, t - (conv_width - 1):])
            else:
                pt = sample_ctx["page_table"]
                scores = sample_idx_scores(pt, iq_bf, small, sample_ctx["cache_idx_k"], i)
                pad_rows = lambda a: jnp.pad(a, ((0, 0), (0, LANES - t), (0, 0)))
                att = sample_attention(pt, scores, q_bf, iq_bf, small, pad_rows(ikd_bf), k_f,
                                       qkvi[:, :, 2 * aw:3 * aw], sample_ctx["cache_k"], sample_ctx["cache_v"],
                                       i, t_real, idx_dim)
                state = sample_ctx["state_conv"][i]
                hist = jnp.pad(state, ((0, 0), (HIST_ROWS - (conv_width - 1), 0), (0, 0)))
                outs["conv"].append(jnp.concatenate([state, glu[:, :t_real]], axis=1)[:, -(conv_width - 1):])
            conv_y = conv_branch(glu, hist, conv_w[i], conv_b[i], conv_ln_g[i], conv_ln_b[i])
            mixed = matmul(jnp.concatenate([att, conv_y], axis=-1), att_w_out, lead=i)
            outs["k"].append(k_f[:, :t_real].reshape(b, t_real, att_heads, LANES))
            outs["v"].append(qkvi[:, :t_real, 2 * aw:3 * aw].reshape(b, t_real, att_heads, LANES))
            outs["ik"].append(small[:, :t_real, :idx_dim])
        else:
            if sample_ctx is None:
                shift_prev = jnp.zeros((b, d), F32)
                s_init = jnp.zeros((b, rw_heads, RW_HEAD, RW_HEAD), F32)
            else:
                shift_prev = sample_ctx["state_shift"][i]
                s_init = sample_ctx["state_wkv"][i]
            xs = rw_mix(h, shift_prev, rw_mu[i])
            r = matmul(xs[0], rw_wr, lead=i)
            wl = lora(xs[1], rw_w1, rw_w2, i, "tanh")
            k = matmul(xs[2], rw_wk, lead=i)
            v = matmul(xs[3], rw_wv, lead=i)
            al = lora(xs[4], rw_a1, rw_a2, i, "none")
            g = lora(xs[5], rw_g1, rw_g2, i, "sigmoid")
            if t == t_real and t % WKV_TILE_MIN == 0:
                lw, kk, b_, kh = wkv_prep(k, wl, al, rw_w0[i], rw_a0[i], rw_k_k[i], rw_k_a[i])
                y, h_fin = wkv_chunked(r, lw, kh, v, kk, b_, _state_to_blockdiag(s_init))
                yg = wkv_post(y, r, kh, v, g, rw_r_k[i], rw_lnx_g[i], rw_lnx_b[i])
                s_out = _state_from_blockdiag(h_fin)
            else:
                yg, s_fin = wkv(r, k, v, wl, al, g, rw_w0[i], rw_a0[i], rw_k_k[i], rw_k_a[i], rw_r_k[i],
                                rw_lnx_g[i], rw_lnx_b[i], _state_to_kernel_layout(s_init), t_real)
                s_out = _state_from_kernel_layout(s_fin, rw_heads)
            mixed = matmul(yg, rw_wo, lead=i)
            outs["shift"].append(h[:, t_real - 1])
            outs["wkv"].append(s_out)
        x, h = resid_norm(x, mod, y=mixed, gate_idx=5, coef=1.0, g=norm_g[l, 2], shift_idx=6, scale_idx=7,
                          h_dtype=BF16)
        y = ffn(h, ffn_w1, ffn_w3, ffn_w2, l, 1)
        if l + 1 < depth:
            x, h = resid_norm(x, mod, y=y, gate_idx=8, coef=HALF_STEP, g=norm_g[l + 1, 0],
                              shift_idx=0, scale_idx=1, h_dtype=BF16, mod_norm=mod_all[l + 1])
        else:
            x = resid_norm(x, mod, y=y, gate_idx=8, coef=HALF_STEP)
    return x[:, :t_real], outs


def kernel(x_prompt, x_sample, cache_k, cache_v, cache_idx_k, state_conv, state_shift, state_wkv, page_table, c_prompt, c_sample, norm_g, ada_w, ada_b, ffn_w1, ffn_w3, ffn_w2, att_w_in, att_w_out, q_norm_g, k_norm_g, idx_k_norm_g, conv_w, conv_b, conv_ln_g, conv_ln_b, rw_mu, rw_w0, rw_w1, rw_w2, rw_a0, rw_a1, rw_a2, rw_g1, rw_g2, rw_k_k, rw_k_a, rw_r_k, rw_wr, rw_wk, rw_wv, rw_wo, rw_lnx_g, rw_lnx_b):
    bp, tp, d = x_prompt.shape
    bs, ts, _ = x_sample.shape
    depth = norm_g.shape[0]
    past = page_table.shape[1] * cache_k.shape[2]

    n_c = bp + bs
    c_rows = -(-n_c // 16) * 16
    c_all = jnp.pad(jnp.concatenate([c_prompt, c_sample], axis=0), ((0, c_rows - n_c), (0, 0)))
    mod = adaln_all(c_all, ada_w, ada_b).reshape(depth, c_rows, N_MOD, 1, d)
    mod_p, mod_s = mod[:, :bp], mod[:, bp:n_c]

    weights = (norm_g, ffn_w1, ffn_w3, ffn_w2, att_w_in, att_w_out, q_norm_g, k_norm_g, idx_k_norm_g,
               conv_w, conv_b, conv_ln_g, conv_ln_b, rw_mu, rw_w0, rw_w1, rw_w2, rw_a0, rw_a1, rw_a2,
               rw_g1, rw_g2, rw_k_k, rw_k_a, rw_r_k.reshape(rw_r_k.shape[0], -1), rw_wr, rw_wk, rw_wv, rw_wo,
               rw_lnx_g, rw_lnx_b)

    yp, op = _run_group(x_prompt, mod_p, tp, np.arange(tp), None, weights)
    xs_pad = jnp.pad(x_sample, ((0, 0), (0, SAMPLE_T_PAD - ts), (0, 0)))
    sample_ctx = dict(page_table=page_table, cache_k=cache_k, cache_v=cache_v, cache_idx_k=cache_idx_k,
                      state_conv=state_conv, state_shift=state_shift, state_wkv=state_wkv)
    ys, os_ = _run_group(xs_pad, mod_s, ts, past + np.arange(SAMPLE_T_PAD), sample_ctx, weights)

    st = lambda xs: jnp.stack(xs)
    return (yp, ys,
            st(op["k"]), st(op["v"]), st(op["ik"]), st(op["conv"]), st(op["shift"]), st(op["wkv"]),
            st(os_["k"]), st(os_["v"]), st(os_["ik"]), st(os_["conv"]), st(os_["shift"]), st(os_["wkv"]))
```

```python
import functools

import numpy as np
import jax
import jax.numpy as jnp
from jax import lax
from jax.experimental import pallas as pl
from jax.experimental.pallas import tpu as pltpu

F32 = jnp.float32
BF16 = jnp.bfloat16
I32 = jnp.int32

ROPE_THETA = 10000.0
NORM_EPS = 1e-6
LN_EPS = 1e-5
LNX_EPS = 64e-5
N_MOD = 9
HALF_STEP = 0.5
TOPK_MAX = 256
QBLOCK = 128
IDX_HEADS = 16

LANES = 128
SUBLANES = 8
VMEM_LIMIT_MB = 56
NEG_BIG = -1e30
INT_MIN = -(2 ** 31)

SAMPLE_T_PAD = 8
PAGES_PER_STEP = 8
IDX_PAGES_PER_STEP = 8


def _cparams(sem, vmem_mb=VMEM_LIMIT_MB):
    return pltpu.CompilerParams(dimension_semantics=sem, vmem_limit_bytes=vmem_mb * 1024 * 1024)


def _silu(x):
    return x * jax.nn.sigmoid(x)


def _dot(a, b):
    return jnp.dot(a, b, preferred_element_type=F32)


def _dot_nt(a, b):
    return lax.dot_general(a, b, (((1,), (1,)), ((), ())), preferred_element_type=F32)


def _row_tile(t, target):
    return t if t <= target else target


def _adaln_kernel(c_ref, w_ref, b_ref, o_ref):
    sc = _silu(c_ref[...]).astype(BF16)
    o_ref[...] = _dot(sc, w_ref[...].astype(BF16)) + b_ref[...]


def adaln_all(c_all, ada_w, ada_b):
    depth, d, n = ada_w.shape
    rows = c_all.shape[0]
    tn = 1024
    return pl.pallas_call(
        _adaln_kernel,
        grid=(depth, n // tn),
        in_specs=[
            pl.BlockSpec((rows, d), lambda l, j: (0, 0)),
            pl.BlockSpec((None, d, tn), lambda l, j: (l, 0, j)),
            pl.BlockSpec((None, 1, tn), lambda l, j: (l, 0, j)),
        ],
        out_specs=pl.BlockSpec((None, rows, tn), lambda l, j: (l, 0, j)),
        out_shape=jax.ShapeDtypeStruct((depth, rows, n), F32),
        compiler_params=_cparams(("arbitrary", "arbitrary")),
        name="adaln",
    )(c_all, ada_w, ada_b.reshape(depth, 1, n))


def _resid_norm_kernel(*refs, has_y, coef, emit_x, emit_h):
    refs = list(refs)
    x = refs.pop(0)[...]
    if has_y:
        y = refs.pop(0)[...]
        gate = refs.pop(0)[...]
        x = x + (coef * gate) * y
    if emit_h:
        g = refs.pop(0)[...]
        shift = refs.pop(0)[...]
        scale = refs.pop(0)[...]
    if emit_x:
        refs.pop(0)[...] = x
    if emit_h:
        h_ref = refs.pop(0)
        ms = jnp.mean(x * x, axis=-1, keepdims=True)
        h = x * lax.rsqrt(ms + NORM_EPS) * g
        h_ref[...] = (h * (1.0 + scale) + shift).astype(h_ref.dtype)


def resid_norm(x, mod, *, y=None, gate_idx=None, coef=1.0, g=None, shift_idx=None, scale_idx=None,
               emit_x=True, h_dtype=None, mod_norm=None):
    mod_norm = mod if mod_norm is None else mod_norm
    b, t, d = x.shape
    tt = _row_tile(t, 256)
    has_y = y is not None
    emit_h = h_dtype is not None
    row = pl.BlockSpec((None, tt, d), lambda bi, i: (bi, i, 0))

    def mod_spec(idx):
        return pl.BlockSpec((None, None, 1, d), lambda bi, i: (bi, idx, 0, 0))

    args, specs = [x], [row]
    if has_y:
        args += [y, mod]
        specs += [row, mod_spec(gate_idx)]
    if emit_h:
        args += [g.reshape(1, d), mod_norm, mod_norm]
        specs += [pl.BlockSpec((1, d), lambda bi, i: (0, 0)), mod_spec(shift_idx), mod_spec(scale_idx)]
    out_shape, out_specs = [], []
    if emit_x:
        out_shape.append(jax.ShapeDtypeStruct((b, t, d), F32))
        out_specs.append(row)
    if emit_h:
        out_shape.append(jax.ShapeDtypeStruct((b, t, d), h_dtype))
        out_specs.append(row)
    outs = pl.pallas_call(
        functools.partial(_resid_norm_kernel, has_y=has_y, coef=coef, emit_x=emit_x, emit_h=emit_h),
        grid=(b, t // tt),
        in_specs=specs,
        out_specs=out_specs,
        out_shape=out_shape,
        compiler_params=_cparams(("arbitrary", "arbitrary")),
        name="resid_norm",
    )(*args)
    return outs if len(outs) > 1 else outs[0]


def _ffn_kernel(h_ref, w1_ref, w3_ref, w2_ref, o_ref):
    f = pl.program_id(1)
    h = h_ref[...]
    a = _dot(h, w1_ref[...].astype(BF16))
    b = _dot(h, w3_ref[...].astype(BF16))
    z = (_silu(a) * b).astype(BF16)

    @pl.when(f == 0)
    def _():
        o_ref[...] = jnp.zeros(o_ref.shape, o_ref.dtype)

    o_ref[...] += _dot(z, w2_ref[...].astype(BF16))


def ffn(h, w1, w3, w2, layer, slot):
    b, t, d = h.shape
    m = b * t
    d_ff = w1.shape[-1]
    tm = 1024 if m >= 1024 else m
    tf = 512 if (w1.dtype == BF16 or m < 1024) else 256
    out = pl.pallas_call(
        _ffn_kernel,
        grid=(m // tm, d_ff // tf),
        in_specs=[
            pl.BlockSpec((tm, d), lambda i, f: (i, 0)),
            pl.BlockSpec((None, None, d, tf), lambda i, f: (layer, slot, 0, f)),
            pl.BlockSpec((None, None, d, tf), lambda i, f: (layer, slot, 0, f)),
            pl.BlockSpec((None, None, tf, d), lambda i, f: (layer, slot, f, 0)),
        ],
        out_specs=pl.BlockSpec((tm, d), lambda i, f: (i, 0)),
        out_shape=jax.ShapeDtypeStruct((m, d), F32),
        compiler_params=_cparams(("arbitrary", "arbitrary")),
        name="ffn",
    )(h.reshape(m, d), w1, w3, w2)
    return out.reshape(b, t, d)


def _mm_kernel(x_ref, w_ref, o_ref):
    o_ref[...] = _dot(x_ref[...], w_ref[...].astype(BF16)).astype(o_ref.dtype)


def matmul(x, w, lead=None, n_cols=None, out_dtype=F32):
    b, t, k = x.shape
    m = b * t
    n = w.shape[-1] if n_cols is None else n_cols
    tm = 1024 if m >= 1024 else m
    tn = 1024 if n % 1024 == 0 else (512 if n % 512 == 0 else n)
    if lead is None:
        w_spec = pl.BlockSpec((k, tn), lambda i, j: (0, j))
    else:
        w_spec = pl.BlockSpec((None, k, tn), lambda i, j: (lead, 0, j))
    out = pl.pallas_call(
        _mm_kernel,
        grid=(m // tm, n // tn),
        in_specs=[pl.BlockSpec((tm, k), lambda i, j: (i, 0)), w_spec],
        out_specs=pl.BlockSpec((tm, tn), lambda i, j: (i, j)),
        out_shape=jax.ShapeDtypeStruct((m, n), out_dtype),
        compiler_params=_cparams(("arbitrary", "arbitrary")),
        name="matmul",
    )(x.reshape(m, k), w)
    return out.reshape(b, t, n)


def _lora_kernel(x_ref, a_ref, b_ref, o_ref, *, act):
    h = _dot(x_ref[...], a_ref[...].astype(BF16))
    if act == "tanh":
        h = jnp.tanh(h)
    elif act == "sigmoid":
        h = jax.nn.sigmoid(h)
    o_ref[...] = _dot(h.astype(BF16), b_ref[...].astype(BF16))


def lora(x, a, bmat, layer, act):
    b, t, d = x.shape
    m = b * t
    r = a.shape[-1]
    n = bmat.shape[-1]
    tm = 512 if m >= 512 else m
    out = pl.pallas_call(
        functools.partial(_lora_kernel, act=act),
        grid=(m // tm,),
        in_specs=[
            pl.BlockSpec((tm, d), lambda i: (i, 0)),
            pl.BlockSpec((None, d, r), lambda i: (layer, 0, 0)),
            pl.BlockSpec((None, r, n), lambda i: (layer, 0, 0)),
        ],
        out_specs=pl.BlockSpec((tm, n), lambda i: (i, 0)),
        out_shape=jax.ShapeDtypeStruct((m, n), F32),
        compiler_params=_cparams(("arbitrary",)),
        name="lora",
    )(x.reshape(m, d), a, bmat)
    return out.reshape(b, t, n)


def _rope_tables(positions, head_dim):
    half = head_dim // 2
    inv = ROPE_THETA ** (-np.arange(half, dtype=np.float64) / half)
    ang = np.asarray(positions, np.float64)[:, None] * inv[None, :]
    cos = np.concatenate([np.cos(ang), np.cos(ang)], axis=-1)
    sin = np.concatenate([-np.sin(ang), np.sin(ang)], axis=-1)
    reps = LANES // head_dim
    return (jnp.asarray(np.tile(cos, (1, reps)), F32), jnp.asarray(np.tile(sin, (1, reps)), F32))


def _rope128(x, cos, sin):
    return x * cos + pltpu.roll(x, 64, 1) * sin


def _rope64(x, cos, sin, lane):
    first_half = (lane % 64) < 32
    partner = jnp.where(first_half, pltpu.roll(x, 96, 1), pltpu.roll(x, 32, 1))
    return x * cos + partner * sin


def _even_post_kernel(q_ref, k_ref, v_ref, iq_ref, small_ref, u_ref,
                      c128_ref, s128_ref, c64_ref, s64_ref, qg_ref, kg_ref, ikg_ref,
                      qo_ref, kf_ref, kb_ref, vb_ref, iqo_ref, smallo_ref, ikd_ref, uo_ref,
                      *, att_heads, idx_dim, idx_heads):
    c128, s128 = c128_ref[...], s128_ref[...]
    c64, s64 = c64_ref[...], s64_ref[...]
    tt = c128.shape[0]
    lane = lax.broadcasted_iota(I32, (tt, LANES), 1)
    qg, kg = qg_ref[...], kg_ref[...]
    for h in range(att_heads):
        sl = slice(h * LANES, (h + 1) * LANES)
        q = q_ref[:, sl]
        q = q * lax.rsqrt(jnp.mean(q * q, axis=-1, keepdims=True) + NORM_EPS) * qg
        qo_ref[:, sl] = _rope128(q, c128, s128).astype(BF16)
        k = k_ref[:, sl]
        k = k * lax.rsqrt(jnp.mean(k * k, axis=-1, keepdims=True) + NORM_EPS) * kg
        k = _rope128(k, c128, s128)
        kf_ref[:, sl] = k
        kb_ref[:, sl] = k.astype(BF16)
        vb_ref[:, sl] = v_ref[:, sl].astype(BF16)
    for p in range(idx_heads * idx_dim // LANES):
        sl = slice(p * LANES, (p + 1) * LANES)
        iqo_ref[:, sl] = _rope64(iq_ref[:, sl], c64, s64, lane).astype(BF16)
    small = small_ref[...]
    is_ik = lane < idx_dim
    ik = jnp.where(is_ik, small, 0.0)
    ms = jnp.sum(ik * ik, axis=-1, keepdims=True) * (1.0 / idx_dim)
    ik = _rope64(ik * lax.rsqrt(ms + NORM_EPS) * ikg_ref[...], c64, s64, lane)
    ik = jnp.where(is_ik, ik, 0.0)
    smallo_ref[...] = jnp.where(is_ik, ik, small * (idx_heads ** -0.5))
    ikd_ref[...] = (ik + pltpu.roll(ik, 64, 1)).astype(BF16)
    cc = u_ref.shape[-1] // 2
    uo_ref[...] = u_ref[:, :cc] * jax.nn.sigmoid(u_ref[:, cc:])


def even_post(qkvi, small, u, tables, q_g, k_g, ik_g, att_heads, idx_dim):
    b, t, _ = qkvi.shape
    aw = att_heads * LANES
    iw = IDX_HEADS * idx_dim
    cc = u.shape[-1] // 2
    tt = _row_tile(t, 256)
    c128, s128, c64, s64 = tables

    def row(width, col=0):
        return pl.BlockSpec((None, tt, width), lambda bi, i: (bi, i, col))

    tab = pl.BlockSpec((tt, LANES), lambda bi, i: (i, 0))
    vec = pl.BlockSpec((1, LANES), lambda bi, i: (0, 0))
    ikg_pad = jnp.zeros((1, LANES), F32).at[0, :idx_dim].set(ik_g)
    assert aw == iw, "q/k/v/indexer-q column groups are addressed as equal-width blocks"
    return pl.pallas_call(
        functools.partial(_even_post_kernel, att_heads=att_heads, idx_dim=idx_dim, idx_heads=IDX_HEADS),
        grid=(b, t // tt),
        in_specs=[row(aw, 0), row(aw, 1), row(aw, 2), row(iw, 3), row(LANES), row(2 * cc),
                  tab, tab, tab, tab, vec, vec, vec],
        out_specs=[row(aw), row(aw), row(aw), row(aw), row(iw), row(LANES), row(LANES), row(cc)],
        out_shape=[
            jax.ShapeDtypeStruct((b, t, aw), BF16),
            jax.ShapeDtypeStruct((b, t, aw), F32),
            jax.ShapeDtypeStruct((b, t, aw), BF16),
            jax.ShapeDtypeStruct((b, t, aw), BF16),
            jax.ShapeDtypeStruct((b, t, iw), BF16),
            jax.ShapeDtypeStruct((b, t, LANES), F32),
            jax.ShapeDtypeStruct((b, t, LANES), BF16),
            jax.ShapeDtypeStruct((b, t, cc), F32),
        ],
        compiler_params=_cparams(("arbitrary", "arbitrary")),
        name="even_post",
    )(qkvi, qkvi, qkvi, qkvi, small, u, c128, s128, c64, s64,
      q_g.reshape(1, LANES), k_g.reshape(1, LANES), ikg_pad)


def _order_key(score):
    bits = pltpu.bitcast(score, I32)
    return jnp.where(bits < 0, bits ^ 0x7FFFFFFF, bits)


def _kth_largest_key(count_ge, top_k, shape):
    tau = jnp.where(count_ge(jnp.zeros(shape, I32)) >= top_k, 0, INT_MIN).astype(I32)

    def body(i, tau):
        cand = tau | jnp.left_shift(jnp.int32(1), 30 - i)
        return jnp.where(count_ge(cand) >= top_k, cand, tau)

    return lax.fori_loop(0, 31, body, tau)


def _idx_head_lhs(iq, h, lane):
    slab = iq[:, (h // 2) * LANES:(h // 2 + 1) * LANES]
    keep = (lane < 64) if h % 2 == 0 else (lane >= 64)
    return jnp.where(keep, slab, jnp.zeros_like(slab))


def _idx_weight(small, h, lane, idx_dim):
    return jnp.sum(jnp.where(lane == idx_dim + h, small, 0.0), axis=-1, keepdims=True)


KEY_CHUNK = 512


def _dsa_prompt_kernel(q_ref, iq_ref, small_t_ref, k_ref, v_t_ref, ikd_ref, o_ref, lhs_sc, key_sc,
                       *, top_k, att_heads, idx_dim):
    n = pl.program_id(1)
    qb = q_ref.shape[0]
    n_chunks = ((n + 1) * qb + KEY_CHUNK - 1) // KEY_CHUNK
    lane = lax.broadcasted_iota(I32, (qb, LANES), 1)
    iq = iq_ref[...]
    for h in range(IDX_HEADS):
        lhs_sc[h // 2, (h % 2) * qb:(h % 2 + 1) * qb, :] = _idx_head_lhs(iq, h, lane)
    small_t = small_t_ref[...]
    weights = [small_t[idx_dim + h:idx_dim + h + 1, :] * (idx_dim ** -0.5) for h in range(IDX_HEADS)]
    qpos = lax.broadcasted_iota(I32, (KEY_CHUNK, qb), 1) + n * qb
    krow = lax.broadcasted_iota(I32, (KEY_CHUNK, qb), 0)

    def chunk_rows(c):
        return pl.ds(pl.multiple_of(c * KEY_CHUNK, KEY_CHUNK), KEY_CHUNK)

    def score_chunk(c, carry):
        ikd = ikd_ref[chunk_rows(c), :]
        score = jnp.zeros((KEY_CHUNK, qb), F32)
        for pair in range(IDX_HEADS // 2):
            d = jnp.maximum(_dot_nt(ikd, lhs_sc[pair]), 0.0)
            score = score + weights[2 * pair] * d[:, :qb] + weights[2 * pair + 1] * d[:, qb:]
        admissible = krow + c * KEY_CHUNK <= qpos
        key_sc[c] = jnp.where(admissible, _order_key(score), INT_MIN)
        return carry

    lax.fori_loop(0, n_chunks, score_chunk, 0)

    def count_ge(cand):
        def add_chunk(c, acc):
            hit = (key_sc[c] >= cand).astype(I32)
            return acc + jnp.sum(hit.reshape(KEY_CHUNK // SUBLANES, SUBLANES, qb), axis=0)

        part = lax.fori_loop(0, n_chunks, add_chunk, jnp.zeros((SUBLANES, qb), I32))
        return jnp.sum(part, axis=0, keepdims=True)

    tau = jnp.maximum(_kth_largest_key(count_ge, top_k, (1, qb)), INT_MIN + 1)
    scale_log2e = LANES ** -0.5 * float(np.log2(np.e))
    heads = [slice(h * LANES, (h + 1) * LANES) for h in range(att_heads)]

    def attend_chunk(c, carry):
        sel = key_sc[c] >= tau
        out = []
        for sl, (m_old, l_old, acc) in zip(heads, carry):
            s = jnp.where(sel, _dot_nt(k_ref[chunk_rows(c), sl], q_ref[:, sl]), NEG_BIG)
            m_new = jnp.maximum(m_old, jnp.max(s, axis=0, keepdims=True))
            alpha = jnp.exp2((m_old - m_new) * scale_log2e)
            p = jnp.exp2((s - m_new) * scale_log2e)
            l_new = alpha * l_old + jnp.sum(p, axis=0, keepdims=True)
            out.append((m_new, l_new, alpha * acc + _dot(v_t_ref[c, sl, :], p.astype(BF16))))
        return tuple(out)

    init = tuple((jnp.full((1, qb), NEG_BIG, F32), jnp.zeros((1, qb), F32), jnp.zeros((LANES, qb), F32))
                 for _ in heads)
    for sl, (_, l_fin, acc) in zip(heads, lax.fori_loop(0, n_chunks, attend_chunk, init)):
        o_ref[:, sl] = (acc / l_fin).T.astype(o_ref.dtype)


def dsa_prompt(q_bf, iq_bf, small, k_bf, v_bf, ikd_bf, idx_dim):
    b, t, aw = q_bf.shape
    assert t % KEY_CHUNK == 0 and KEY_CHUNK % QBLOCK == 0
    top_k = min(TOPK_MAX, t // 4)
    n_kc = t // KEY_CHUNK
    small_t = jnp.swapaxes(small, 1, 2)
    v_t = jnp.swapaxes(v_bf.reshape(b, n_kc, KEY_CHUNK, aw), 2, 3)
    qrow = lambda w: pl.BlockSpec((None, QBLOCK, w), lambda bi, n: (bi, n, 0))
    full = lambda w: pl.BlockSpec((None, t, w), lambda bi, n: (bi, 0, 0))
    return pl.pallas_call(
        functools.partial(_dsa_prompt_kernel, top_k=top_k, att_heads=aw // LANES, idx_dim=idx_dim),
        grid=(b, t // QBLOCK),
        in_specs=[qrow(aw), qrow(iq_bf.shape[-1]),
                  pl.BlockSpec((None, LANES, QBLOCK), lambda bi, n: (bi, 0, n)),
                  full(aw),
                  pl.BlockSpec((None, n_kc, aw, KEY_CHUNK), lambda bi, n: (bi, 0, 0, 0)),
                  full(LANES)],
        out_specs=qrow(aw),
        out_shape=jax.ShapeDtypeStruct((b, t, aw), BF16),
        scratch_shapes=[pltpu.VMEM((IDX_HEADS // 2, 2 * QBLOCK, LANES), BF16),
                        pltpu.VMEM((n_kc, KEY_CHUNK, QBLOCK), I32)],
        compiler_params=_cparams(("arbitrary", "arbitrary")),
        name="dsa_prompt",
    )(q_bf, iq_bf, small_t, k_bf, v_t, ikd_bf)


def _idx_lhs_all_heads(iq, lane):
    iq = iq.astype(F32)
    return jnp.concatenate([_idx_head_lhs(iq, h, lane) for h in range(IDX_HEADS)], axis=0).astype(BF16)


def _idx_scores(lhs, small, keys_dup, lane, idx_dim):
    d = _dot_nt(lhs, keys_dup) * (idx_dim ** -0.5)
    score = jnp.zeros((SAMPLE_T_PAD, keys_dup.shape[0]), F32)
    for h in range(IDX_HEADS):
        dh = d[h * SAMPLE_T_PAD:(h + 1) * SAMPLE_T_PAD, :]
        score = score + _idx_weight(small, h, lane, idx_dim) * jnp.maximum(dh, 0.0)
    return score


def _sample_idx_kernel(pt_ref, iq_ref, small_ref, *rest, idx_dim):
    page_refs, o_ref = rest[:-1], rest[-1]
    lane = lax.broadcasted_iota(I32, (SAMPLE_T_PAD, LANES), 1)
    keys = jnp.concatenate([r[...] for r in page_refs], axis=0)
    keys_dup = jnp.concatenate([keys, keys], axis=-1).astype(BF16)
    lhs = _idx_lhs_all_heads(iq_ref[...], lane)
    o_ref[...] = _idx_scores(lhs, small_ref[...], keys_dup, lane, idx_dim)


def sample_idx_scores(page_table, iq_bf, small, cache_idx_k, layer):
    b, n_pages = page_table.shape
    page, idx_dim = cache_idx_k.shape[-2:]
    g = IDX_PAGES_PER_STEP if n_pages % IDX_PAGES_PER_STEP == 0 else 1

    def page_spec(j):
        return pl.BlockSpec((None, None, page, idx_dim),
                            lambda bi, s, pt: (layer, pt[bi * n_pages + s * g + j], 0, 0))

    row = lambda w: pl.BlockSpec((None, SAMPLE_T_PAD, w), lambda bi, s, pt: (bi, 0, 0))
    grid_spec = pltpu.PrefetchScalarGridSpec(
        num_scalar_prefetch=1,
        grid=(b, n_pages // g),
        in_specs=[row(iq_bf.shape[-1]), row(LANES)] + [page_spec(j) for j in range(g)],
        out_specs=pl.BlockSpec((None, SAMPLE_T_PAD, g * page), lambda bi, s, pt: (bi, 0, s)),
    )
    return pl.pallas_call(
        functools.partial(_sample_idx_kernel, idx_dim=idx_dim),
        grid_spec=grid_spec,
        out_shape=jax.ShapeDtypeStruct((b, SAMPLE_T_PAD, n_pages * page), F32),
        compiler_params=_cparams(("arbitrary", "arbitrary")),
        name="sample_idx",
    )(page_table.reshape(-1), iq_bf, small, *([cache_idx_k] * g))


def _sample_attn_kernel(pt_ref, sc_all_ref, sc_ref, q_ref, iq_ref, small_ref, ikn_ref, kn_ref, vn_ref,
                        *rest, top_k, n_new, att_heads, idx_dim, n_groups):
    g = (len(rest) - 8) // 2
    k_pages, v_pages = rest[:g], rest[g:2 * g]
    o_ref, qrows_ref, erow_ref, tau_ref, keyn_ref, m_ref, l_ref, acc_ref = rest[2 * g:]
    s_idx = pl.program_id(1)
    page = k_pages[0].shape[0]
    scale = LANES ** -0.5
    assert att_heads == SUBLANES and att_heads * SAMPLE_T_PAD <= LANES and page <= LANES

    @pl.when(s_idx == 0)
    def _():
        lane = lax.broadcasted_iota(I32, (SAMPLE_T_PAD, LANES), 1)
        tok = lax.broadcasted_iota(I32, (SAMPLE_T_PAD, LANES), 0)
        lhs = _idx_lhs_all_heads(iq_ref[...], lane)
        sc_new = _idx_scores(lhs, small_ref[...], ikn_ref[...], lane, idx_dim)
        new_ok = (lane <= tok) & (lane < n_new)
        key_new = jnp.where(new_ok, _order_key(sc_new), INT_MIN)
        key_past = _order_key(sc_all_ref[...])

        def count_ge(cand):
            return (jnp.sum((key_past >= cand).astype(I32), axis=-1, keepdims=True)
                    + jnp.sum((key_new >= cand).astype(I32), axis=-1, keepdims=True))

        tau = _kth_largest_key(count_ge, top_k, (SAMPLE_T_PAD, 1))
        tau_ref[...] = jnp.broadcast_to(tau, (SAMPLE_T_PAD, LANES))
        keyn_ref[...] = key_new
        q = q_ref[...].astype(F32)
        q_rows = [q[:, h * LANES:(h + 1) * LANES] for h in range(att_heads)]
        q_rows.append(jnp.zeros((LANES - att_heads * SAMPLE_T_PAD, LANES), F32))
        qrows_ref[...] = jnp.concatenate(q_rows, axis=0).astype(BF16)
        r_tok = lax.broadcasted_iota(I32, erow_ref.shape, 0) // att_heads
        c_tok = lax.broadcasted_iota(I32, erow_ref.shape, 1)
        erow_ref[...] = jnp.where(r_tok == c_tok, 1.0, 0.0).astype(BF16)
        m_ref[...] = jnp.full(m_ref.shape, NEG_BIG, F32)
        l_ref[...] = jnp.zeros(l_ref.shape, F32)
        acc_ref[...] = jnp.zeros(acc_ref.shape, F32)

    tau = tau_ref[:, 0:1]
    rep = jnp.where(lax.broadcasted_iota(I32, (LANES, LANES), 0)
                    == lax.broadcasted_iota(I32, (LANES, LANES), 1) % SAMPLE_T_PAD, 1.0, 0.0).astype(BF16)
    head_match8 = (lax.broadcasted_iota(I32, (SUBLANES, LANES), 0)
                   == lax.broadcasted_iota(I32, (SUBLANES, LANES), 1) // SAMPLE_T_PAD)

    def masked_scores(sel8, keys):
        rows = keys.shape[0]
        n_tok = rows // att_heads
        sel_pad = jnp.concatenate([sel8.astype(F32), jnp.zeros((LANES - SAMPLE_T_PAD, LANES), F32)], axis=0)
        sel_rep = _dot(sel_pad.T.astype(BF16), rep)
        sel_rows = _dot(erow_ref[0:rows, :], sel_rep.astype(BF16))
        head_match = jnp.broadcast_to(head_match8[None], (n_tok, SUBLANES, LANES)).reshape(rows, LANES)
        mask = (sel_rows > 0.5) & head_match
        return jnp.where(mask, _dot_nt(keys, qrows_ref[...]) * scale, NEG_BIG)

    def attend(groups):
        scores = [masked_scores(sel8, keys) for sel8, keys, _ in groups]
        m_old = m_ref[0:1, :]
        m_new = m_old
        for s in scores:
            m_new = jnp.maximum(m_new, jnp.max(s, axis=0, keepdims=True))
        alpha = jnp.exp(m_old - m_new)
        l_new = alpha * l_ref[0:1, :]
        acc = alpha * acc_ref[...]
        for s, (_, _, vals) in zip(scores, groups):
            p = jnp.exp(s - m_new)
            l_new = l_new + jnp.sum(p, axis=0, keepdims=True)
            acc = acc + lax.dot_general(vals, p.astype(BF16), (((0,), (0,)), ((), ())),
                                        preferred_element_type=F32)
        l_ref[...] = jnp.broadcast_to(l_new, l_ref.shape)
        acc_ref[...] = acc
        m_ref[...] = jnp.broadcast_to(m_new, m_ref.shape)

    def as_rows(ref):
        return ref[...].reshape(ref.shape[0] * att_heads, LANES).astype(BF16)

    attend([(_order_key(sc_ref[:, j * page:(j + 1) * page]) >= tau, as_rows(k_pages[j]), as_rows(v_pages[j]))
            for j in range(g)])

    @pl.when(s_idx == n_groups - 1)
    def _():
        attend([(keyn_ref[...] >= tau, as_rows(kn_ref), as_rows(vn_ref))])
        out = (acc_ref[...] / l_ref[0:1, :]).T
        for h in range(att_heads):
            o_ref[:, h * LANES:(h + 1) * LANES] = (
                out[h * SAMPLE_T_PAD:(h + 1) * SAMPLE_T_PAD, :].astype(o_ref.dtype))


def sample_attention(page_table, scores, q_bf, iq_bf, small, ikn_pad, k_new, v_new,
                     cache_k, cache_v, layer, n_new, idx_dim):
    b, n_pages = page_table.shape
    page = cache_k.shape[2]
    aw = q_bf.shape[-1]
    att_heads = aw // LANES
    past = n_pages * page
    top_k = min(TOPK_MAX, (past + n_new) // 4)
    g = PAGES_PER_STEP if n_pages % PAGES_PER_STEP == 0 else 1
    n_groups = n_pages // g

    def page_spec(j):
        return pl.BlockSpec((None, None, page, att_heads, LANES),
                            lambda bi, s, pt: (layer, pt[bi * n_pages + s * g + j], 0, 0, 0))

    def per_seq(r, w):
        return pl.BlockSpec((None, r, w), lambda bi, s, pt: (bi, 0, 0))

    new_rows = pl.BlockSpec((None, SAMPLE_T_PAD, att_heads, LANES), lambda bi, s, pt: (bi, 0, 0, 0))
    grid_spec = pltpu.PrefetchScalarGridSpec(
        num_scalar_prefetch=1,
        grid=(b, n_groups),
        in_specs=[per_seq(SAMPLE_T_PAD, past),
                  pl.BlockSpec((None, SAMPLE_T_PAD, g * page), lambda bi, s, pt: (bi, 0, s)),
                  per_seq(SAMPLE_T_PAD, aw), per_seq(SAMPLE_T_PAD, iq_bf.shape[-1]),
                  per_seq(SAMPLE_T_PAD, LANES),
                  per_seq(LANES, LANES), new_rows, new_rows]
                 + [page_spec(j) for j in range(g)] * 2,
        out_specs=per_seq(SAMPLE_T_PAD, aw),
        scratch_shapes=[
            pltpu.VMEM((LANES, LANES), BF16),
            pltpu.VMEM((page * att_heads, LANES), BF16),
            pltpu.VMEM((SAMPLE_T_PAD, LANES), I32),
            pltpu.VMEM((SAMPLE_T_PAD, LANES), I32),
            pltpu.VMEM((SUBLANES, LANES), F32),
            pltpu.VMEM((SUBLANES, LANES), F32),
            pltpu.VMEM((LANES, LANES), F32),
        ],
    )
    return pl.pallas_call(
        functools.partial(_sample_attn_kernel, top_k=top_k, n_new=n_new, att_heads=att_heads,
                          idx_dim=idx_dim, n_groups=n_groups),
        grid_spec=grid_spec,
        out_shape=jax.ShapeDtypeStruct((b, SAMPLE_T_PAD, aw), BF16),
        compiler_params=_cparams(("arbitrary", "arbitrary")),
        name="sample_attn",
    )(page_table.reshape(-1), scores, scores, q_bf, iq_bf, small, ikn_pad,
      k_new.reshape(b, SAMPLE_T_PAD, att_heads, LANES), v_new.reshape(b, SAMPLE_T_PAD, att_heads, LANES),
      *([cache_k] * g), *([cache_v] * g))


HIST_ROWS = 32
CONV_ROWS = 32


def _conv_kernel(*refs, width, multi_tile):
    if multi_tile:
        u_ref, prev_ref, hist_ref, cw_ref, cb_ref, g_ref, b_ref, o_ref, buf = refs
    else:
        u_ref, hist_ref, cw_ref, cb_ref, g_ref, b_ref, o_ref, buf = refs
    tt = u_ref.shape[0]
    buf[HIST_ROWS:HIST_ROWS + tt, :] = u_ref[...]
    if multi_tile:
        buf[0:HIST_ROWS, :] = jnp.where(pl.program_id(1) == 0, hist_ref[...], prev_ref[...])
    else:
        buf[0:HIST_ROWS, :] = hist_ref[...]
    first = HIST_ROWS - (width - 1)
    rows = min(CONV_ROWS, tt)
    cb, g, b = cb_ref[...], g_ref[...], b_ref[...]
    for r0 in range(0, tt, rows):
        acc = jnp.zeros((rows, u_ref.shape[1]), F32)
        for j in range(width):
            acc = acc + buf[r0 + first + j:r0 + first + j + rows, :] * cw_ref[j:j + 1, :]
        y = acc + cb
        yc = y - jnp.mean(y, axis=-1, keepdims=True)
        var = jnp.mean(yc * yc, axis=-1, keepdims=True)
        y = yc * lax.rsqrt(var + LN_EPS) * g + b
        o_ref[r0:r0 + rows, :] = _silu(y).astype(o_ref.dtype)


def conv_branch(u, hist, cw, cb, ln_g, ln_b):
    b, t, c = u.shape
    width = cw.shape[0]
    tt = _row_tile(t, 256)
    multi_tile = t > tt
    row = pl.BlockSpec((None, tt, c), lambda bi, i: (bi, i, 0))
    vec = pl.BlockSpec((1, c), lambda bi, i: (0, 0))
    specs, args = [row], [u]
    if multi_tile:
        per = tt // HIST_ROWS
        specs.append(pl.BlockSpec((None, HIST_ROWS, c), lambda bi, i: (bi, jnp.maximum(i * per - 1, 0), 0)))
        args.append(u)
    specs += [pl.BlockSpec((None, HIST_ROWS, c), lambda bi, i: (bi, 0, 0)),
              pl.BlockSpec((width, c), lambda bi, i: (0, 0)), vec, vec, vec]
    args += [hist, cw, cb.reshape(1, c), ln_g.reshape(1, c), ln_b.reshape(1, c)]
    return pl.pallas_call(
        functools.partial(_conv_kernel, width=width, multi_tile=multi_tile),
        grid=(b, t // tt),
        in_specs=specs,
        out_specs=row,
        out_shape=jax.ShapeDtypeStruct((b, t, c), BF16),
        scratch_shapes=[pltpu.VMEM((HIST_ROWS + tt, c), F32)],
        compiler_params=_cparams(("arbitrary", "arbitrary")),
        name="conv_branch",
    )(*args)


def _rw_mix_kernel(h_ref, prev_ref, shift_ref, mu_ref, o_ref, buf):
    tt = h_ref.shape[0]
    h = h_ref[...]
    buf[SUBLANES:SUBLANES + tt, :] = h
    buf[SUBLANES - 1:SUBLANES, :] = jnp.where(pl.program_id(1) == 0, shift_ref[...],
                                              prev_ref[SUBLANES - 1:SUBLANES, :])
    xx = buf[SUBLANES - 1:SUBLANES - 1 + tt, :] - h
    for j in range(o_ref.shape[0]):
        o_ref[j] = (h + xx * mu_ref[j:j + 1, :]).astype(o_ref.dtype)


def rw_mix(h, shift_prev, mu):
    b, t, d = h.shape
    n_mix = mu.shape[0]
    tt = _row_tile(t, 256)
    per = tt // SUBLANES
    return pl.pallas_call(
        _rw_mix_kernel,
        grid=(b, t // tt),
        in_specs=[
            pl.BlockSpec((None, tt, d), lambda bi, i: (bi, i, 0)),
            pl.BlockSpec((None, SUBLANES, d), lambda bi, i: (bi, jnp.maximum(i * per - 1, 0), 0)),
            pl.BlockSpec((None, 1, d), lambda bi, i: (bi, 0, 0)),
            pl.BlockSpec((n_mix, d), lambda bi, i: (0, 0)),
        ],
        out_specs=pl.BlockSpec((n_mix, None, tt, d), lambda bi, i: (0, bi, i, 0)),
        out_shape=jax.ShapeDtypeStruct((n_mix, b, t, d), BF16),
        scratch_shapes=[pltpu.VMEM((SUBLANES + tt, d), F32)],
        compiler_params=_cparams(("arbitrary", "arbitrary")),
        name="rw_mix",
    )(h, h, shift_prev.reshape(b, 1, d), mu)


RW_HEAD = 64


def _seg_sum(x, ones_blk):
    outs = []
    for s in range(x.shape[1] // LANES):
        xs = x[:, s * LANES:(s + 1) * LANES]
        hi = xs.astype(BF16)
        lo = (xs - hi.astype(F32)).astype(BF16)
        outs.append(_dot(hi, ones_blk) + _dot(lo, ones_blk))
    return jnp.concatenate(outs, axis=-1)


def _wkv_kernel(r_ref, k_ref, v_ref, wl_ref, al_ref, g_ref,
                w0_ref, a0_ref, kk_ref, ka_ref, rk_ref, lg_ref, lb_ref, s0_ref,
                o_ref, s_ref, dec_sc, kk_sc, b_sc, kh_sc, y_sc, *, n_steps):
    i = pl.program_id(1)
    tc, d = r_ref.shape
    rb = lax.broadcasted_iota(I32, (LANES, LANES), 0) // RW_HEAD
    cb = lax.broadcasted_iota(I32, (LANES, LANES), 1) // RW_HEAD
    ones_blk = jnp.where(rb == cb, 1.0, 0.0).astype(BF16)
    eye2 = (lax.broadcasted_iota(I32, (RW_HEAD, LANES), 0)
            == lax.broadcasted_iota(I32, (RW_HEAD, LANES), 1) % RW_HEAD)

    @pl.when(i == 0)
    def _():
        s_ref[...] = s0_ref[...]

    k = k_ref[...]
    z = -(w0_ref[...] + wl_ref[...])
    softplus = jnp.maximum(z, 0.0) + jnp.log(1.0 + jnp.exp(-jnp.abs(z)))
    dec_sc[...] = jnp.exp(-jnp.exp(-softplus - 0.5))
    a = jax.nn.sigmoid(a0_ref[...] + al_ref[...])
    kk = k * kk_ref[...]
    kk = kk * lax.rsqrt(jnp.maximum(_seg_sum(kk * kk, ones_blk), 1e-24))
    kk_sc[...] = kk
    b_sc[...] = kk * a
    kh_sc[...] = k * (1.0 + (a - 1.0) * ka_ref[...])
    assert -(-n_steps // SUBLANES) * SUBLANES == tc, "the recurrence blocks must cover the whole time tile"

    steps_per_block = min(SUBLANES, n_steps)

    slabs = [slice(s * LANES, (s + 1) * LANES) for s in range(d // LANES)]
    head_sel = (lax.broadcasted_iota(I32, (SUBLANES, LANES), 0)
                == lax.broadcasted_iota(I32, (SUBLANES, LANES), 1) // RW_HEAD).astype(BF16)

    def block(blk, carry):
        rows = pl.ds(pl.multiple_of(blk * SUBLANES, SUBLANES), SUBLANES)
        st = [s_ref[:, sl] for sl in slabs]
        y_rows = [[] for _ in slabs]
        for j in range(steps_per_block):
            sa, vb = [], []
            for s, sl in enumerate(slabs):
                lhs = jnp.concatenate([st[s] * kk_sc[rows, sl][j:j + 1],
                                       jnp.where(eye2, v_ref[rows, sl][j:j + 1], 0.0)], axis=0)
                both = _dot(lhs.astype(BF16), ones_blk)
                sa.append(both[:RW_HEAD])
                vb.append(both[RW_HEAD:])
            for s, sl in enumerate(slabs):
                st[s] = (st[s] * dec_sc[rows, sl][j:j + 1] - sa[s] * b_sc[rows, sl][j:j + 1]
                         + vb[s] * kh_sc[rows, sl][j:j + 1])
            for s, sl in enumerate(slabs):
                yh = _dot_nt(head_sel, (st[s] * r_ref[rows, sl][j:j + 1]).astype(BF16))
                y_rows[s].append(jnp.concatenate([yh[0:1], yh[1:2]], axis=-1))
        for s, sl in enumerate(slabs):
            s_ref[:, sl] = st[s]
            pad = [jnp.zeros((1, LANES), F32)] * (SUBLANES - steps_per_block)
            y_sc[rows, sl] = jnp.concatenate(y_rows[s] + pad, axis=0)
        return carry

    lax.fori_loop(0, -(-n_steps // SUBLANES), block, 0)

    y = y_sc[...]
    inv_n = 1.0 / RW_HEAD
    yc = y - _seg_sum(y, ones_blk) * inv_n
    var = _seg_sum(yc * yc, ones_blk) * inv_n
    y = yc * lax.rsqrt(var + LNX_EPS) * lg_ref[...] + lb_ref[...]
    r = r_ref[...]
    y = y + _seg_sum(r * kh_sc[...] * rk_ref[...], ones_blk) * v_ref[...]
    o_ref[...] = (y * g_ref[...]).astype(o_ref.dtype)


def wkv(r, k, v, wl, al, g, w0, a0, k_k, k_a, r_k, lnx_g, lnx_b, s0, n_steps):
    b, t, d = r.shape
    tc = _row_tile(t, 128)
    steps = tc if t > tc else n_steps
    row = pl.BlockSpec((None, tc, d), lambda bi, i: (bi, i, 0))
    vec = pl.BlockSpec((1, d), lambda bi, i: (0, 0))
    st = pl.BlockSpec((None, RW_HEAD, d), lambda bi, i: (bi, 0, 0))
    vecs = [x.reshape(1, d) for x in (w0, a0, k_k, k_a, r_k, lnx_g, lnx_b)]
    return pl.pallas_call(
        functools.partial(_wkv_kernel, n_steps=steps),
        grid=(b, t // tc),
        in_specs=[row] * 6 + [vec] * 7 + [st],
        out_specs=[row, st],
        out_shape=[jax.ShapeDtypeStruct((b, t, d), BF16), jax.ShapeDtypeStruct((b, RW_HEAD, d), F32)],
        scratch_shapes=[pltpu.VMEM((tc, d), F32)] * 5,
        compiler_params=_cparams(("arbitrary", "arbitrary")),
        name="wkv",
    )(r, k, v, wl, al, g, *vecs, s0)


WKV_CHUNK = 64
WKV_TILE = 1024
WKV_TILE_MIN = 512


def _split3(x):
    h1 = x.astype(BF16)
    r1 = x - h1.astype(F32)
    h2 = r1.astype(BF16)
    return h1, h2, (r1 - h2.astype(F32)).astype(BF16)


def _wkv_prep_kernel(k_ref, wl_ref, al_ref, w0_ref, a0_ref, kkp_ref, ka_ref, lw_ref, kk_ref, b_ref, kh_ref):
    rb = lax.broadcasted_iota(I32, (LANES, LANES), 0) // RW_HEAD
    cb = lax.broadcasted_iota(I32, (LANES, LANES), 1) // RW_HEAD
    ones_blk = jnp.where(rb == cb, 1.0, 0.0).astype(BF16)
    k = k_ref[...]
    z = -(w0_ref[...] + wl_ref[...])
    softplus = jnp.maximum(z, 0.0) + jnp.log(1.0 + jnp.exp(-jnp.abs(z)))
    lw_ref[...] = -jnp.exp(-softplus - 0.5)
    a = jax.nn.sigmoid(a0_ref[...] + al_ref[...])
    kk = k * kkp_ref[...]
    kk = kk * lax.rsqrt(jnp.maximum(_seg_sum(kk * kk, ones_blk), 1e-24))
    kk_ref[...] = kk
    b_ref[...] = kk * a
    kh_ref[...] = k * (1.0 + (a - 1.0) * ka_ref[...])


def wkv_prep(k, wl, al, w0, a0, k_k, k_a):
    b, t, d = k.shape
    tt = _row_tile(t, 256)
    row = pl.BlockSpec((None, tt, d), lambda bi, i: (bi, i, 0))
    vec = pl.BlockSpec((1, d), lambda bi, i: (0, 0))
    return pl.pallas_call(
        _wkv_prep_kernel,
        grid=(b, t // tt),
        in_specs=[row] * 3 + [vec] * 4,
        out_specs=[row] * 4,
        out_shape=[jax.ShapeDtypeStruct((b, t, d), F32)] * 4,
        compiler_params=_cparams(("arbitrary", "arbitrary")),
        name="wkv_prep",
    )(k, wl, al, *[x.reshape(1, d) for x in (w0, a0, k_k, k_a)])


def _wkv_chunk_kernel(r_ref, lw_ref, kh_ref, v_ref, kk_ref, b_ref, lwt_ref, kht_ref, bt_ref, h0_ref,
                      y_ref, h_ref, *, n_chunks):
    c_ = WKV_CHUNK

    @pl.when(pl.program_id(2) == 0)
    def _():
        h_ref[...] = h0_ref[...]

    row = lax.broadcasted_iota(I32, (c_, c_), 0)
    col = lax.broadcasted_iota(I32, (c_, c_), 1)
    tri_incl = col <= row
    tri_strict = col < row
    l_incl = jnp.where(tri_incl, 1.0, 0.0).astype(BF16)
    u_incl = jnp.where(row <= col, 1.0, 0.0).astype(BF16)
    lane_head = lax.broadcasted_iota(I32, (1, LANES), 1) // RW_HEAD
    row_head = lax.broadcasted_iota(I32, (LANES, 1), 0) // RW_HEAD
    n_heads = LANES // RW_HEAD
    chunks = range(n_chunks)
    items = [(c, hh) for c in chunks for hh in range(n_heads)]
    ch = []
    for c in chunks:
        rs = slice(c * c_, (c + 1) * c_)
        lw, r, kh, v, kk, b = lw_ref[rs, :], r_ref[rs, :], kh_ref[rs, :], v_ref[rs, :], kk_ref[rs, :], b_ref[rs, :]
        cum = sum(_dot(l_incl, part) for part in _split3(lw))
        e_neg = jnp.exp(-cum)
        bt = b * e_neg
        kt = kh * e_neg
        cumt = sum(_dot(part, u_incl) for part in _split3(lwt_ref[c]))
        dt = jnp.exp(cumt[:, c_ - 1:c_] - cumt)
        ch.append(dict(
            kkt=kk * jnp.exp(cum - lw), rt=r * jnp.exp(cum),
            y1=jnp.concatenate([bt, kt], axis=0).astype(BF16), v_bf=v.astype(BF16),
            g_col=jnp.exp(cumt[:, c_ - 1:c_]),
            bht=bt_ref[c] * dt, kht=kht_ref[c] * dt))
    it = {}
    for c, hh in items:
        m = lane_head == hh
        kkt_h = jnp.where(m, ch[c]["kkt"], 0.0)
        rt_h = jnp.where(m, ch[c]["rt"], 0.0)
        gmat = _dot_nt(jnp.concatenate([kkt_h, rt_h], axis=0).astype(BF16), ch[c]["y1"])
        it[c, hh] = dict(
            kkt_h=kkt_h, rt_h=rt_h,
            a_b=jnp.where(tri_strict, gmat[:c_, :c_], 0.0).astype(BF16),
            a_k=jnp.where(tri_strict, gmat[:c_, c_:], 0.0).astype(BF16),
            m_b=jnp.where(tri_incl, gmat[c_:, :c_], 0.0).astype(BF16),
            m_k=jnp.where(tri_incl, gmat[c_:, c_:], 0.0).astype(BF16))
    for c, hh in items:
        d_ = it[c, hh]
        d_["x"] = jnp.concatenate([d_["kkt_h"], _dot(d_["a_k"], ch[c]["v_bf"])], axis=1)
        d_["pows"] = [d_["a_b"]]
    n_levels = c_.bit_length() - 1
    for _ in range(n_levels - 1):
        for key in items:
            pows = it[key]["pows"]
            pows.append(_dot(pows[-1], pows[-1]).astype(BF16))
    eye_c = jnp.where(row == col, 1.0, 0.0)
    for key in items:
        it[key]["t"] = eye_c - it[key]["a_b"].astype(F32)
    for level in range(1, n_levels):
        for key in items:
            d_ = it[key]
            d_["t"] = d_["t"] + _dot(d_["t"].astype(BF16), d_["pows"][level])
    hst = h_ref[...]
    for c in chunks:
        for hh in range(n_heads):
            d_ = it[c, hh]
            m = lane_head == hh
            rmask = row_head == hh
            x_bf = _dot(d_["t"].astype(BF16), d_["x"].astype(BF16)).astype(BF16)
            mbx = _dot(d_["m_b"], x_bf)
            d_["rw"] = d_["rt_h"] - mbx[:, :LANES]
            d_["y0"] = jnp.where(m, _dot(d_["m_k"], ch[c]["v_bf"]) - mbx[:, LANES:], 0.0)
            bx = _dot(jnp.where(rmask, ch[c]["bht"], 0.0).astype(BF16), x_bf)
            d_["pm"] = -bx[:, :LANES]
            d_["qm"] = jnp.where(m, _dot(jnp.where(rmask, ch[c]["kht"], 0.0).astype(BF16), ch[c]["v_bf"])
                                 - bx[:, LANES:], 0.0)
        parts = [it[c, hh] for hh in range(n_heads)]
        h_bf = hst.astype(BF16)
        y_ref[c * c_:(c + 1) * c_, :] = (_dot(sum(p_["rw"] for p_ in parts).astype(BF16), h_bf)
                                         + sum(p_["y0"] for p_ in parts))
        hst = (ch[c]["g_col"] * hst + _dot(sum(p_["pm"] for p_ in parts).astype(BF16), h_bf)
               + sum(p_["qm"] for p_ in parts))
    h_ref[...] = hst


def wkv_chunked(r, lw, kh, v, kk, b_, h0):
    bsz, t, d = r.shape
    n_pairs = d // LANES
    tile = WKV_TILE if t % WKV_TILE == 0 else WKV_TILE_MIN
    per_tile = tile // WKV_CHUNK

    def chunk_major_t(x):
        return jnp.swapaxes(x.reshape(bsz, t // WKV_CHUNK, WKV_CHUNK, d), 2, 3)

    row = pl.BlockSpec((None, tile, LANES), lambda bi, p, i: (bi, i, p))
    row_t = pl.BlockSpec((None, per_tile, LANES, WKV_CHUNK), lambda bi, p, i: (bi, i, p, 0))
    st = pl.BlockSpec((None, None, LANES, LANES), lambda bi, p, i: (bi, p, 0, 0))
    return pl.pallas_call(
        functools.partial(_wkv_chunk_kernel, n_chunks=per_tile),
        grid=(bsz, n_pairs, t // tile),
        in_specs=[row] * 6 + [row_t] * 3 + [st],
        out_specs=[row, st],
        out_shape=[jax.ShapeDtypeStruct((bsz, t, d), F32),
                   jax.ShapeDtypeStruct((bsz, n_pairs, LANES, LANES), F32)],
        compiler_params=_cparams(("arbitrary", "arbitrary", "arbitrary")),
        name="wkv_chunked",
    )(r, lw, kh, v, kk, b_, chunk_major_t(lw), chunk_major_t(kh), chunk_major_t(b_), h0)


def _wkv_post_kernel(y_ref, r_ref, kh_ref, v_ref, g_ref, rk_ref, lg_ref, lb_ref, o_ref):
    rb = lax.broadcasted_iota(I32, (LANES, LANES), 0) // RW_HEAD
    cb = lax.broadcasted_iota(I32, (LANES, LANES), 1) // RW_HEAD
    ones_blk = jnp.where(rb == cb, 1.0, 0.0).astype(BF16)
    y = y_ref[...]
    inv_n = 1.0 / RW_HEAD
    yc = y - _seg_sum(y, ones_blk) * inv_n
    var = _seg_sum(yc * yc, ones_blk) * inv_n
    y = yc * lax.rsqrt(var + LNX_EPS) * lg_ref[...] + lb_ref[...]
    y = y + _seg_sum(r_ref[...] * kh_ref[...] * rk_ref[...], ones_blk) * v_ref[...]
    o_ref[...] = (y * g_ref[...]).astype(o_ref.dtype)


def wkv_post(y, r, kh, v, g, r_k, lnx_g, lnx_b):
    b, t, d = y.shape
    tt = _row_tile(t, 256)
    row = pl.BlockSpec((None, tt, d), lambda bi, i: (bi, i, 0))
    vec = pl.BlockSpec((1, d), lambda bi, i: (0, 0))
    return pl.pallas_call(
        _wkv_post_kernel,
        grid=(b, t // tt),
        in_specs=[row] * 5 + [vec] * 3,
        out_specs=row,
        out_shape=jax.ShapeDtypeStruct((b, t, d), BF16),
        compiler_params=_cparams(("arbitrary", "arbitrary")),
        name="wkv_post",
    )(y, r, kh, v, g, *[x.reshape(1, d) for x in (r_k, lnx_g, lnx_b)])


def _state_to_blockdiag(s):
    st = jnp.swapaxes(s, 2, 3)
    st = st.reshape(s.shape[0], s.shape[1] // 2, 2, RW_HEAD, RW_HEAD)
    z = jnp.zeros_like(st[:, :, 0])
    return jnp.concatenate([jnp.concatenate([st[:, :, 0], z], axis=-1),
                            jnp.concatenate([z, st[:, :, 1]], axis=-1)], axis=-2)


def _state_from_blockdiag(hbd):
    b, p = hbd.shape[:2]
    x = hbd.reshape(b, p, 2, RW_HEAD, 2, RW_HEAD)
    diag = jnp.stack([x[:, :, 0, :, 0, :], x[:, :, 1, :, 1, :]], axis=2)
    return jnp.swapaxes(diag.reshape(b, 2 * p, RW_HEAD, RW_HEAD), 2, 3)


def _state_to_kernel_layout(s):
    b, h, n, _ = s.shape
    return jnp.transpose(s, (0, 2, 1, 3)).reshape(b, n, h * n)


def _state_from_kernel_layout(s, heads):
    b, n, _ = s.shape
    return jnp.transpose(s.reshape(b, n, heads, n), (0, 2, 1, 3))


def _run_group(x, mod_all, t_real, positions, sample_ctx, weights):
    (norm_g, ffn_w1, ffn_w3, ffn_w2, att_w_in, att_w_out, q_norm_g, k_norm_g, idx_k_norm_g,
     conv_w, conv_b, conv_ln_g, conv_ln_b, rw_mu, rw_w0, rw_w1, rw_w2, rw_a0, rw_a1, rw_a2, rw_g1, rw_g2,
     rw_k_k, rw_k_a, rw_r_k, rw_wr, rw_wk, rw_wv, rw_wo, rw_lnx_g, rw_lnx_b) = weights
    b, t, d = x.shape
    depth = norm_g.shape[0]
    aw = att_w_out.shape[1] - conv_w.shape[2]
    att_heads = aw // LANES
    cc = conv_w.shape[2]
    conv_width = conv_w.shape[1]
    idx_dim = idx_k_norm_g.shape[1]
    iq_w = IDX_HEADS * idx_dim
    rw_heads = d // RW_HEAD
    tables = _rope_tables(positions, LANES) + _rope_tables(positions, idx_dim)

    outs = dict(k=[], v=[], ik=[], conv=[], shift=[], wkv=[])
    h = resid_norm(x, mod_all[0], g=norm_g[0, 0], shift_idx=0, scale_idx=1, emit_x=False, h_dtype=BF16)
    for l in range(depth):
        i = l // 2
        mod = mod_all[l]
        even = l % 2 == 0
        y = ffn(h, ffn_w1, ffn_w3, ffn_w2, l, 0)
        x, h = resid_norm(x, mod, y=y, gate_idx=2, coef=HALF_STEP, g=norm_g[l, 1], shift_idx=3, scale_idx=4,
                          h_dtype=BF16 if even else F32)
        if even:
            qkvi = matmul(h, att_w_in, lead=i, n_cols=3 * aw + iq_w)
            w_small = jnp.pad(att_w_in[i][:, 3 * aw + iq_w:3 * aw + iq_w + idx_dim + IDX_HEADS],
                              ((0, 0), (0, LANES - idx_dim - IDX_HEADS)))
            small = matmul(h, w_small)
            u = matmul(h, att_w_in[i][:, 3 * aw + iq_w + idx_dim + IDX_HEADS:])
            q_bf, k_f, k_bf, v_bf, iq_bf, small, ikd_bf, glu = even_post(
                qkvi, small, u, tables, q_norm_g[i], k_norm_g[i], idx_k_norm_g[i], att_heads, idx_dim)
            if sample_ctx is None:
                att = dsa_prompt(q_bf, iq_bf, small, k_bf, v_bf, ikd_bf, idx_dim)
                hist = jnp.zeros((b, HIST_ROWS, cc), F32)
                outs["conv"].append(glu[:, t - (conv_width - 1):])
            else:
                pt = sample_ctx["page_table"]
                scores = sample_idx_scores(pt, iq_bf, small, sample_ctx["cache_idx_k"], i)
                pad_rows = lambda a: jnp.pad(a, ((0, 0), (0, LANES - t), (0, 0)))
                att = sample_attention(pt, scores, q_bf, iq_bf, small, pad_rows(ikd_bf), k_f,
                                       qkvi[:, :, 2 * aw:3 * aw], sample_ctx["cache_k"], sample_ctx["cache_v"],
                                       i, t_real, idx_dim)
                state = sample_ctx["state_conv"][i]
                hist = jnp.pad(state, ((0, 0), (HIST_ROWS - (conv_width - 1), 0), (0, 0)))
                outs["conv"].append(jnp.concatenate([state, glu[:, :t_real]], axis=1)[:, -(conv_width - 1):])
            conv_y = conv_branch(glu, hist, conv_w[i], conv_b[i], conv_ln_g[i], conv_ln_b[i])
            mixed = matmul(jnp.concatenate([att, conv_y], axis=-1), att_w_out, lead=i)
            outs["k"].append(k_f[:, :t_real].reshape(b, t_real, att_heads, LANES))
            outs["v"].append(qkvi[:, :t_real, 2 * aw:3 * aw].reshape(b, t_real, att_heads, LANES))
            outs["ik"].append(small[:, :t_real, :idx_dim])
        else:
            if sample_ctx is None:
                shift_prev = jnp.zeros((b, d), F32)
                s_init = jnp.zeros((b, rw_heads, RW_HEAD, RW_HEAD), F32)
            else:
                shift_prev = sample_ctx["state_shift"][i]
                s_init = sample_ctx["state_wkv"][i]
            xs = rw_mix(h, shift_prev, rw_mu[i])
            r = matmul(xs[0], rw_wr, lead=i)
            wl = lora(xs[1], rw_w1, rw_w2, i, "tanh")
            k = matmul(xs[2], rw_wk, lead=i)
            v = matmul(xs[3], rw_wv, lead=i)
            al = lora(xs[4], rw_a1, rw_a2, i, "none")
            g = lora(xs[5], rw_g1, rw_g2, i, "sigmoid")
            if t == t_real and t % WKV_TILE_MIN == 0:
                lw, kk, b_, kh = wkv_prep(k, wl, al, rw_w0[i], rw_a0[i], rw_k_k[i], rw_k_a[i])
                y, h_fin = wkv_chunked(r, lw, kh, v, kk, b_, _state_to_blockdiag(s_init))
                yg = wkv_post(y, r, kh, v, g, rw_r_k[i], rw_lnx_g[i], rw_lnx_b[i])
                s_out = _state_from_blockdiag(h_fin)
            else:
                yg, s_fin = wkv(r, k, v, wl, al, g, rw_w0[i], rw_a0[i], rw_k_k[i], rw_k_a[i], rw_r_k[i],
                                rw_lnx_g[i], rw_lnx_b[i], _state_to_kernel_layout(s_init), t_real)
                s_out = _state_from_kernel_layout(s_fin, rw_heads)
            mixed = matmul(yg, rw_wo, lead=i)
            outs["shift"].append(h[:, t_real - 1])
            outs["wkv"].append(s_out)
        x, h = resid_norm(x, mod, y=mixed, gate_idx=5, coef=1.0, g=norm_g[l, 2], shift_idx=6, scale_idx=7,
                          h_dtype=BF16)
        y = ffn(h, ffn_w1, ffn_w3, ffn_w2, l, 1)
        if l + 1 < depth:
            x, h = resid_norm(x, mod, y=y, gate_idx=8, coef=HALF_STEP, g=norm_g[l + 1, 0],
                              shift_idx=0, scale_idx=1, h_dtype=BF16, mod_norm=mod_all[l + 1])
        else:
            x = resid_norm(x, mod, y=y, gate_idx=8, coef=HALF_STEP)
    return x[:, :t_real], outs


def kernel(x_prompt, x_sample, cache_k, cache_v, cache_idx_k, state_conv, state_shift, state_wkv, page_table, c_prompt, c_sample, norm_g, ada_w, ada_b, ffn_w1, ffn_w3, ffn_w2, att_w_in, att_w_out, q_norm_g, k_norm_g, idx_k_norm_g, conv_w, conv_b, conv_ln_g, conv_ln_b, rw_mu, rw_w0, rw_w1, rw_w2, rw_a0, rw_a1, rw_a2, rw_g1, rw_g2, rw_k_k, rw_k_a, rw_r_k, rw_wr, rw_wk, rw_wv, rw_wo, rw_lnx_g, rw_lnx_b):
    bp, tp, d = x_prompt.shape
    bs, ts, _ = x_sample.shape
    depth = norm_g.shape[0]
    past = page_table.shape[1] * cache_k.shape[2]

    n_c = bp + bs
    c_rows = -(-n_c // 16) * 16
    c_all = jnp.pad(jnp.concatenate([c_prompt, c_sample], axis=0), ((0, c_rows - n_c), (0, 0)))
    mod = adaln_all(c_all, ada_w, ada_b).reshape(depth, c_rows, N_MOD, 1, d)
    mod_p, mod_s = mod[:, :bp], mod[:, bp:n_c]

    ffn_w1, ffn_w3, ffn_w2 = (w.astype(BF16) for w in (ffn_w1, ffn_w3, ffn_w2))
    weights = (norm_g, ffn_w1, ffn_w3, ffn_w2, att_w_in, att_w_out, q_norm_g, k_norm_g, idx_k_norm_g,
               conv_w, conv_b, conv_ln_g, conv_ln_b, rw_mu, rw_w0, rw_w1, rw_w2, rw_a0, rw_a1, rw_a2,
               rw_g1, rw_g2, rw_k_k, rw_k_a, rw_r_k.reshape(rw_r_k.shape[0], -1), rw_wr, rw_wk, rw_wv, rw_wo,
               rw_lnx_g, rw_lnx_b)

    yp, op = _run_group(x_prompt, mod_p, tp, np.arange(tp), None, weights)
    xs_pad = jnp.pad(x_sample, ((0, 0), (0, SAMPLE_T_PAD - ts), (0, 0)))
    sample_ctx = dict(page_table=page_table, cache_k=cache_k, cache_v=cache_v, cache_idx_k=cache_idx_k,
                      state_conv=state_conv, state_shift=state_shift, state_wkv=state_wkv)
    ys, os_ = _run_group(xs_pad, mod_s, ts, past + np.arange(SAMPLE_T_PAD), sample_ctx, weights)

    st = lambda xs: jnp.stack(xs)
    return (yp, ys,
            st(op["k"]), st(op["v"]), st(op["ik"]), st(op["conv"]), st(op["shift"]), st(op["wkv"]),
            st(os_["k"]), st(os_["v"]), st(os_["ik"]), st(os_["conv"]), st(os_["shift"]), st(os_["wkv"]))
```
